```python
import jax, jax.numpy as jnp
from jax import lax
import numpy as np

D_MODEL = 1024
BATCH = 2
SEQ = 8192
DEPTH = 2
DEC_BATCH = 16
DEC_SEQ = 32
PAST_LEN = 4096

CHUNK = 64
N_MIXERS = 2
N_A_LAYERS = (DEPTH + N_MIXERS - 1) // N_MIXERS
N_B_LAYERS = DEPTH // N_MIXERS
HEAD_DIM = 64
A_HEADS = D_MODEL // HEAD_DIM
A_KV_HEADS = 4
A_GROUP = A_HEADS // A_KV_HEADS
WINDOW = 128
LOOKBACK = WINDOW // CHUNK
ROT_DIM = HEAD_DIM // 4
ROPE_THETA = 500000.0
B_HEADS = D_MODEL // HEAD_DIM
Q_BLOCK = 128
N_GROUPS = 4
EXPERTS_PER_GROUP = 8
N_EXPERTS = N_GROUPS * EXPERTS_PER_GROUP
TOP_K = 2
D_EXPERT = D_MODEL // 2
MOE_BLOCK = 128
RMS_EPS = 1e-6
NEG_INF = -1e30

kernel_name = 'hybrid_swa_stickbreak_hmoe_step'


def rmsnorm(x, g):
    xf = x.astype(jnp.float32)
    y = xf * lax.rsqrt(jnp.mean(xf * xf, axis=-1, keepdims=True) + RMS_EPS)
    return (y * g.astype(jnp.float32)).astype(x.dtype)


def partial_rope(x, pos):
    inv = ROPE_THETA ** (-jnp.arange(0, ROT_DIM, 2, dtype=jnp.float32) / ROT_DIM)
    ang = pos.astype(jnp.float32)[:, None] * inv[None, :]
    cos = jnp.cos(ang)[None, :, None, :].astype(x.dtype)
    sin = jnp.sin(ang)[None, :, None, :].astype(x.dtype)
    half = ROT_DIM // 2
    x1 = x[..., :half]
    x2 = x[..., half:ROT_DIM]
    return jnp.concatenate([x1 * cos - x2 * sin, x2 * cos + x1 * sin, x[..., ROT_DIM:]], axis=-1)


def project_a(h, w_qkv):
    b, s, _ = h.shape
    qkv = h @ w_qkv
    nq = A_HEADS * HEAD_DIM
    nk = A_KV_HEADS * HEAD_DIM
    q = qkv[..., :nq].reshape(b, s, A_HEADS, HEAD_DIM)
    k = qkv[..., nq:nq + nk].reshape(b, s, A_KV_HEADS, HEAD_DIM)
    v = qkv[..., nq + nk:].reshape(b, s, A_KV_HEADS, HEAD_DIM)
    return q, k, v


def sink_attend(q, k, v, sinks, valid):
    s = jnp.einsum('bnqkgd,bnskd->bnkgqs', q, k).astype(jnp.float32) * (HEAD_DIM ** -0.5)
    s = jnp.where(valid[None, :, None, None, None, :], s, NEG_INF)
    sink = jnp.broadcast_to(sinks.astype(jnp.float32).reshape(1, 1, A_KV_HEADS, A_GROUP, 1, 1), s.shape[:-1] + (1,))
    p = jax.nn.softmax(jnp.concatenate([s, sink], axis=-1), axis=-1)[..., :-1]
    return jnp.einsum('bnkgqs,bnskd->bnqkgd', p.astype(v.dtype), v)


def swa_prompt(h, w_qkv, sinks):
    b, s, _ = h.shape
    nc = s // CHUNK
    q, k, v = project_a(h, w_qkv)
    pos = jnp.arange(s)
    q = partial_rope(q, pos)
    k = partial_rope(k, pos)

    def band(t):
        tp = jnp.pad(t, ((0, 0), (WINDOW, 0), (0, 0), (0, 0))).reshape(b, nc + LOOKBACK, CHUNK, A_KV_HEADS, HEAD_DIM)
        return jnp.concatenate([tp[:, j:j + nc] for j in range(LOOKBACK + 1)], axis=2)

    kpos = jnp.arange(nc)[:, None] * CHUNK - WINDOW + jnp.arange((LOOKBACK + 1) * CHUNK)[None, :]
    o = sink_attend(q.reshape(b, nc, CHUNK, A_KV_HEADS, A_GROUP, HEAD_DIM), band(k), band(v), sinks, kpos >= 0)
    return o.reshape(b, s, A_HEADS * HEAD_DIM), k[:, -WINDOW:], v[:, -WINDOW:]


def swa_sample(h, ck, cv, w_qkv, sinks):
    b, t, _ = h.shape
    q, k, v = project_a(h, w_qkv)
    pos = PAST_LEN + jnp.arange(t)
    q = partial_rope(q, pos)
    k = partial_rope(k, pos)
    kk = jnp.concatenate([ck, k], axis=1)
    vv = jnp.concatenate([cv, v], axis=1)
    valid = jnp.ones((1, kk.shape[1]), dtype=bool)
    o = sink_attend(q.reshape(b, 1, t, A_KV_HEADS, A_GROUP, HEAD_DIM), kk[:, None], vv[:, None], sinks, valid)
    return o.reshape(b, t, A_HEADS * HEAD_DIM), kk[:, -WINDOW:], vv[:, -WINDOW:]


def project_b(h, w_qkv):
    b, s, _ = h.shape
    qkv = (h @ w_qkv).reshape(b, s, 3, B_HEADS, HEAD_DIM)
    return qkv[:, :, 0], qkv[:, :, 1], qkv[:, :, 2]


def stick_breaking(q, k, v, q_pos, k_pos):
    z = jnp.einsum('bqhd,bshd->bhqs', q, k).astype(jnp.float32) * (HEAD_DIM ** -0.5)
    before = k_pos[None, :] < q_pos[:, None]
    log_keep = jnp.where(before, jax.nn.log_sigmoid(-z), 0.0)
    suffix = lax.cumsum(log_keep, axis=3, reverse=True) - log_keep
    a = jnp.where(before, jnp.exp(jax.nn.log_sigmoid(z) + suffix), 0.0)
    return jnp.einsum('bhqs,bshd->bqhd', a.astype(v.dtype), v)


def sb_prompt(h, w_qkv):
    b, s, _ = h.shape
    q, k, v = project_b(h, w_qkv)
    pos = jnp.arange(s)
    outs = [stick_breaking(q[:, s0:s0 + Q_BLOCK], k[:, :s0 + Q_BLOCK], v[:, :s0 + Q_BLOCK],
                           pos[s0:s0 + Q_BLOCK], pos[:s0 + Q_BLOCK])
            for s0 in range(0, s, Q_BLOCK)]
    o = jnp.concatenate(outs, axis=1)
    return o.reshape(b, s, B_HEADS * HEAD_DIM), k, v


def sb_sample(h, ck, cv, w_qkv):
    b, t, _ = h.shape
    q, k, v = project_b(h, w_qkv)
    kk = jnp.concatenate([ck, k], axis=1)
    vv = jnp.concatenate([cv, v], axis=1)
    o = stick_breaking(q, kk, vv, PAST_LEN + jnp.arange(t), jnp.arange(PAST_LEN + t))
    return o.reshape(b, t, B_HEADS * HEAD_DIM), k, v


def hier_moe(x, w_group, w_router, w_gate, w_up, w_down):
    n_tok = x.shape[0]
    n_assign = n_tok * TOP_K
    g_logits = (x @ w_group).astype(jnp.float32)
    g_idx = jnp.argmax(g_logits, axis=-1)
    g_p = jnp.take_along_axis(jax.nn.softmax(g_logits, axis=-1), g_idx[:, None], axis=-1)[:, 0]
    r_logits = jnp.einsum('td,gde->tge', x, w_router)
    sel = jnp.take_along_axis(r_logits, g_idx[:, None, None], axis=1)[:, 0].astype(jnp.float32)
    top_p, top_i = lax.top_k(jax.nn.softmax(sel, axis=-1), TOP_K)
    gate_w = g_p[:, None] * top_p / jnp.sum(top_p, axis=-1, keepdims=True)
    flat_e = (g_idx[:, None] * EXPERTS_PER_GROUP + top_i).reshape(-1).astype(jnp.int32)
    flat_tok = jnp.repeat(jnp.arange(n_tok, dtype=jnp.int32), TOP_K)
    flat_w = gate_w.reshape(-1)
    order = jnp.argsort(flat_e, stable=True)
    e_s = flat_e[order]
    tok_s = flat_tok[order]
    w_s = flat_w[order]
    counts = jnp.bincount(flat_e, length=N_EXPERTS).astype(jnp.int32)
    starts = jnp.cumsum(counts) - counts
    padded = (counts + MOE_BLOCK - 1) // MOE_BLOCK * MOE_BLOCK
    pends = jnp.cumsum(padded)
    dest = pends[e_s] - padded[e_s] + jnp.arange(n_assign, dtype=jnp.int32) - starts[e_s]
    n_blocks = -(-n_assign // MOE_BLOCK) + N_EXPERTS
    buf = jnp.zeros((n_blocks * MOE_BLOCK, x.shape[1]), x.dtype).at[dest].set(x[tok_s])
    block_e = jnp.minimum(jnp.searchsorted(pends, jnp.arange(n_blocks, dtype=jnp.int32) * MOE_BLOCK, side='right'), N_EXPERTS - 1)

    def expert_block(args):
        xb, e = args
        return (jax.nn.silu(xb @ w_gate[e]) * (xb @ w_up[e])) @ w_down[e]

    yb = lax.map(expert_block, (buf.reshape(n_blocks, MOE_BLOCK, x.shape[1]), block_e))
    yb = yb.reshape(n_blocks * MOE_BLOCK, x.shape[1])
    return jnp.zeros_like(x).at[tok_s].add(yb[dest] * w_s[:, None].astype(x.dtype))


def channel_mix(h, g, w_group, w_router, w_gate, w_up, w_down):
    b, s, d = h.shape
    y = hier_moe(rmsnorm(h, g).reshape(b * s, d), w_group, w_router, w_gate, w_up, w_down)
    return y.reshape(b, s, d)


def setup_inputs(seed: int = 0) -> dict:
    key = jax.random.key(seed)
    ks = jax.random.split(key, 22)

    def nrm(k, shape, scale):
        return jax.random.normal(k, shape, jnp.float32) * scale

    qkv_a = (A_HEADS + 2 * A_KV_HEADS) * HEAD_DIM
    return {
        'x_prompt': nrm(ks[0], (BATCH, SEQ, D_MODEL), 1.0),
        'x_sample': nrm(ks[1], (DEC_BATCH, DEC_SEQ, D_MODEL), 1.0),
        'cache_a_k': nrm(ks[2], (N_A_LAYERS, DEC_BATCH, WINDOW, A_KV_HEADS, HEAD_DIM), 1.0),
        'cache_a_v': nrm(ks[3], (N_A_LAYERS, DEC_BATCH, WINDOW, A_KV_HEADS, HEAD_DIM), 1.0),
        'cache_b_k': nrm(ks[4], (N_B_LAYERS, DEC_BATCH, PAST_LEN, B_HEADS, HEAD_DIM), 1.0),
        'cache_b_v': nrm(ks[5], (N_B_LAYERS, DEC_BATCH, PAST_LEN, B_HEADS, HEAD_DIM), 1.0),
        'norm_mix': 1.0 + nrm(ks[6], (DEPTH, D_MODEL), 0.01),
        'norm_ffn': 1.0 + nrm(ks[7], (DEPTH, D_MODEL), 0.01),
        'norm_final': 1.0 + nrm(ks[8], (D_MODEL,), 0.01),
        'a_w_qkv': nrm(ks[9], (N_A_LAYERS, D_MODEL, qkv_a), D_MODEL ** -0.5),
        'a_w_o': nrm(ks[10], (N_A_LAYERS, A_HEADS * HEAD_DIM, D_MODEL), (A_HEADS * HEAD_DIM) ** -0.5),
        'a_sinks': nrm(ks[11], (N_A_LAYERS, A_HEADS), 0.5),
        'b_w_qkv': nrm(ks[12], (N_B_LAYERS, D_MODEL, 3 * B_HEADS * HEAD_DIM), D_MODEL ** -0.5),
        'b_w_o': nrm(ks[13], (N_B_LAYERS, B_HEADS * HEAD_DIM, D_MODEL), (B_HEADS * HEAD_DIM) ** -0.5),
        'moe_w_group': nrm(ks[14], (DEPTH, D_MODEL, N_GROUPS), D_MODEL ** -0.5),
        'moe_w_router': nrm(ks[15], (DEPTH, N_GROUPS, D_MODEL, EXPERTS_PER_GROUP), D_MODEL ** -0.5),
        'moe_w_gate': nrm(ks[16], (DEPTH, N_EXPERTS, D_MODEL, D_EXPERT), D_MODEL ** -0.5),
        'moe_w_up': nrm(ks[17], (DEPTH, N_EXPERTS, D_MODEL, D_EXPERT), D_MODEL ** -0.5),
        'moe_w_down': nrm(ks[18], (DEPTH, N_EXPERTS, D_EXPERT, D_MODEL), D_EXPERT ** -0.5),
    }


def reference(x_prompt, x_sample, cache_a_k, cache_a_v, cache_b_k, cache_b_v, norm_mix, norm_ffn, norm_final,
              a_w_qkv, a_w_o, a_sinks, b_w_qkv, b_w_o, moe_w_group, moe_w_router, moe_w_gate, moe_w_up, moe_w_down):
    hp = x_prompt
    hs = x_sample
    a_kp, a_vp, a_ks, a_vs = [], [], [], []
    b_kp, b_vp, b_ks, b_vs = [], [], [], []
    for i in range(DEPTH):
        j = i // N_MIXERS
        up = rmsnorm(hp, norm_mix[i])
        us = rmsnorm(hs, norm_mix[i])
        if i % N_MIXERS == 0:
            op, kp, vp = swa_prompt(up, a_w_qkv[j], a_sinks[j])
            osm, ksm, vsm = swa_sample(us, cache_a_k[j], cache_a_v[j], a_w_qkv[j], a_sinks[j])
            a_kp.append(kp)
            a_vp.append(vp)
            a_ks.append(ksm)
            a_vs.append(vsm)
            w_o = a_w_o[j]
        else:
            op, kp, vp = sb_prompt(up, b_w_qkv[j])
            osm, ksm, vsm = sb_sample(us, cache_b_k[j], cache_b_v[j], b_w_qkv[j])
            b_kp.append(kp)
            b_vp.append(vp)
            b_ks.append(ksm)
            b_vs.append(vsm)
            w_o = b_w_o[j]
        hp = hp + op @ w_o
        hs = hs + osm @ w_o
        hp = hp + channel_mix(hp, norm_ffn[i], moe_w_group[i], moe_w_router[i], moe_w_gate[i], moe_w_up[i], moe_w_down[i])
        hs = hs + channel_mix(hs, norm_ffn[i], moe_w_group[i], moe_w_router[i], moe_w_gate[i], moe_w_up[i], moe_w_down[i])
    y_prompt = rmsnorm(hp, norm_final)
    y_sample = rmsnorm(hs, norm_final)
    new_a_k_prompt = jnp.stack(a_kp)
    new_a_v_prompt = jnp.stack(a_vp)
    new_a_k_sample = jnp.stack(a_ks)
    new_a_v_sample = jnp.stack(a_vs)
    new_b_k_prompt = jnp.stack(b_kp)
    new_b_v_prompt = jnp.stack(b_vp)
    new_b_k_sample = jnp.stack(b_ks)
    new_b_v_sample = jnp.stack(b_vs)
    return (y_prompt, y_sample, new_a_k_prompt, new_a_v_prompt, new_a_k_sample, new_a_v_sample,
            new_b_k_prompt, new_b_v_prompt, new_b_k_sample, new_b_v_sample)
```

```python
import functools

import jax
import jax.numpy as jnp
import numpy as np
from jax import lax
from jax.experimental import pallas as pl
from jax.experimental.pallas import tpu as pltpu

F32 = jnp.float32
BF16 = jnp.bfloat16

D_MODEL = 1024
HEAD_DIM = 64
N_HEADS = D_MODEL // HEAD_DIM
A_KV_HEADS = 4
A_GROUP = N_HEADS // A_KV_HEADS
A_KV_DIM = A_KV_HEADS * HEAD_DIM
CHUNK = 64
WINDOW = 128
ROT_DIM = HEAD_DIM // 4
ROPE_THETA = 500000.0
N_GROUPS = 4
EXPERTS_PER_GROUP = 8
N_EXPERTS = N_GROUPS * EXPERTS_PER_GROUP
TOP_K = 2
D_EXPERT = D_MODEL // 2
RMS_EPS = 1e-6
NEG_INF = -1e30
SCALE = HEAD_DIM ** -0.5

LANES = 128
ROW_TILE = 512
ATTN_A_BLOCK = WINDOW
SB_BLOCK = 256
EXPERT_BLOCK = 256
COMBINE_TILE = 256
VMEM_LIMIT = 48 * 1024 * 1024


def _params(n_axes, **kw):
    return pltpu.CompilerParams(dimension_semantics=("arbitrary",) * n_axes,
                                vmem_limit_bytes=VMEM_LIMIT, **kw)


def _rmsnorm(x, g):
    return x * lax.rsqrt(jnp.mean(x * x, axis=-1, keepdims=True) + RMS_EPS) * g


def _proj_a_kernel(x_ref, g_ref, w_ref, cs_ref, o_ref):
    xn = _rmsnorm(x_ref[...], g_ref[...]).astype(BF16)
    cos = cs_ref[:, :LANES]
    sin = cs_ref[:, LANES:]
    lane = lax.broadcasted_iota(jnp.int32, cos.shape, 1) % HEAD_DIM
    first_half = lane < ROT_DIM // 2
    n_rot = (D_MODEL + A_KV_DIM) // LANES
    n_all = o_ref.shape[1] // LANES
    for j in range(n_all):
        cols = slice(j * LANES, (j + 1) * LANES)
        blk = jnp.dot(xn, w_ref[:, cols], preferred_element_type=F32)
        if j < n_rot:
            partner = jnp.where(first_half, pltpu.roll(blk, LANES - ROT_DIM // 2, 1),
                                pltpu.roll(blk, ROT_DIM // 2, 1))
            blk = blk * cos + partner * sin
        o_ref[:, cols] = blk


def _rope_table(pos):
    half = ROT_DIM // 2
    inv = ROPE_THETA ** (-jnp.arange(0, ROT_DIM, 2, dtype=F32) / ROT_DIM)
    ang = pos.astype(F32)[:, None] * inv[None, :]
    cos = jnp.cos(ang)
    sin = jnp.sin(ang)
    n = pos.shape[0]
    ones = jnp.ones((n, HEAD_DIM - ROT_DIM), F32)
    zeros = jnp.zeros((n, HEAD_DIM - ROT_DIM), F32)
    cos_head = jnp.concatenate([cos, cos, ones], axis=1)
    sin_head = jnp.concatenate([-sin, sin, zeros], axis=1)
    assert cos_head.shape[1] == HEAD_DIM and half * 2 == ROT_DIM
    return jnp.concatenate([cos_head, cos_head, sin_head, sin_head], axis=1)


def _proj_a(x, g, w_bf, cs):
    n = x.shape[0]
    tm = min(ROW_TILE, n)
    n_out = w_bf.shape[1]
    return pl.pallas_call(
        _proj_a_kernel,
        grid=(n // tm,),
        in_specs=[pl.BlockSpec((tm, D_MODEL), lambda i: (i, 0)),
                  pl.BlockSpec((1, D_MODEL), lambda i: (0, 0)),
                  pl.BlockSpec((D_MODEL, n_out), lambda i: (0, 0)),
                  pl.BlockSpec((tm, 2 * LANES), lambda i: (i, 0))],
        out_specs=pl.BlockSpec((tm, n_out), lambda i: (i, 0)),
        out_shape=jax.ShapeDtypeStruct((n, n_out), F32),
        compiler_params=_params(1),
        name="proj_a",
    )(x, g, w_bf, cs)


def _sink_attention(q, k_bf, v_bf, sink_ref, valid):
    outs = []
    for h in range(N_HEADS):
        g = h // A_GROUP
        qh = q[:, h * HEAD_DIM:(h + 1) * HEAD_DIM].astype(BF16)
        kh = k_bf[:, g * HEAD_DIM:(g + 1) * HEAD_DIM]
        vh = v_bf[:, g * HEAD_DIM:(g + 1) * HEAD_DIM]
        s = lax.dot_general(qh, kh, (((1,), (1,)), ((), ())), preferred_element_type=F32) * SCALE
        if valid is not None:
            s = jnp.where(valid, s, NEG_INF)
        sink = sink_ref[h]
        m = jnp.maximum(jnp.max(s, axis=-1, keepdims=True), sink)
        e = jnp.exp(s - m)
        den = jnp.sum(e, axis=-1, keepdims=True) + jnp.exp(sink - m)
        o = jnp.dot(e.astype(BF16), vh, preferred_element_type=F32) / den
        outs.append(o)
    return jnp.concatenate(outs, axis=1).astype(BF16)


def _attn_a_prompt_kernel(sink_ref, q_ref, kp_ref, kc_ref, vp_ref, vc_ref, o_ref, *, blocks_per_seq):
    i = pl.program_id(0)
    has_prev = (i % blocks_per_seq) != 0
    k = jnp.concatenate([kp_ref[...], kc_ref[...]], axis=0).astype(BF16)
    v = jnp.concatenate([vp_ref[...], vc_ref[...]], axis=0).astype(BF16)
    rows, keys = ATTN_A_BLOCK, 2 * ATTN_A_BLOCK
    q_chunk = lax.broadcasted_iota(jnp.int32, (rows, keys), 0) // CHUNK
    col = lax.broadcasted_iota(jnp.int32, (rows, keys), 1)
    k_chunk = col // CHUNK
    valid = (k_chunk >= q_chunk) & (k_chunk <= q_chunk + WINDOW // CHUNK)
    valid = valid & ((col >= ATTN_A_BLOCK) | has_prev)
    o_ref[...] = _sink_attention(q_ref[...], k, v, sink_ref, valid)


def _attn_a_prompt(qkv, sinks, seq):
    n = qkv.shape[0]
    blk = ATTN_A_BLOCK
    blocks_per_seq = seq // blk
    kcol = D_MODEL // A_KV_DIM
    prev = lambda i: jnp.maximum(i - 1, 0)
    return pl.pallas_call(
        functools.partial(_attn_a_prompt_kernel, blocks_per_seq=blocks_per_seq),
        grid=(n // blk,),
        in_specs=[pl.BlockSpec(memory_space=pltpu.SMEM),
                  pl.BlockSpec((blk, D_MODEL), lambda i: (i, 0)),
                  pl.BlockSpec((blk, A_KV_DIM), lambda i: (prev(i), kcol)),
                  pl.BlockSpec((blk, A_KV_DIM), lambda i: (i, kcol)),
                  pl.BlockSpec((blk, A_KV_DIM), lambda i: (prev(i), kcol + 1)),
                  pl.BlockSpec((blk, A_KV_DIM), lambda i: (i, kcol + 1))],
        out_specs=pl.BlockSpec((blk, D_MODEL), lambda i: (i, 0)),
        out_shape=jax.ShapeDtypeStruct((n, D_MODEL), BF16),
        compiler_params=_params(1),
        name="attn_a_prompt",
    )(sinks, qkv, qkv, qkv, qkv, qkv)


def _attn_a_sample_kernel(sink_ref, q_ref, kn_ref, vn_ref, ck_ref, cv_ref, o_ref):
    k = jnp.concatenate([ck_ref[0], kn_ref[...]], axis=0).astype(BF16)
    v = jnp.concatenate([cv_ref[0], vn_ref[...]], axis=0).astype(BF16)
    o_ref[...] = _sink_attention(q_ref[...], k, v, sink_ref, None)


def _attn_a_sample(qkv, cache_k, cache_v, sinks):
    streams = cache_k.shape[0]
    t = qkv.shape[0] // streams
    kcol = D_MODEL // A_KV_DIM
    return pl.pallas_call(
        _attn_a_sample_kernel,
        grid=(streams,),
        in_specs=[pl.BlockSpec(memory_space=pltpu.SMEM),
                  pl.BlockSpec((t, D_MODEL), lambda b: (b, 0)),
                  pl.BlockSpec((t, A_KV_DIM), lambda b: (b, kcol)),
                  pl.BlockSpec((t, A_KV_DIM), lambda b: (b, kcol + 1)),
                  pl.BlockSpec((1, WINDOW, A_KV_DIM), lambda b: (b, 0, 0)),
                  pl.BlockSpec((1, WINDOW, A_KV_DIM), lambda b: (b, 0, 0))],
        out_specs=pl.BlockSpec((t, D_MODEL), lambda b: (b, 0)),
        out_shape=jax.ShapeDtypeStruct((qkv.shape[0], D_MODEL), BF16),
        compiler_params=_params(1),
        name="attn_a_sample",
    )(sinks, qkv, qkv, qkv, cache_k, cache_v)


def _proj_b_kernel(x_ref, g_ref, w_ref, q_ref, k_ref, v_ref, kb_ref, vb_ref):
    xn = _rmsnorm(x_ref[...], g_ref[...]).astype(BF16)
    for j in range(D_MODEL // LANES):
        cols = slice(j * LANES, (j + 1) * LANES)
        q = jnp.dot(xn, w_ref[:, cols], preferred_element_type=F32)
        q_ref[:, cols] = (q * SCALE).astype(BF16)
        k = jnp.dot(xn, w_ref[:, D_MODEL + j * LANES:D_MODEL + (j + 1) * LANES], preferred_element_type=F32)
        k_ref[:, cols] = k
        kb_ref[:, cols] = k.astype(BF16)
        v = jnp.dot(xn, w_ref[:, 2 * D_MODEL + j * LANES:2 * D_MODEL + (j + 1) * LANES],
                    preferred_element_type=F32)
        v_ref[:, cols] = v
        vb_ref[:, cols] = v.astype(BF16)


def _proj_b(x, g, w_bf):
    n = x.shape[0]
    tm = min(ROW_TILE, n)
    row = lambda i: (i, 0)
    return pl.pallas_call(
        _proj_b_kernel,
        grid=(n // tm,),
        in_specs=[pl.BlockSpec((tm, D_MODEL), row),
                  pl.BlockSpec((1, D_MODEL), lambda i: (0, 0)),
                  pl.BlockSpec((D_MODEL, 3 * D_MODEL), lambda i: (0, 0))],
        out_specs=[pl.BlockSpec((tm, D_MODEL), row)] * 5,
        out_shape=[jax.ShapeDtypeStruct((n, D_MODEL), BF16),
                   jax.ShapeDtypeStruct((n, D_MODEL), F32),
                   jax.ShapeDtypeStruct((n, D_MODEL), F32),
                   jax.ShapeDtypeStruct((n, D_MODEL), BF16),
                   jax.ShapeDtypeStruct((n, D_MODEL), BF16)],
        compiler_params=_params(1),
        name="proj_b",
    )(x, g, w_bf)


def _strict_lower_ones(n):
    j = lax.broadcasted_iota(jnp.int32, (n, n), 0)
    s = lax.broadcasted_iota(jnp.int32, (n, n), 1)
    return jnp.where(j > s, 1.0, 0.0).astype(BF16)


def _sb_head(qh, kh, vh, tri, carry, before):
    n_keys = kh.shape[0]
    z = lax.dot_general(qh, kh, (((1,), (1,)), ((), ())), preferred_element_type=F32)
    l1p = jnp.log1p(jnp.exp(-jnp.abs(z)))
    log_keep = -(jnp.maximum(z, 0.0) + l1p)
    log_beta = jnp.minimum(z, 0.0) - l1p
    if before is not None:
        log_keep = jnp.where(before, log_keep, 0.0)
    hi = log_keep.astype(BF16)
    lo = (log_keep - hi.astype(F32)).astype(BF16)
    suffix = jnp.dot(hi, tri, preferred_element_type=F32) + jnp.dot(lo, tri, preferred_element_type=F32)
    reps = -(-n_keys // LANES)
    later = jnp.concatenate([carry] * reps, axis=1)[:, :n_keys]
    a = jnp.exp(log_beta + suffix + later)
    if before is not None:
        a = jnp.where(before, a, 0.0)
    out = jnp.dot(a.astype(BF16), vh, preferred_element_type=F32)
    new_carry = carry + (suffix[:, 0:1] + log_keep[:, 0:1])
    return out, new_carry


def _sb_block(q_bf, k_bf, v_bf, tri, carry_ref, acc_ref, before):
    for pair in range(N_HEADS // 2):
        outs = []
        for h in (2 * pair, 2 * pair + 1):
            cols = slice(h * HEAD_DIM, (h + 1) * HEAD_DIM)
            o, c = _sb_head(q_bf[:, cols], k_bf[:, cols], v_bf[:, cols], tri, carry_ref[h], before)
            carry_ref[h] = c
            outs.append(o)
        cols = slice(pair * LANES, (pair + 1) * LANES)
        acc_ref[:, cols] += jnp.concatenate(outs, axis=1)


def _sb_prompt_kernel(qi_ref, kj_ref, q_ref, k_ref, v_ref, o_ref, acc_ref, carry_ref):
    p = pl.program_id(1)
    qi = qi_ref[p]
    kj = kj_ref[p]

    @pl.when(kj == qi)
    def _():
        acc_ref[...] = jnp.zeros_like(acc_ref)
        carry_ref[...] = jnp.zeros_like(carry_ref)

    n = SB_BLOCK
    row = lax.broadcasted_iota(jnp.int32, (n, n), 0) + qi * n
    col = lax.broadcasted_iota(jnp.int32, (n, n), 1) + kj * n
    _sb_block(q_ref[...], k_ref[...], v_ref[...], _strict_lower_ones(n), carry_ref, acc_ref, col < row)

    @pl.when(kj == 0)
    def _():
        o_ref[...] = acc_ref[...].astype(BF16)


def _attn_b_prompt(q_bf, k_bf, v_bf, seq):
    n = q_bf.shape[0]
    blk = SB_BLOCK
    nq = seq // blk
    pairs = [(qi, kj) for qi in range(nq) for kj in range(qi, -1, -1)]
    qi_tab = jnp.asarray(np.array([p[0] for p in pairs], np.int32))
    kj_tab = jnp.asarray(np.array([p[1] for p in pairs], np.int32))
    grid_spec = pltpu.PrefetchScalarGridSpec(
        num_scalar_prefetch=2,
        grid=(n // seq, len(pairs)),
        in_specs=[pl.BlockSpec((blk, D_MODEL), lambda b, p, qi, kj: (b * nq + qi[p], 0)),
                  pl.BlockSpec((blk, D_MODEL), lambda b, p, qi, kj: (b * nq + kj[p], 0)),
                  pl.BlockSpec((blk, D_MODEL), lambda b, p, qi, kj: (b * nq + kj[p], 0))],
        out_specs=pl.BlockSpec((blk, D_MODEL), lambda b, p, qi, kj: (b * nq + qi[p], 0)),
        scratch_shapes=[pltpu.VMEM((blk, D_MODEL), F32),
                        pltpu.VMEM((N_HEADS, blk, LANES), F32)])
    return pl.pallas_call(
        _sb_prompt_kernel,
        grid_spec=grid_spec,
        out_shape=jax.ShapeDtypeStruct((n, D_MODEL), BF16),
        compiler_params=_params(2),
        name="attn_b_prompt",
    )(qi_tab, kj_tab, q_bf, k_bf, v_bf)


def _sb_sample_kernel(q_ref, kn_ref, vn_ref, ck_ref, cv_ref, o_ref, acc_ref, carry_ref):
    j = pl.program_id(1)
    t = q_ref.shape[0]

    @pl.when(j == 0)
    def _():
        acc_ref[...] = jnp.zeros_like(acc_ref)
        carry_ref[...] = jnp.zeros_like(carry_ref)
        row = lax.broadcasted_iota(jnp.int32, (t, t), 0)
        col = lax.broadcasted_iota(jnp.int32, (t, t), 1)
        _sb_block(q_ref[...], kn_ref[...], vn_ref[...], _strict_lower_ones(t), carry_ref, acc_ref, col < row)

    @pl.when(j > 0)
    def _():
        _sb_block(q_ref[...], ck_ref[0].astype(BF16), cv_ref[0].astype(BF16),
                  _strict_lower_ones(SB_BLOCK), carry_ref, acc_ref, None)

    @pl.when(j == pl.num_programs(1) - 1)
    def _():
        o_ref[...] = acc_ref[...].astype(BF16)


def _attn_b_sample(q_bf, k_bf, v_bf, cache_k, cache_v):
    streams, past, _ = cache_k.shape
    t = q_bf.shape[0] // streams
    blk = SB_BLOCK
    n_cache = past // blk
    new = lambda b, j: (b, 0)
    old = lambda b, j: (b, jnp.minimum(n_cache - j, n_cache - 1), 0)
    return pl.pallas_call(
        _sb_sample_kernel,
        grid=(streams, n_cache + 1),
        in_specs=[pl.BlockSpec((t, D_MODEL), new),
                  pl.BlockSpec((t, D_MODEL), new),
                  pl.BlockSpec((t, D_MODEL), new),
                  pl.BlockSpec((1, blk, D_MODEL), old),
                  pl.BlockSpec((1, blk, D_MODEL), old)],
        out_specs=pl.BlockSpec((t, D_MODEL), new),
        out_shape=jax.ShapeDtypeStruct(q_bf.shape, BF16),
        scratch_shapes=[pltpu.VMEM((t, D_MODEL), F32),
                        pltpu.VMEM((N_HEADS, t, LANES), F32)],
        compiler_params=_params(2),
        name="attn_b_sample",
    )(q_bf, k_bf, v_bf, cache_k, cache_v)


ROUTE_COLS = N_GROUPS + N_EXPERTS


def _oproj_route_kernel(o_ref, wo_ref, h_ref, g_ref, wr_hi_ref, wr_lo_ref, h_out_ref, xn_ref, route_ref):
    h = h_ref[...] + jnp.dot(o_ref[...], wo_ref[...], preferred_element_type=F32)
    h_out_ref[...] = h
    xn = _rmsnorm(h, g_ref[...])
    xn_ref[...] = xn
    x_hi = xn.astype(BF16)
    x_lo = (xn - x_hi.astype(F32)).astype(BF16)
    logits = (jnp.dot(x_hi, wr_hi_ref[...], preferred_element_type=F32)
              + jnp.dot(x_hi, wr_lo_ref[...], preferred_element_type=F32)
              + jnp.dot(x_lo, wr_hi_ref[...], preferred_element_type=F32))
    lane = lax.broadcasted_iota(jnp.int32, logits.shape, 1)
    big = jnp.int32(2 * LANES)

    def first_argmax(vals, peak):
        return jnp.min(jnp.where(vals == peak, lane, big), axis=-1, keepdims=True)

    g_mask = lane < N_GROUPS
    g_logits = jnp.where(g_mask, logits, -jnp.inf)
    g_max = jnp.max(g_logits, axis=-1, keepdims=True)
    g_idx = first_argmax(g_logits, g_max)
    g_sum = jnp.sum(jnp.where(g_mask, jnp.exp(logits - g_max), 0.0), axis=-1, keepdims=True)
    g_p = 1.0 / g_sum

    e_lo = N_GROUPS + g_idx * EXPERTS_PER_GROUP
    e_mask = (lane >= e_lo) & (lane < e_lo + EXPERTS_PER_GROUP)
    e_logits = jnp.where(e_mask, logits, -jnp.inf)
    m1 = jnp.max(e_logits, axis=-1, keepdims=True)
    i1 = first_argmax(e_logits, m1)
    rest = jnp.where(lane == i1, -jnp.inf, e_logits)
    m2 = jnp.max(rest, axis=-1, keepdims=True)
    i2 = first_argmax(rest, m2)
    e_sum = jnp.sum(jnp.where(e_mask, jnp.exp(logits - m1), 0.0), axis=-1, keepdims=True)
    p1 = 1.0 / e_sum
    p2 = jnp.exp(m2 - m1) / e_sum
    top_sum = p1 + p2
    w1 = g_p * p1 / top_sum
    w2 = g_p * p2 / top_sum
    route = jnp.where(lane == 0, (i1 - N_GROUPS).astype(F32),
                      jnp.where(lane == 1, (i2 - N_GROUPS).astype(F32),
                                jnp.where(lane == 2, w1, jnp.where(lane == 3, w2, 0.0))))
    route_ref[...] = route


def _oproj_route(o_bf, wo_bf, h, g, wr_hi, wr_lo):
    n = h.shape[0]
    tm = min(ROW_TILE, n)
    row = lambda i: (i, 0)
    fixed = lambda i: (0, 0)
    return pl.pallas_call(
        _oproj_route_kernel,
        grid=(n // tm,),
        in_specs=[pl.BlockSpec((tm, D_MODEL), row),
                  pl.BlockSpec((D_MODEL, D_MODEL), fixed),
                  pl.BlockSpec((tm, D_MODEL), row),
                  pl.BlockSpec((1, D_MODEL), fixed),
                  pl.BlockSpec((D_MODEL, LANES), fixed),
                  pl.BlockSpec((D_MODEL, LANES), fixed)],
        out_specs=[pl.BlockSpec((tm, D_MODEL), row),
                   pl.BlockSpec((tm, D_MODEL), row),
                   pl.BlockSpec((tm, LANES), row)],
        out_shape=[jax.ShapeDtypeStruct((n, D_MODEL), F32),
                   jax.ShapeDtypeStruct((n, D_MODEL), F32),
                   jax.ShapeDtypeStruct((n, LANES), F32)],
        compiler_params=_params(1),
        name="oproj_route",
    )(o_bf, wo_bf, h, g, wr_hi, wr_lo)


def _dispatch_kernel(dest_ref, xn_ref, buf_in_ref, buf_ref, sem):
    del buf_in_ref
    i = pl.program_id(0)
    tm = xn_ref.shape[0]
    base = i * (tm * TOP_K)

    def row_copy(t, slot):
        return pltpu.make_async_copy(xn_ref.at[pl.ds(t, 1)], buf_ref.at[pl.ds(slot, 1)], sem)

    def issue(t, carry):
        for k in range(TOP_K):
            row_copy(t, dest_ref[base + TOP_K * t + k]).start()
        return carry

    lax.fori_loop(0, tm, issue, 0)

    def drain(t, carry):
        for k in range(TOP_K):
            row_copy(t, dest_ref[base + TOP_K * t + k]).wait()
        return carry

    lax.fori_loop(0, tm, drain, 0)


def _dispatch(dest, xn, buf):
    n = xn.shape[0]
    tm = min(ROW_TILE, n)
    grid_spec = pltpu.PrefetchScalarGridSpec(
        num_scalar_prefetch=1,
        grid=(n // tm,),
        in_specs=[pl.BlockSpec((tm, D_MODEL), lambda i, d: (i, 0)),
                  pl.BlockSpec(memory_space=pl.ANY)],
        out_specs=pl.BlockSpec(memory_space=pl.ANY),
        scratch_shapes=[pltpu.SemaphoreType.DMA(())])
    return pl.pallas_call(
        _dispatch_kernel,
        grid_spec=grid_spec,
        out_shape=jax.ShapeDtypeStruct(buf.shape, buf.dtype),
        input_output_aliases={2: 0},
        compiler_params=_params(1, disable_bounds_checks=True, has_side_effects=True),
        name="moe_dispatch",
    )(dest, xn, buf)


def _expert_kernel(be_ref, nused_ref, x_ref, wg_ref, wu_ref, wd_ref, o_ref):
    used = pl.program_id(0) < nused_ref[0]

    @pl.when(used)
    def _():
        x = x_ref[...].astype(BF16)
        a = jnp.dot(x, wg_ref[0], preferred_element_type=F32)
        b = jnp.dot(x, wu_ref[0], preferred_element_type=F32)
        mid = (a * (1.0 / (1.0 + jnp.exp(-a))) * b).astype(BF16)
        o_ref[...] = jnp.dot(mid, wd_ref[0], preferred_element_type=F32)

    @pl.when(jnp.logical_not(used))
    def _():
        o_ref[...] = jnp.zeros_like(o_ref)


def _experts(block_e, n_used, buf, wg_bf, wu_bf, wd_bf):
    bm = EXPERT_BLOCK
    n_blocks = buf.shape[0] // bm
    blk = lambda i, be, nu: (jnp.minimum(i, nu[0] - 1), 0)
    wsel = lambda i, be, nu: (be[jnp.minimum(i, nu[0] - 1)], 0, 0)
    grid_spec = pltpu.PrefetchScalarGridSpec(
        num_scalar_prefetch=2,
        grid=(n_blocks,),
        in_specs=[pl.BlockSpec((bm, D_MODEL), blk),
                  pl.BlockSpec((1, D_MODEL, D_EXPERT), wsel),
                  pl.BlockSpec((1, D_MODEL, D_EXPERT), wsel),
                  pl.BlockSpec((1, D_EXPERT, D_MODEL), wsel)],
        out_specs=pl.BlockSpec((bm, D_MODEL), lambda i, be, nu: (i, 0)))
    return pl.pallas_call(
        _expert_kernel,
        grid_spec=grid_spec,
        out_shape=jax.ShapeDtypeStruct(buf.shape, F32),
        compiler_params=_params(1),
        name="moe_experts",
    )(block_e, n_used, buf, wg_bf, wu_bf, wd_bf)


def _combine_kernel(dest_ref, h_ref, route_ref, g_ref, yb_ref, o_ref, rows_ref, sem, *, final_norm):
    i = pl.program_id(0)
    n_steps = pl.num_programs(0)
    tm = h_ref.shape[0]

    def row_copy(step, t, k):
        slot = step % 2
        src = dest_ref[step * (tm * TOP_K) + TOP_K * t + k]
        return pltpu.make_async_copy(yb_ref.at[pl.ds(src, 1)], rows_ref.at[slot, k, pl.ds(t, 1)], sem.at[slot])

    def issue(step):
        def body(t, carry):
            for k in range(TOP_K):
                row_copy(step, t, k).start()
            return carry
        lax.fori_loop(0, tm, body, 0)

    @pl.when(i == 0)
    def _():
        issue(i)

    @pl.when(i + 1 < n_steps)
    def _():
        issue(i + 1)

    def drain(t, carry):
        for k in range(TOP_K):
            row_copy(i, t, k).wait()
        return carry

    lax.fori_loop(0, tm, drain, 0)

    slot = i % 2
    route = route_ref[...]
    w0 = route[:, 2:3]
    w1 = route[:, 3:4]
    out = h_ref[...] + (rows_ref[slot, 0] * w0 + rows_ref[slot, 1] * w1)
    if final_norm:
        out = _rmsnorm(out, g_ref[...])
    o_ref[...] = out


def _combine(dest, h, route, g, yb, final_norm):
    n = h.shape[0]
    tm = min(COMBINE_TILE, n)
    grid_spec = pltpu.PrefetchScalarGridSpec(
        num_scalar_prefetch=1,
        grid=(n // tm,),
        in_specs=[pl.BlockSpec((tm, D_MODEL), lambda i, d: (i, 0)),
                  pl.BlockSpec((tm, LANES), lambda i, d: (i, 0)),
                  pl.BlockSpec((1, D_MODEL), lambda i, d: (0, 0)),
                  pl.BlockSpec(memory_space=pl.ANY)],
        out_specs=pl.BlockSpec((tm, D_MODEL), lambda i, d: (i, 0)),
        scratch_shapes=[pltpu.VMEM((2, TOP_K, tm, D_MODEL), F32),
                        pltpu.SemaphoreType.DMA((2,))])
    return pl.pallas_call(
        functools.partial(_combine_kernel, final_norm=final_norm),
        grid_spec=grid_spec,
        out_shape=jax.ShapeDtypeStruct(h.shape, F32),
        compiler_params=_params(1, disable_bounds_checks=True),
        name="moe_combine",
    )(dest, h, route, g, yb)


def _route_plan(route_p, route_s):
    route = jnp.concatenate([route_p[:, :TOP_K], route_s[:, :TOP_K]], axis=0)
    flat_e = route.astype(jnp.int32).reshape(-1)
    n_assign = flat_e.shape[0]
    onehot = (flat_e[:, None] == jnp.arange(N_EXPERTS, dtype=jnp.int32)[None, :]).astype(jnp.int32)
    running = jnp.cumsum(onehot, axis=0)
    rank = jnp.sum((running - onehot) * onehot, axis=1)
    counts = running[-1]
    bm = EXPERT_BLOCK
    padded = (counts + bm - 1) // bm * bm
    pends = jnp.cumsum(padded)
    dest = (pends - padded)[flat_e] + rank
    n_blocks = -(-n_assign // bm) + N_EXPERTS
    block_e = jnp.minimum(jnp.searchsorted(pends, jnp.arange(n_blocks, dtype=jnp.int32) * bm, side='right'),
                          N_EXPERTS - 1).astype(jnp.int32)
    n_used = (pends[-1] // bm).astype(jnp.int32).reshape(1)
    return dest.astype(jnp.int32), block_e, n_used, n_blocks


def _moe(hp, hs, route_p, route_s, xn_p, xn_s, g_final, wg_bf, wu_bf, wd_bf, final_norm):
    dest, block_e, n_used, n_blocks = _route_plan(route_p, route_s)
    n_p = hp.shape[0] * TOP_K
    dest_p, dest_s = dest[:n_p], dest[n_p:]
    buf = jnp.zeros((n_blocks * EXPERT_BLOCK, D_MODEL), F32)
    buf = _dispatch(dest_p, xn_p, buf)
    buf = _dispatch(dest_s, xn_s, buf)
    yb = _experts(block_e, n_used, buf, wg_bf, wu_bf, wd_bf)
    out_p = _combine(dest_p, hp, route_p, g_final, yb, final_norm)
    out_s = _combine(dest_s, hs, route_s, g_final, yb, final_norm)
    return out_p, out_s


def _router_weights(w_group, w_router):
    w_exp = jnp.transpose(w_router, (1, 0, 2)).reshape(D_MODEL, N_EXPERTS)
    w = jnp.concatenate([w_group, w_exp, jnp.zeros((D_MODEL, LANES - ROUTE_COLS), F32)], axis=1)
    hi = w.astype(BF16)
    lo = (w - hi.astype(F32)).astype(BF16)
    return hi, lo


def kernel(x_prompt, x_sample, cache_a_k, cache_a_v, cache_b_k, cache_b_v, norm_mix, norm_ffn, norm_final,
           a_w_qkv, a_w_o, a_sinks, b_w_qkv, b_w_o, moe_w_group, moe_w_router, moe_w_gate, moe_w_up, moe_w_down):
    batch, seq, _ = x_prompt.shape
    streams, t_new, _ = x_sample.shape
    past = cache_b_k.shape[2]
    hp = x_prompt.reshape(batch * seq, D_MODEL)
    hs = x_sample.reshape(streams * t_new, D_MODEL)
    g_final = norm_final.reshape(1, D_MODEL)

    pos_p = jnp.tile(jnp.arange(seq, dtype=jnp.int32), batch)
    pos_s = jnp.tile(past + jnp.arange(t_new, dtype=jnp.int32), streams)
    cs_p = _rope_table(pos_p)
    cs_s = _rope_table(pos_s)

    def moe_layer(i, hp, hs, op, os_, w_o, final_norm):
        wo_bf = w_o.astype(BF16)
        g = norm_ffn[i].reshape(1, D_MODEL)
        wr_hi, wr_lo = _router_weights(moe_w_group[i], moe_w_router[i])
        hp, xn_p, route_p = _oproj_route(op, wo_bf, hp, g, wr_hi, wr_lo)
        hs, xn_s, route_s = _oproj_route(os_, wo_bf, hs, g, wr_hi, wr_lo)
        return _moe(hp, hs, route_p, route_s, xn_p, xn_s, g_final,
                    moe_w_gate[i].astype(BF16), moe_w_up[i].astype(BF16), moe_w_down[i].astype(BF16), final_norm)

    g0 = norm_mix[0].reshape(1, D_MODEL)
    wa_bf = a_w_qkv[0].astype(BF16)
    qkv_p = _proj_a(hp, g0, wa_bf, cs_p)
    qkv_s = _proj_a(hs, g0, wa_bf, cs_s)
    ck = cache_a_k[0].reshape(streams, WINDOW, A_KV_DIM)
    cv = cache_a_v[0].reshape(streams, WINDOW, A_KV_DIM)
    op = _attn_a_prompt(qkv_p, a_sinks[0], seq)
    os_ = _attn_a_sample(qkv_s, ck, cv, a_sinks[0])
    k_p = qkv_p[:, D_MODEL:D_MODEL + A_KV_DIM].reshape(batch, seq, A_KV_HEADS, HEAD_DIM)
    v_p = qkv_p[:, D_MODEL + A_KV_DIM:].reshape(batch, seq, A_KV_HEADS, HEAD_DIM)
    k_s = qkv_s[:, D_MODEL:D_MODEL + A_KV_DIM].reshape(streams, t_new, A_KV_HEADS, HEAD_DIM)
    v_s = qkv_s[:, D_MODEL + A_KV_DIM:].reshape(streams, t_new, A_KV_HEADS, HEAD_DIM)
    new_a_k_prompt = k_p[None, :, -WINDOW:]
    new_a_v_prompt = v_p[None, :, -WINDOW:]
    new_a_k_sample = jnp.concatenate([cache_a_k[0], k_s], axis=1)[None, :, -WINDOW:]
    new_a_v_sample = jnp.concatenate([cache_a_v[0], v_s], axis=1)[None, :, -WINDOW:]
    hp, hs = moe_layer(0, hp, hs, op, os_, a_w_o[0], False)

    g1 = norm_mix[1].reshape(1, D_MODEL)
    wb_bf = b_w_qkv[0].astype(BF16)
    q_p, kf_p, vf_p, kb_p, vb_p = _proj_b(hp, g1, wb_bf)
    q_s, kf_s, vf_s, kb_s, vb_s = _proj_b(hs, g1, wb_bf)
    op = _attn_b_prompt(q_p, kb_p, vb_p, seq)
    os_ = _attn_b_sample(q_s, kb_s, vb_s, cache_b_k[0].reshape(streams, past, D_MODEL),
                         cache_b_v[0].reshape(streams, past, D_MODEL))
    new_b_k_prompt = kf_p.reshape(1, batch, seq, N_HEADS, HEAD_DIM)
    new_b_v_prompt = vf_p.reshape(1, batch, seq, N_HEADS, HEAD_DIM)
    new_b_k_sample = kf_s.reshape(1, streams, t_new, N_HEADS, HEAD_DIM)
    new_b_v_sample = vf_s.reshape(1, streams, t_new, N_HEADS, HEAD_DIM)
    hp, hs = moe_layer(1, hp, hs, op, os_, b_w_o[0], True)

    y_prompt = hp.reshape(batch, seq, D_MODEL)
    y_sample = hs.reshape(streams, t_new, D_MODEL)
    return (y_prompt, y_sample, new_a_k_prompt, new_a_v_prompt, new_a_k_sample, new_a_v_sample,
            new_b_k_prompt, new_b_v_prompt, new_b_k_sample, new_b_v_sample)
```

```python
import functools

import jax
import jax.numpy as jnp
import numpy as np
from jax import lax
from jax.experimental import pallas as pl
from jax.experimental.pallas import tpu as pltpu

F32 = jnp.float32
BF16 = jnp.bfloat16

D_MODEL = 1024
HEAD_DIM = 64
N_HEADS = D_MODEL // HEAD_DIM
A_KV_HEADS = 4
A_GROUP = N_HEADS // A_KV_HEADS
A_KV_DIM = A_KV_HEADS * HEAD_DIM
CHUNK = 64
WINDOW = 128
ROT_DIM = HEAD_DIM // 4
ROPE_THETA = 500000.0
N_GROUPS = 4
EXPERTS_PER_GROUP = 8
N_EXPERTS = N_GROUPS * EXPERTS_PER_GROUP
TOP_K = 2
D_EXPERT = D_MODEL // 2
RMS_EPS = 1e-6
NEG_INF = -1e30
SCALE = HEAD_DIM ** -0.5

LANES = 128
ROW_TILE = 512
ATTN_A_BLOCK = WINDOW
SB_BLOCK = 256
EXPERT_BLOCK = 256
COMBINE_TILE = 256
VMEM_LIMIT = 48 * 1024 * 1024
SB_DECAY_LIMIT = 105.0


def _params(n_axes, **kw):
    return pltpu.CompilerParams(dimension_semantics=("arbitrary",) * n_axes,
                                vmem_limit_bytes=VMEM_LIMIT, **kw)


def _rmsnorm(x, g):
    return x * lax.rsqrt(jnp.mean(x * x, axis=-1, keepdims=True) + RMS_EPS) * g


def _proj_a_kernel(x_ref, g_ref, w_ref, cs_ref, o_ref):
    xn = _rmsnorm(x_ref[...], g_ref[...]).astype(BF16)
    cos = cs_ref[:, :LANES]
    sin = cs_ref[:, LANES:]
    lane = lax.broadcasted_iota(jnp.int32, cos.shape, 1) % HEAD_DIM
    first_half = lane < ROT_DIM // 2
    n_rot = (D_MODEL + A_KV_DIM) // LANES
    n_all = o_ref.shape[1] // LANES
    for j in range(n_all):
        cols = slice(j * LANES, (j + 1) * LANES)
        blk = jnp.dot(xn, w_ref[:, cols], preferred_element_type=F32)
        if j < n_rot:
            partner = jnp.where(first_half, pltpu.roll(blk, LANES - ROT_DIM // 2, 1),
                                pltpu.roll(blk, ROT_DIM // 2, 1))
            blk = blk * cos + partner * sin
        o_ref[:, cols] = blk


def _rope_table(pos):
    half = ROT_DIM // 2
    inv = ROPE_THETA ** (-jnp.arange(0, ROT_DIM, 2, dtype=F32) / ROT_DIM)
    ang = pos.astype(F32)[:, None] * inv[None, :]
    cos = jnp.cos(ang)
    sin = jnp.sin(ang)
    n = pos.shape[0]
    ones = jnp.ones((n, HEAD_DIM - ROT_DIM), F32)
    zeros = jnp.zeros((n, HEAD_DIM - ROT_DIM), F32)
    cos_head = jnp.concatenate([cos, cos, ones], axis=1)
    sin_head = jnp.concatenate([-sin, sin, zeros], axis=1)
    assert cos_head.shape[1] == HEAD_DIM and half * 2 == ROT_DIM
    return jnp.concatenate([cos_head, cos_head, sin_head, sin_head], axis=1)


def _proj_a(x, g, w_bf, cs):
    n = x.shape[0]
    tm = min(ROW_TILE, n)
    n_out = w_bf.shape[1]
    return pl.pallas_call(
        _proj_a_kernel,
        grid=(n // tm,),
        in_specs=[pl.BlockSpec((tm, D_MODEL), lambda i: (i, 0)),
                  pl.BlockSpec((1, D_MODEL), lambda i: (0, 0)),
                  pl.BlockSpec((D_MODEL, n_out), lambda i: (0, 0)),
                  pl.BlockSpec((tm, 2 * LANES), lambda i: (i, 0))],
        out_specs=pl.BlockSpec((tm, n_out), lambda i: (i, 0)),
        out_shape=jax.ShapeDtypeStruct((n, n_out), F32),
        compiler_params=_params(1),
        name="proj_a",
    )(x, g, w_bf, cs)


def _sink_attention(q, k_bf, v_bf, sink_ref, valid):
    outs = []
    for h in range(N_HEADS):
        g = h // A_GROUP
        qh = q[:, h * HEAD_DIM:(h + 1) * HEAD_DIM].astype(BF16)
        kh = k_bf[:, g * HEAD_DIM:(g + 1) * HEAD_DIM]
        vh = v_bf[:, g * HEAD_DIM:(g + 1) * HEAD_DIM]
        s = lax.dot_general(qh, kh, (((1,), (1,)), ((), ())), preferred_element_type=F32) * SCALE
        if valid is not None:
            s = jnp.where(valid, s, NEG_INF)
        sink = sink_ref[h]
        m = jnp.maximum(jnp.max(s, axis=-1, keepdims=True), sink)
        e = jnp.exp(s - m)
        den = jnp.sum(e, axis=-1, keepdims=True) + jnp.exp(sink - m)
        o = jnp.dot(e.astype(BF16), vh, preferred_element_type=F32) / den
        outs.append(o)
    return jnp.concatenate(outs, axis=1).astype(BF16)


def _attn_a_prompt_kernel(sink_ref, q_ref, kp_ref, kc_ref, vp_ref, vc_ref, o_ref, *, blocks_per_seq):
    i = pl.program_id(0)
    has_prev = (i % blocks_per_seq) != 0
    k = jnp.concatenate([kp_ref[...], kc_ref[...]], axis=0).astype(BF16)
    v = jnp.concatenate([vp_ref[...], vc_ref[...]], axis=0).astype(BF16)
    rows, keys = ATTN_A_BLOCK, 2 * ATTN_A_BLOCK
    q_chunk = lax.broadcasted_iota(jnp.int32, (rows, keys), 0) // CHUNK
    col = lax.broadcasted_iota(jnp.int32, (rows, keys), 1)
    k_chunk = col // CHUNK
    valid = (k_chunk >= q_chunk) & (k_chunk <= q_chunk + WINDOW // CHUNK)
    valid = valid & ((col >= ATTN_A_BLOCK) | has_prev)
    o_ref[...] = _sink_attention(q_ref[...], k, v, sink_ref, valid)


def _attn_a_prompt(qkv, sinks, seq):
    n = qkv.shape[0]
    blk = ATTN_A_BLOCK
    blocks_per_seq = seq // blk
    kcol = D_MODEL // A_KV_DIM
    prev = lambda i: jnp.maximum(i - 1, 0)
    return pl.pallas_call(
        functools.partial(_attn_a_prompt_kernel, blocks_per_seq=blocks_per_seq),
        grid=(n // blk,),
        in_specs=[pl.BlockSpec(memory_space=pltpu.SMEM),
                  pl.BlockSpec((blk, D_MODEL), lambda i: (i, 0)),
                  pl.BlockSpec((blk, A_KV_DIM), lambda i: (prev(i), kcol)),
                  pl.BlockSpec((blk, A_KV_DIM), lambda i: (i, kcol)),
                  pl.BlockSpec((blk, A_KV_DIM), lambda i: (prev(i), kcol + 1)),
                  pl.BlockSpec((blk, A_KV_DIM), lambda i: (i, kcol + 1))],
        out_specs=pl.BlockSpec((blk, D_MODEL), lambda i: (i, 0)),
        out_shape=jax.ShapeDtypeStruct((n, D_MODEL), BF16),
        compiler_params=_params(1),
        name="attn_a_prompt",
    )(sinks, qkv, qkv, qkv, qkv, qkv)


def _attn_a_sample_kernel(sink_ref, q_ref, kn_ref, vn_ref, ck_ref, cv_ref, o_ref):
    k = jnp.concatenate([ck_ref[0], kn_ref[...]], axis=0).astype(BF16)
    v = jnp.concatenate([cv_ref[0], vn_ref[...]], axis=0).astype(BF16)
    o_ref[...] = _sink_attention(q_ref[...], k, v, sink_ref, None)


def _attn_a_sample(qkv, cache_k, cache_v, sinks):
    streams = cache_k.shape[0]
    t = qkv.shape[0] // streams
    kcol = D_MODEL // A_KV_DIM
    return pl.pallas_call(
        _attn_a_sample_kernel,
        grid=(streams,),
        in_specs=[pl.BlockSpec(memory_space=pltpu.SMEM),
                  pl.BlockSpec((t, D_MODEL), lambda b: (b, 0)),
                  pl.BlockSpec((t, A_KV_DIM), lambda b: (b, kcol)),
                  pl.BlockSpec((t, A_KV_DIM), lambda b: (b, kcol + 1)),
                  pl.BlockSpec((1, WINDOW, A_KV_DIM), lambda b: (b, 0, 0)),
                  pl.BlockSpec((1, WINDOW, A_KV_DIM), lambda b: (b, 0, 0))],
        out_specs=pl.BlockSpec((t, D_MODEL), lambda b: (b, 0)),
        out_shape=jax.ShapeDtypeStruct((qkv.shape[0], D_MODEL), BF16),
        compiler_params=_params(1),
        name="attn_a_sample",
    )(sinks, qkv, qkv, qkv, cache_k, cache_v)


def _proj_b_kernel(x_ref, g_ref, w_ref, q_ref, k_ref, v_ref, kb_ref, vb_ref):
    xn = _rmsnorm(x_ref[...], g_ref[...]).astype(BF16)
    for j in range(D_MODEL // LANES):
        cols = slice(j * LANES, (j + 1) * LANES)
        q = jnp.dot(xn, w_ref[:, cols], preferred_element_type=F32)
        q_ref[:, cols] = (q * SCALE).astype(BF16)
        k = jnp.dot(xn, w_ref[:, D_MODEL + j * LANES:D_MODEL + (j + 1) * LANES], preferred_element_type=F32)
        k_ref[:, cols] = k
        kb_ref[:, cols] = k.astype(BF16)
        v = jnp.dot(xn, w_ref[:, 2 * D_MODEL + j * LANES:2 * D_MODEL + (j + 1) * LANES],
                    preferred_element_type=F32)
        v_ref[:, cols] = v
        vb_ref[:, cols] = v.astype(BF16)


def _proj_b(x, g, w_bf):
    n = x.shape[0]
    tm = min(ROW_TILE, n)
    row = lambda i: (i, 0)
    return pl.pallas_call(
        _proj_b_kernel,
        grid=(n // tm,),
        in_specs=[pl.BlockSpec((tm, D_MODEL), row),
                  pl.BlockSpec((1, D_MODEL), lambda i: (0, 0)),
                  pl.BlockSpec((D_MODEL, 3 * D_MODEL), lambda i: (0, 0))],
        out_specs=[pl.BlockSpec((tm, D_MODEL), row)] * 5,
        out_shape=[jax.ShapeDtypeStruct((n, D_MODEL), BF16),
                   jax.ShapeDtypeStruct((n, D_MODEL), F32),
                   jax.ShapeDtypeStruct((n, D_MODEL), F32),
                   jax.ShapeDtypeStruct((n, D_MODEL), BF16),
                   jax.ShapeDtypeStruct((n, D_MODEL), BF16)],
        compiler_params=_params(1),
        name="proj_b",
    )(x, g, w_bf)


def _strict_lower_ones(n):
    j = lax.broadcasted_iota(jnp.int32, (n, n), 0)
    s = lax.broadcasted_iota(jnp.int32, (n, n), 1)
    return jnp.where(j > s, 1.0, 0.0).astype(BF16)


def _sb_head(qh, kh, vh, tri, before):
    n_rows = qh.shape[0]
    z = lax.dot_general(qh, kh, (((1,), (1,)), ((), ())), preferred_element_type=F32)
    softplus = jnp.maximum(z, 0.0) + jnp.log(1.0 + jnp.exp(-jnp.abs(z)))
    log_beta = z - softplus
    if before is not None:
        softplus = jnp.where(before, softplus, 0.0)
    hi = softplus.astype(BF16)
    lo = (softplus - hi.astype(F32)).astype(BF16)
    both = jnp.dot(jnp.concatenate([hi, lo], axis=0), tri, preferred_element_type=F32)
    later = both[:n_rows] + both[n_rows:]
    a = jnp.exp(log_beta - later)
    if before is not None:
        a = jnp.where(before, a, 0.0)
    out = jnp.dot(a.astype(BF16), vh, preferred_element_type=F32)
    return out, later[:, 0:1] + softplus[:, 0:1]


def _sb_block(q_bf, k_bf, v_bf, tri, carry_ref, acc_ref, live_ref, before):
    first_head = lax.broadcasted_iota(jnp.int32, carry_ref.shape[1:], 1) < HEAD_DIM
    for pair in range(N_HEADS // 2):
        outs, masses = [], []
        for h in (2 * pair, 2 * pair + 1):
            cols = slice(h * HEAD_DIM, (h + 1) * HEAD_DIM)
            o, m = _sb_head(q_bf[:, cols], k_bf[:, cols], v_bf[:, cols], tri, before)
            outs.append(o)
            masses.append(m)
        carry = carry_ref[pair]
        cols = slice(pair * LANES, (pair + 1) * LANES)
        acc_ref[:, cols] += jnp.exp(-carry) * jnp.concatenate(outs, axis=1)
        carry_ref[pair] = carry + jnp.where(first_head, masses[0], masses[1])
    live_ref[0] = (jnp.min(carry_ref[...]) <= SB_DECAY_LIMIT).astype(jnp.int32)


def _sb_prompt_kernel(qi_ref, kj_ref, q_ref, k_ref, v_ref, o_ref, acc_ref, carry_ref, live_ref):
    p = pl.program_id(1)
    qi = qi_ref[p]
    kj = kj_ref[p]
    n = SB_BLOCK
    diagonal = kj == qi

    @pl.when(diagonal)
    def _():
        acc_ref[...] = jnp.zeros_like(acc_ref)
        carry_ref[...] = jnp.zeros_like(carry_ref)
        row = lax.broadcasted_iota(jnp.int32, (n, n), 0)
        col = lax.broadcasted_iota(jnp.int32, (n, n), 1)
        _sb_block(q_ref[...], k_ref[...], v_ref[...], _strict_lower_ones(n), carry_ref, acc_ref, live_ref,
                  col < row)

    @pl.when(jnp.logical_not(diagonal))
    def _():
        @pl.when(live_ref[0] == 1)
        def _():
            _sb_block(q_ref[...], k_ref[...], v_ref[...], _strict_lower_ones(n), carry_ref, acc_ref, live_ref,
                      None)

    @pl.when(kj == 0)
    def _():
        o_ref[...] = acc_ref[...].astype(BF16)


def _attn_b_prompt(q_bf, k_bf, v_bf, seq):
    n = q_bf.shape[0]
    blk = SB_BLOCK
    nq = seq // blk
    pairs = [(qi, kj) for qi in range(nq) for kj in range(qi, -1, -1)]
    qi_tab = jnp.asarray(np.array([p[0] for p in pairs], np.int32))
    kj_tab = jnp.asarray(np.array([p[1] for p in pairs], np.int32))
    grid_spec = pltpu.PrefetchScalarGridSpec(
        num_scalar_prefetch=2,
        grid=(n // seq, len(pairs)),
        in_specs=[pl.BlockSpec((blk, D_MODEL), lambda b, p, qi, kj: (b * nq + qi[p], 0)),
                  pl.BlockSpec((blk, D_MODEL), lambda b, p, qi, kj: (b * nq + kj[p], 0)),
                  pl.BlockSpec((blk, D_MODEL), lambda b, p, qi, kj: (b * nq + kj[p], 0))],
        out_specs=pl.BlockSpec((blk, D_MODEL), lambda b, p, qi, kj: (b * nq + qi[p], 0)),
        scratch_shapes=[pltpu.VMEM((blk, D_MODEL), F32),
                        pltpu.VMEM((N_HEADS // 2, blk, LANES), F32),
                        pltpu.SMEM((1,), jnp.int32)])
    return pl.pallas_call(
        _sb_prompt_kernel,
        grid_spec=grid_spec,
        out_shape=jax.ShapeDtypeStruct((n, D_MODEL), BF16),
        compiler_params=_params(2),
        name="attn_b_prompt",
    )(qi_tab, kj_tab, q_bf, k_bf, v_bf)


def _sb_sample_kernel(q_ref, kn_ref, vn_ref, ck_ref, cv_ref, o_ref, acc_ref, carry_ref, live_ref):
    j = pl.program_id(1)
    t = q_ref.shape[0]

    @pl.when(j == 0)
    def _():
        acc_ref[...] = jnp.zeros_like(acc_ref)
        carry_ref[...] = jnp.zeros_like(carry_ref)
        row = lax.broadcasted_iota(jnp.int32, (t, t), 0)
        col = lax.broadcasted_iota(jnp.int32, (t, t), 1)
        _sb_block(q_ref[...], kn_ref[...], vn_ref[...], _strict_lower_ones(t), carry_ref, acc_ref, live_ref,
                  col < row)

    @pl.when(j > 0)
    def _():
        @pl.when(live_ref[0] == 1)
        def _():
            _sb_block(q_ref[...], ck_ref[0].astype(BF16), cv_ref[0].astype(BF16),
                      _strict_lower_ones(SB_BLOCK), carry_ref, acc_ref, live_ref, None)

    @pl.when(j == pl.num_programs(1) - 1)
    def _():
        o_ref[...] = acc_ref[...].astype(BF16)


def _attn_b_sample(q_bf, k_bf, v_bf, cache_k, cache_v):
    streams, past, _ = cache_k.shape
    t = q_bf.shape[0] // streams
    blk = SB_BLOCK
    n_cache = past // blk
    new = lambda b, j: (b, 0)
    old = lambda b, j: (b, jnp.minimum(n_cache - j, n_cache - 1), 0)
    return pl.pallas_call(
        _sb_sample_kernel,
        grid=(streams, n_cache + 1),
        in_specs=[pl.BlockSpec((t, D_MODEL), new),
                  pl.BlockSpec((t, D_MODEL), new),
                  pl.BlockSpec((t, D_MODEL), new),
                  pl.BlockSpec((1, blk, D_MODEL), old),
                  pl.BlockSpec((1, blk, D_MODEL), old)],
        out_specs=pl.BlockSpec((t, D_MODEL), new),
        out_shape=jax.ShapeDtypeStruct(q_bf.shape, BF16),
        scratch_shapes=[pltpu.VMEM((t, D_MODEL), F32),
                        pltpu.VMEM((N_HEADS // 2, t, LANES), F32),
                        pltpu.SMEM((1,), jnp.int32)],
        compiler_params=_params(2),
        name="attn_b_sample",
    )(q_bf, k_bf, v_bf, cache_k, cache_v)


ROUTE_COLS = N_GROUPS + N_EXPERTS


def _oproj_route_kernel(o_ref, wo_ref, h_ref, g_ref, wr_hi_ref, wr_lo_ref, base_ref,
                        h_out_ref, xn_ref, route_ref, count_ref):
    @pl.when(pl.program_id(0) == 0)
    def _():
        count_ref[...] = base_ref[...]

    h = h_ref[...] + jnp.dot(o_ref[...], wo_ref[...], preferred_element_type=F32)
    h_out_ref[...] = h
    xn = _rmsnorm(h, g_ref[...])
    xn_ref[...] = xn
    x_hi = xn.astype(BF16)
    x_lo = (xn - x_hi.astype(F32)).astype(BF16)
    logits = (jnp.dot(x_hi, wr_hi_ref[...], preferred_element_type=F32)
              + jnp.dot(x_hi, wr_lo_ref[...], preferred_element_type=F32)
              + jnp.dot(x_lo, wr_hi_ref[...], preferred_element_type=F32))
    lane = lax.broadcasted_iota(jnp.int32, logits.shape, 1)
    big = jnp.int32(2 * LANES)

    def first_argmax(vals, peak):
        return jnp.min(jnp.where(vals == peak, lane, big), axis=-1, keepdims=True)

    g_mask = lane < N_GROUPS
    g_logits = jnp.where(g_mask, logits, -jnp.inf)
    g_max = jnp.max(g_logits, axis=-1, keepdims=True)
    g_idx = first_argmax(g_logits, g_max)
    g_sum = jnp.sum(jnp.where(g_mask, jnp.exp(logits - g_max), 0.0), axis=-1, keepdims=True)
    g_p = 1.0 / g_sum

    e_lo = N_GROUPS + g_idx * EXPERTS_PER_GROUP
    e_mask = (lane >= e_lo) & (lane < e_lo + EXPERTS_PER_GROUP)
    e_logits = jnp.where(e_mask, logits, -jnp.inf)
    m1 = jnp.max(e_logits, axis=-1, keepdims=True)
    i1 = first_argmax(e_logits, m1)
    rest = jnp.where(lane == i1, -jnp.inf, e_logits)
    m2 = jnp.max(rest, axis=-1, keepdims=True)
    i2 = first_argmax(rest, m2)
    e_sum = jnp.sum(jnp.where(e_mask, jnp.exp(logits - m1), 0.0), axis=-1, keepdims=True)
    p1 = 1.0 / e_sum
    p2 = jnp.exp(m2 - m1) / e_sum
    top_sum = p1 + p2
    w1 = g_p * p1 / top_sum
    w2 = g_p * p2 / top_sum
    e1 = i1 - N_GROUPS
    e2 = i2 - N_GROUPS
    pick1 = lane == e1
    pick2 = lane == e2
    picks = jnp.where(pick1 | pick2, 1.0, 0.0)
    tm = picks.shape[0]
    earlier = jnp.where(lax.broadcasted_iota(jnp.int32, (tm, tm), 1) < lax.broadcasted_iota(jnp.int32, (tm, tm), 0),
                        1.0, 0.0).astype(BF16)
    counts = count_ref[...]
    before = jnp.dot(earlier, picks.astype(BF16), preferred_element_type=F32) + counts
    rank1 = jnp.sum(jnp.where(pick1, before, 0.0), axis=-1, keepdims=True)
    rank2 = jnp.sum(jnp.where(pick2, before, 0.0), axis=-1, keepdims=True)
    count_ref[...] = counts + jnp.sum(picks, axis=0, keepdims=True)
    cols = (e1.astype(F32), e2.astype(F32), w1, w2, rank1, rank2)
    route = jnp.zeros_like(logits)
    for c, val in enumerate(cols):
        route = jnp.where(lane == c, val, route)
    route_ref[...] = route


def _oproj_route(o_bf, wo_bf, h, g, wr_hi, wr_lo, base_counts):
    n = h.shape[0]
    tm = min(ROW_TILE, n)
    row = lambda i: (i, 0)
    fixed = lambda i: (0, 0)
    return pl.pallas_call(
        _oproj_route_kernel,
        grid=(n // tm,),
        in_specs=[pl.BlockSpec((tm, D_MODEL), row),
                  pl.BlockSpec((D_MODEL, D_MODEL), fixed),
                  pl.BlockSpec((tm, D_MODEL), row),
                  pl.BlockSpec((1, D_MODEL), fixed),
                  pl.BlockSpec((D_MODEL, LANES), fixed),
                  pl.BlockSpec((D_MODEL, LANES), fixed),
                  pl.BlockSpec((1, LANES), fixed)],
        out_specs=[pl.BlockSpec((tm, D_MODEL), row),
                   pl.BlockSpec((tm, D_MODEL), row),
                   pl.BlockSpec((tm, LANES), row),
                   pl.BlockSpec((1, LANES), fixed)],
        out_shape=[jax.ShapeDtypeStruct((n, D_MODEL), F32),
                   jax.ShapeDtypeStruct((n, D_MODEL), F32),
                   jax.ShapeDtypeStruct((n, LANES), F32),
                   jax.ShapeDtypeStruct((1, LANES), F32)],
        compiler_params=_params(1),
        name="oproj_route",
    )(o_bf, wo_bf, h, g, wr_hi, wr_lo, base_counts)


def _dispatch_kernel(dest_ref, xn_ref, buf_in_ref, buf_ref, sem):
    del buf_in_ref
    i = pl.program_id(0)
    tm = xn_ref.shape[0]
    base = i * (tm * TOP_K)

    def row_copy(t, slot):
        return pltpu.make_async_copy(xn_ref.at[pl.ds(t, 1)], buf_ref.at[pl.ds(slot, 1)], sem)

    def issue(t, carry):
        for k in range(TOP_K):
            row_copy(t, dest_ref[base + TOP_K * t + k]).start()
        return carry

    lax.fori_loop(0, tm, issue, 0, unroll=8)
    for k in range(TOP_K):
        pltpu.make_async_copy(xn_ref, buf_ref.at[pl.ds(0, tm)], sem).wait()


def _dispatch(dest, xn, buf):
    n = xn.shape[0]
    tm = min(ROW_TILE, n)
    grid_spec = pltpu.PrefetchScalarGridSpec(
        num_scalar_prefetch=1,
        grid=(n // tm,),
        in_specs=[pl.BlockSpec((tm, D_MODEL), lambda i, d: (i, 0)),
                  pl.BlockSpec(memory_space=pl.ANY)],
        out_specs=pl.BlockSpec(memory_space=pl.ANY),
        scratch_shapes=[pltpu.SemaphoreType.DMA(())])
    return pl.pallas_call(
        _dispatch_kernel,
        grid_spec=grid_spec,
        out_shape=jax.ShapeDtypeStruct(buf.shape, buf.dtype),
        input_output_aliases={2: 0},
        compiler_params=_params(1, disable_bounds_checks=True, has_side_effects=True),
        name="moe_dispatch",
    )(dest, xn, buf)


def _expert_kernel(be_ref, nused_ref, x_ref, wg_ref, wu_ref, wd_ref, o_ref):
    used = pl.program_id(0) < nused_ref[0]

    @pl.when(used)
    def _():
        x = x_ref[...].astype(BF16)
        a = jnp.dot(x, wg_ref[0], preferred_element_type=F32)
        b = jnp.dot(x, wu_ref[0], preferred_element_type=F32)
        mid = (a * (1.0 / (1.0 + jnp.exp(-a))) * b).astype(BF16)
        o_ref[...] = jnp.dot(mid, wd_ref[0], preferred_element_type=F32)

    @pl.when(jnp.logical_not(used))
    def _():
        o_ref[...] = jnp.zeros_like(o_ref)


def _experts(block_e, n_used, buf, wg_bf, wu_bf, wd_bf):
    bm = EXPERT_BLOCK
    n_blocks = buf.shape[0] // bm
    blk = lambda i, be, nu: (jnp.minimum(i, nu[0] - 1), 0)
    wsel = lambda i, be, nu: (be[jnp.minimum(i, nu[0] - 1)], 0, 0)
    grid_spec = pltpu.PrefetchScalarGridSpec(
        num_scalar_prefetch=2,
        grid=(n_blocks,),
        in_specs=[pl.BlockSpec((bm, D_MODEL), blk),
                  pl.BlockSpec((1, D_MODEL, D_EXPERT), wsel),
                  pl.BlockSpec((1, D_MODEL, D_EXPERT), wsel),
                  pl.BlockSpec((1, D_EXPERT, D_MODEL), wsel)],
        out_specs=pl.BlockSpec((bm, D_MODEL), lambda i, be, nu: (i, 0)))
    return pl.pallas_call(
        _expert_kernel,
        grid_spec=grid_spec,
        out_shape=jax.ShapeDtypeStruct(buf.shape, F32),
        compiler_params=_params(1),
        name="moe_experts",
    )(block_e, n_used, buf, wg_bf, wu_bf, wd_bf)


def _combine_kernel(dest_ref, h_ref, route_ref, g_ref, yb_ref, o_ref, rows_ref, sem, *, final_norm):
    i = pl.program_id(0)
    n_steps = pl.num_programs(0)
    tm = h_ref.shape[0]

    def row_copy(step, t, k):
        slot = step % 2
        src = dest_ref[step * (tm * TOP_K) + TOP_K * t + k]
        return pltpu.make_async_copy(yb_ref.at[pl.ds(src, 1)], rows_ref.at[slot, k, pl.ds(t, 1)], sem.at[slot])

    def issue(step):
        def body(t, carry):
            for k in range(TOP_K):
                row_copy(step, t, k).start()
            return carry
        lax.fori_loop(0, tm, body, 0, unroll=8)

    @pl.when(i == 0)
    def _():
        issue(i)

    @pl.when(i + 1 < n_steps)
    def _():
        issue(i + 1)

    slot = i % 2
    for k in range(TOP_K):
        pltpu.make_async_copy(yb_ref.at[pl.ds(0, tm)], rows_ref.at[slot, k], sem.at[slot]).wait()

    route = route_ref[...]
    w0 = route[:, 2:3]
    w1 = route[:, 3:4]
    out = h_ref[...] + (rows_ref[slot, 0] * w0 + rows_ref[slot, 1] * w1)
    if final_norm:
        out = _rmsnorm(out, g_ref[...])
    o_ref[...] = out


def _combine(dest, h, route, g, yb, final_norm):
    n = h.shape[0]
    tm = min(COMBINE_TILE, n)
    grid_spec = pltpu.PrefetchScalarGridSpec(
        num_scalar_prefetch=1,
        grid=(n // tm,),
        in_specs=[pl.BlockSpec((tm, D_MODEL), lambda i, d: (i, 0)),
                  pl.BlockSpec((tm, LANES), lambda i, d: (i, 0)),
                  pl.BlockSpec((1, D_MODEL), lambda i, d: (0, 0)),
                  pl.BlockSpec(memory_space=pl.ANY)],
        out_specs=pl.BlockSpec((tm, D_MODEL), lambda i, d: (i, 0)),
        scratch_shapes=[pltpu.VMEM((2, TOP_K, tm, D_MODEL), F32),
                        pltpu.SemaphoreType.DMA((2,))])
    return pl.pallas_call(
        functools.partial(_combine_kernel, final_norm=final_norm),
        grid_spec=grid_spec,
        out_shape=jax.ShapeDtypeStruct(h.shape, F32),
        compiler_params=_params(1, disable_bounds_checks=True),
        name="moe_combine",
    )(dest, h, route, g, yb)


def _route_plan(route_p, route_s, counts):
    route = jnp.concatenate([route_p[:, :3 * TOP_K], route_s[:, :3 * TOP_K]], axis=0)
    flat_e = route[:, :TOP_K].astype(jnp.int32).reshape(-1)
    rank = route[:, 2 * TOP_K:].astype(jnp.int32).reshape(-1)
    n_assign = flat_e.shape[0]
    counts = counts[0, :N_EXPERTS].astype(jnp.int32)
    bm = EXPERT_BLOCK
    padded = (counts + bm - 1) // bm * bm
    pends = jnp.cumsum(padded)
    onehot = flat_e[:, None] == jnp.arange(N_EXPERTS, dtype=jnp.int32)[None, :]
    dest = jnp.sum(jnp.where(onehot, (pends - padded)[None, :], 0), axis=1) + rank
    n_blocks = -(-n_assign // bm) + N_EXPERTS
    first_rows = jnp.arange(n_blocks, dtype=jnp.int32) * bm
    block_e = jnp.minimum(jnp.sum((pends[None, :] <= first_rows[:, None]).astype(jnp.int32), axis=1), N_EXPERTS - 1)
    n_used = (pends[-1] // bm).astype(jnp.int32).reshape(1)
    return dest.astype(jnp.int32), block_e, n_used, n_blocks


def _moe(hp, hs, route_p, route_s, counts, xn_p, xn_s, g_final, wg_bf, wu_bf, wd_bf, final_norm):
    dest, block_e, n_used, n_blocks = _route_plan(route_p, route_s, counts)
    n_p = hp.shape[0] * TOP_K
    dest_p, dest_s = dest[:n_p], dest[n_p:]
    buf = jnp.zeros((n_blocks * EXPERT_BLOCK, D_MODEL), F32)
    buf = _dispatch(dest_p, xn_p, buf)
    buf = _dispatch(dest_s, xn_s, buf)
    yb = _experts(block_e, n_used, buf, wg_bf, wu_bf, wd_bf)
    out_p = _combine(dest_p, hp, route_p, g_final, yb, final_norm)
    out_s = _combine(dest_s, hs, route_s, g_final, yb, final_norm)
    return out_p, out_s


def _router_weights(w_group, w_router):
    w_exp = jnp.transpose(w_router, (1, 0, 2)).reshape(D_MODEL, N_EXPERTS)
    w = jnp.concatenate([w_group, w_exp, jnp.zeros((D_MODEL, LANES - ROUTE_COLS), F32)], axis=1)
    hi = w.astype(BF16)
    lo = (w - hi.astype(F32)).astype(BF16)
    return hi, lo


def kernel(x_prompt, x_sample, cache_a_k, cache_a_v, cache_b_k, cache_b_v, norm_mix, norm_ffn, norm_final,
           a_w_qkv, a_w_o, a_sinks, b_w_qkv, b_w_o, moe_w_group, moe_w_router, moe_w_gate, moe_w_up, moe_w_down):
    batch, seq, _ = x_prompt.shape
    streams, t_new, _ = x_sample.shape
    past = cache_b_k.shape[2]
    hp = x_prompt.reshape(batch * seq, D_MODEL)
    hs = x_sample.reshape(streams * t_new, D_MODEL)
    g_final = norm_final.reshape(1, D_MODEL)

    pos_p = jnp.tile(jnp.arange(seq, dtype=jnp.int32), batch)
    pos_s = jnp.tile(past + jnp.arange(t_new, dtype=jnp.int32), streams)
    cs_p = _rope_table(pos_p)
    cs_s = _rope_table(pos_s)

    def moe_layer(i, hp, hs, op, os_, w_o, final_norm):
        wo_bf = w_o.astype(BF16)
        g = norm_ffn[i].reshape(1, D_MODEL)
        wr_hi, wr_lo = _router_weights(moe_w_group[i], moe_w_router[i])
        hp, xn_p, route_p, counts = _oproj_route(op, wo_bf, hp, g, wr_hi, wr_lo, jnp.zeros((1, LANES), F32))
        hs, xn_s, route_s, counts = _oproj_route(os_, wo_bf, hs, g, wr_hi, wr_lo, counts)
        return _moe(hp, hs, route_p, route_s, counts, xn_p, xn_s, g_final,
                    moe_w_gate[i].astype(BF16), moe_w_up[i].astype(BF16), moe_w_down[i].astype(BF16), final_norm)

    g0 = norm_mix[0].reshape(1, D_MODEL)
    wa_bf = a_w_qkv[0].astype(BF16)
    qkv_p = _proj_a(hp, g0, wa_bf, cs_p)
    qkv_s = _proj_a(hs, g0, wa_bf, cs_s)
    ck = cache_a_k[0].reshape(streams, WINDOW, A_KV_DIM)
    cv = cache_a_v[0].reshape(streams, WINDOW, A_KV_DIM)
    op = _attn_a_prompt(qkv_p, a_sinks[0], seq)
    os_ = _attn_a_sample(qkv_s, ck, cv, a_sinks[0])
    k_p = qkv_p[:, D_MODEL:D_MODEL + A_KV_DIM].reshape(batch, seq, A_KV_HEADS, HEAD_DIM)
    v_p = qkv_p[:, D_MODEL + A_KV_DIM:].reshape(batch, seq, A_KV_HEADS, HEAD_DIM)
    k_s = qkv_s[:, D_MODEL:D_MODEL + A_KV_DIM].reshape(streams, t_new, A_KV_HEADS, HEAD_DIM)
    v_s = qkv_s[:, D_MODEL + A_KV_DIM:].reshape(streams, t_new, A_KV_HEADS, HEAD_DIM)
    new_a_k_prompt = k_p[None, :, -WINDOW:]
    new_a_v_prompt = v_p[None, :, -WINDOW:]
    new_a_k_sample = jnp.concatenate([cache_a_k[0], k_s], axis=1)[None, :, -WINDOW:]
    new_a_v_sample = jnp.concatenate([cache_a_v[0], v_s], axis=1)[None, :, -WINDOW:]
    hp, hs = moe_layer(0, hp, hs, op, os_, a_w_o[0], False)

    g1 = norm_mix[1].reshape(1, D_MODEL)
    wb_bf = b_w_qkv[0].astype(BF16)
    q_p, kf_p, vf_p, kb_p, vb_p = _proj_b(hp, g1, wb_bf)
    q_s, kf_s, vf_s, kb_s, vb_s = _proj_b(hs, g1, wb_bf)
    op = _attn_b_prompt(q_p, kb_p, vb_p, seq)
    os_ = _attn_b_sample(q_s, kb_s, vb_s, cache_b_k[0].reshape(streams, past, D_MODEL),
                         cache_b_v[0].reshape(streams, past, D_MODEL))
    new_b_k_prompt = kf_p.reshape(1, batch, seq, N_HEADS, HEAD_DIM)
    new_b_v_prompt = vf_p.reshape(1, batch, seq, N_HEADS, HEAD_DIM)
    new_b_k_sample = kf_s.reshape(1, streams, t_new, N_HEADS, HEAD_DIM)
    new_b_v_sample = vf_s.reshape(1, streams, t_new, N_HEADS, HEAD_DIM)
    hp, hs = moe_layer(1, hp, hs, op, os_, b_w_o[0], True)

    y_prompt = hp.reshape(batch, seq, D_MODEL)
    y_sample = hs.reshape(streams, t_new, D_MODEL)
    return (y_prompt, y_sample, new_a_k_prompt, new_a_v_prompt, new_a_k_sample, new_a_v_sample,
            new_b_k_prompt, new_b_v_prompt, new_b_k_sample, new_b_v_sample)
```

```python
import functools

import jax
import jax.numpy as jnp
import numpy as np
from jax import lax
from jax.experimental import pallas as pl
from jax.experimental.pallas import tpu as pltpu

F32 = jnp.float32
BF16 = jnp.bfloat16

D_MODEL = 1024
HEAD_DIM = 64
N_HEADS = D_MODEL // HEAD_DIM
A_KV_HEADS = 4
A_GROUP = N_HEADS // A_KV_HEADS
A_KV_DIM = A_KV_HEADS * HEAD_DIM
CHUNK = 64
WINDOW = 128
ROT_DIM = HEAD_DIM // 4
ROPE_THETA = 500000.0
N_GROUPS = 4
EXPERTS_PER_GROUP = 8
N_EXPERTS = N_GROUPS * EXPERTS_PER_GROUP
TOP_K = 2
D_EXPERT = D_MODEL // 2
RMS_EPS = 1e-6
NEG_INF = -1e30
SCALE = HEAD_DIM ** -0.5

LANES = 128
ROW_TILE = 512
ATTN_A_BLOCK = WINDOW
SB_BLOCK = 256
EXPERT_BLOCK = 256
COMBINE_TILE = 256
VMEM_LIMIT = 48 * 1024 * 1024
SB_STAGE_SKEW = 1
SB_DECAY_LIMIT = 105.0


def _params(n_axes, **kw):
    return pltpu.CompilerParams(dimension_semantics=("arbitrary",) * n_axes,
                                vmem_limit_bytes=VMEM_LIMIT, **kw)


def _rmsnorm(x, g):
    return x * lax.rsqrt(jnp.mean(x * x, axis=-1, keepdims=True) + RMS_EPS) * g


def _proj_a_kernel(x_ref, g_ref, w_ref, cs_ref, o_ref):
    xn = _rmsnorm(x_ref[...], g_ref[...]).astype(BF16)
    cos = cs_ref[:, :LANES]
    sin = cs_ref[:, LANES:]
    lane = lax.broadcasted_iota(jnp.int32, cos.shape, 1) % HEAD_DIM
    first_half = lane < ROT_DIM // 2
    n_rot = (D_MODEL + A_KV_DIM) // LANES
    n_all = o_ref.shape[1] // LANES
    for j in range(n_all):
        cols = slice(j * LANES, (j + 1) * LANES)
        blk = jnp.dot(xn, w_ref[:, cols], preferred_element_type=F32)
        if j < n_rot:
            partner = jnp.where(first_half, pltpu.roll(blk, LANES - ROT_DIM // 2, 1),
                                pltpu.roll(blk, ROT_DIM // 2, 1))
            blk = blk * cos + partner * sin
        o_ref[:, cols] = blk


def _rope_table(pos):
    half = ROT_DIM // 2
    inv = ROPE_THETA ** (-jnp.arange(0, ROT_DIM, 2, dtype=F32) / ROT_DIM)
    lane = jnp.arange(LANES, dtype=jnp.int32) % HEAD_DIM
    ang = pos.astype(F32)[:, None] * inv[lane % half][None, :]
    cos = jnp.where(lane < ROT_DIM, jnp.cos(ang), 1.0)
    sin = jnp.sin(ang)
    sin = jnp.where(lane < half, -sin, jnp.where(lane < ROT_DIM, sin, 0.0))
    return jnp.concatenate([cos, sin], axis=1)


def _proj_a(x, g, w_bf, cs):
    n = x.shape[0]
    tm = min(ROW_TILE, n)
    n_out = w_bf.shape[1]
    cs_blocks = cs.shape[0] // tm
    return pl.pallas_call(
        _proj_a_kernel,
        grid=(n // tm,),
        in_specs=[pl.BlockSpec((tm, D_MODEL), lambda i: (i, 0)),
                  pl.BlockSpec((1, D_MODEL), lambda i: (0, 0)),
                  pl.BlockSpec((D_MODEL, n_out), lambda i: (0, 0)),
                  pl.BlockSpec((tm, 2 * LANES), lambda i: (i % cs_blocks, 0))],
        out_specs=pl.BlockSpec((tm, n_out), lambda i: (i, 0)),
        out_shape=jax.ShapeDtypeStruct((n, n_out), F32),
        compiler_params=_params(1),
        name="proj_a",
    )(x, g, w_bf, cs)


def _sink_attention(q, k_bf, v_bf, sink_ref, valid):
    def scores(h):
        g = h // A_GROUP
        qh = q[:, h * HEAD_DIM:(h + 1) * HEAD_DIM].astype(BF16)
        kh = k_bf[:, g * HEAD_DIM:(g + 1) * HEAD_DIM]
        return lax.dot_general(qh, kh, (((1,), (1,)), ((), ())), preferred_element_type=F32) * SCALE

    def attend(h, s):
        g = h // A_GROUP
        vh = v_bf[:, g * HEAD_DIM:(g + 1) * HEAD_DIM]
        if valid is not None:
            s = jnp.where(valid, s, NEG_INF)
        sink = sink_ref[h]
        m = jnp.maximum(jnp.max(s, axis=-1, keepdims=True), sink)
        e = jnp.exp(s - m)
        den = jnp.sum(e, axis=-1, keepdims=True) + jnp.exp(sink - m)
        return jnp.dot(e.astype(BF16), vh, preferred_element_type=F32) / den

    s, outs = {}, []
    for t in range(N_HEADS + 1):
        if t < N_HEADS:
            s[t] = scores(t)
        if t >= 1:
            outs.append(attend(t - 1, s.pop(t - 1)))
    return jnp.concatenate(outs, axis=1).astype(BF16)


def _attn_a_prompt_kernel(sink_ref, q_ref, kp_ref, kc_ref, vp_ref, vc_ref, o_ref, *, blocks_per_seq):
    i = pl.program_id(0)
    has_prev = (i % blocks_per_seq) != 0
    k = jnp.concatenate([kp_ref[...], kc_ref[...]], axis=0).astype(BF16)
    v = jnp.concatenate([vp_ref[...], vc_ref[...]], axis=0).astype(BF16)
    rows, keys = ATTN_A_BLOCK, 2 * ATTN_A_BLOCK
    q_chunk = lax.broadcasted_iota(jnp.int32, (rows, keys), 0) // CHUNK
    col = lax.broadcasted_iota(jnp.int32, (rows, keys), 1)
    k_chunk = col // CHUNK
    valid = (k_chunk >= q_chunk) & (k_chunk <= q_chunk + WINDOW // CHUNK)
    valid = valid & ((col >= ATTN_A_BLOCK) | has_prev)
    o_ref[...] = _sink_attention(q_ref[...], k, v, sink_ref, valid)


def _attn_a_prompt(qkv, sinks, seq):
    n = qkv.shape[0]
    blk = ATTN_A_BLOCK
    blocks_per_seq = seq // blk
    kcol = D_MODEL // A_KV_DIM
    prev = lambda i: jnp.maximum(i - 1, 0)
    return pl.pallas_call(
        functools.partial(_attn_a_prompt_kernel, blocks_per_seq=blocks_per_seq),
        grid=(n // blk,),
        in_specs=[pl.BlockSpec(memory_space=pltpu.SMEM),
                  pl.BlockSpec((blk, D_MODEL), lambda i: (i, 0)),
                  pl.BlockSpec((blk, A_KV_DIM), lambda i: (prev(i), kcol)),
                  pl.BlockSpec((blk, A_KV_DIM), lambda i: (i, kcol)),
                  pl.BlockSpec((blk, A_KV_DIM), lambda i: (prev(i), kcol + 1)),
                  pl.BlockSpec((blk, A_KV_DIM), lambda i: (i, kcol + 1))],
        out_specs=pl.BlockSpec((blk, D_MODEL), lambda i: (i, 0)),
        out_shape=jax.ShapeDtypeStruct((n, D_MODEL), BF16),
        compiler_params=_params(1),
        name="attn_a_prompt",
    )(sinks, qkv, qkv, qkv, qkv, qkv)


def _attn_a_sample_kernel(sink_ref, q_ref, kn_ref, vn_ref, ck_ref, cv_ref, o_ref):
    k = jnp.concatenate([ck_ref[0], kn_ref[...]], axis=0).astype(BF16)
    v = jnp.concatenate([cv_ref[0], vn_ref[...]], axis=0).astype(BF16)
    o_ref[...] = _sink_attention(q_ref[...], k, v, sink_ref, None)


def _attn_a_sample(qkv, cache_k, cache_v, sinks):
    streams = cache_k.shape[0]
    t = qkv.shape[0] // streams
    kcol = D_MODEL // A_KV_DIM
    return pl.pallas_call(
        _attn_a_sample_kernel,
        grid=(streams,),
        in_specs=[pl.BlockSpec(memory_space=pltpu.SMEM),
                  pl.BlockSpec((t, D_MODEL), lambda b: (b, 0)),
                  pl.BlockSpec((t, A_KV_DIM), lambda b: (b, kcol)),
                  pl.BlockSpec((t, A_KV_DIM), lambda b: (b, kcol + 1)),
                  pl.BlockSpec((1, WINDOW, A_KV_DIM), lambda b: (b, 0, 0)),
                  pl.BlockSpec((1, WINDOW, A_KV_DIM), lambda b: (b, 0, 0))],
        out_specs=pl.BlockSpec((t, D_MODEL), lambda b: (b, 0)),
        out_shape=jax.ShapeDtypeStruct((qkv.shape[0], D_MODEL), BF16),
        compiler_params=_params(1),
        name="attn_a_sample",
    )(sinks, qkv, qkv, qkv, cache_k, cache_v)


def _proj_b_kernel(x_ref, g_ref, w_ref, q_ref, k_ref, v_ref, kb_ref, vb_ref):
    xn = _rmsnorm(x_ref[...], g_ref[...]).astype(BF16)
    for j in range(D_MODEL // LANES):
        cols = slice(j * LANES, (j + 1) * LANES)
        q = jnp.dot(xn, w_ref[:, cols], preferred_element_type=F32)
        q_ref[:, cols] = (q * SCALE).astype(BF16)
        k = jnp.dot(xn, w_ref[:, D_MODEL + j * LANES:D_MODEL + (j + 1) * LANES], preferred_element_type=F32)
        kb_ref[:, cols] = k.astype(BF16)
        v = jnp.dot(xn, w_ref[:, 2 * D_MODEL + j * LANES:2 * D_MODEL + (j + 1) * LANES],
                    preferred_element_type=F32)
        vb_ref[:, cols] = v.astype(BF16)
        for half in range(LANES // HEAD_DIM):
            h = j * (LANES // HEAD_DIM) + half
            k_ref[:, h, :] = k[:, half * HEAD_DIM:(half + 1) * HEAD_DIM]
            v_ref[:, h, :] = v[:, half * HEAD_DIM:(half + 1) * HEAD_DIM]


def _proj_b(x, g, w_bf):
    n = x.shape[0]
    tm = min(ROW_TILE, n)
    row = lambda i: (i, 0)
    return pl.pallas_call(
        _proj_b_kernel,
        grid=(n // tm,),
        in_specs=[pl.BlockSpec((tm, D_MODEL), row),
                  pl.BlockSpec((1, D_MODEL), lambda i: (0, 0)),
                  pl.BlockSpec((D_MODEL, 3 * D_MODEL), lambda i: (0, 0))],
        out_specs=[pl.BlockSpec((tm, D_MODEL), row),
                   pl.BlockSpec((tm, N_HEADS, HEAD_DIM), lambda i: (i, 0, 0)),
                   pl.BlockSpec((tm, N_HEADS, HEAD_DIM), lambda i: (i, 0, 0)),
                   pl.BlockSpec((tm, D_MODEL), row),
                   pl.BlockSpec((tm, D_MODEL), row)],
        out_shape=[jax.ShapeDtypeStruct((n, D_MODEL), BF16),
                   jax.ShapeDtypeStruct((n, N_HEADS, HEAD_DIM), F32),
                   jax.ShapeDtypeStruct((n, N_HEADS, HEAD_DIM), F32),
                   jax.ShapeDtypeStruct((n, D_MODEL), BF16),
                   jax.ShapeDtypeStruct((n, D_MODEL), BF16)],
        compiler_params=_params(1),
        name="proj_b",
    )(x, g, w_bf)


def _strict_lower_ones(n):
    j = lax.broadcasted_iota(jnp.int32, (2 * n, n), 0) % n
    s = lax.broadcasted_iota(jnp.int32, (2 * n, n), 1)
    return jnp.where(j > s, 1.0, 0.0).astype(BF16)


def _sb_scores(qh, kh):
    return lax.dot_general(qh, kh, (((1,), (1,)), ((), ())), preferred_element_type=F32)


def _sb_suffix(z, tri, before):
    softplus = jnp.maximum(z, 0.0) + jnp.log(1.0 + jnp.exp(-jnp.abs(z)))
    log_beta = z - softplus
    if before is not None:
        softplus = jnp.where(before, softplus, 0.0)
    hi = softplus.astype(BF16)
    lo = (softplus - hi.astype(F32)).astype(BF16)
    later = jnp.dot(jnp.concatenate([hi, lo], axis=1), tri, preferred_element_type=F32)
    return log_beta, later, later[:, 0:1] + softplus[:, 0:1]


def _sb_values(log_beta, later, vh, before):
    a = jnp.exp(log_beta - later)
    if before is not None:
        a = jnp.where(before, a, 0.0)
    return jnp.dot(a.astype(BF16), vh, preferred_element_type=F32)


def _head_cols(ref):
    return lambda h: ref[:, h * HEAD_DIM:(h + 1) * HEAD_DIM].astype(BF16)


def _sb_block(q_ref, k_head, v_head, tri, carry_ref, acc_ref, before):
    first_head = lax.broadcasted_iota(jnp.int32, carry_ref.shape[1:], 1) < HEAD_DIM
    q_head = _head_cols(q_ref)
    carries = [carry_ref[pair] for pair in range(N_HEADS // 2)]
    z, mid, outs, masses = {}, {}, {}, {}
    for t in range(N_HEADS + 2 * SB_STAGE_SKEW):
        if t < N_HEADS:
            z[t] = _sb_scores(q_head(t), k_head(t))
        h = t - SB_STAGE_SKEW
        if 0 <= h < N_HEADS:
            log_beta, later, masses[h] = _sb_suffix(z.pop(h), tri, before)
            mid[h] = (log_beta, later)
        h = t - 2 * SB_STAGE_SKEW
        if 0 <= h < N_HEADS:
            outs[h] = _sb_values(*mid.pop(h), v_head(h), before)
    adds = []
    for pair in range(N_HEADS // 2):
        h = 2 * pair
        adds.append(jnp.exp(-carries[pair]) * jnp.concatenate([outs[h], outs[h + 1]], axis=1))
        carries[pair] = carries[pair] + jnp.where(first_head, masses[h], masses[h + 1])
    acc_ref[...] += jnp.concatenate(adds, axis=1)
    least = carries[0]
    for pair in range(N_HEADS // 2):
        carry_ref[pair] = carries[pair]
        least = jnp.minimum(least, carries[pair])
    return (jnp.min(least) <= SB_DECAY_LIMIT).astype(jnp.int32)


def _sb_old_blocks(n_old, fetch, block, live):
    @pl.when(n_old > 0)
    def _():
        for copy in fetch(1, 0):
            copy.start()

    def cond(state):
        j, live = state
        return jnp.logical_and(j <= n_old, live == 1)

    def body(state):
        j, _ = state
        slot = (j - 1) % 2
        for copy in fetch(j, slot):
            copy.wait()

        @pl.when(j < n_old)
        def _():
            for copy in fetch(j + 1, 1 - slot):
                copy.start()

        return j + 1, block(slot)

    j_end, _ = lax.while_loop(cond, body, (jnp.int32(1), live))

    @pl.when(j_end <= n_old)
    def _():
        for copy in fetch(j_end, (j_end - 1) % 2):
            copy.wait()


def _sb_prompt_kernel(q_ref, kd_ref, vd_ref, k_hbm, v_hbm, o_ref, kbuf, vbuf, sem, acc_ref, carry_ref,
                      *, blocks_per_seq):
    i = pl.program_id(0)
    n = SB_BLOCK
    tri = _strict_lower_ones(n)

    def fetch(j, slot):
        rows = pl.ds((i - j) * n, n)
        return (pltpu.make_async_copy(k_hbm.at[rows], kbuf.at[slot], sem.at[0, slot]),
                pltpu.make_async_copy(v_hbm.at[rows], vbuf.at[slot], sem.at[1, slot]))

    def old_block(slot):
        return _sb_block(q_ref, _head_cols(kbuf.at[slot]), _head_cols(vbuf.at[slot]), tri, carry_ref, acc_ref, None)

    acc_ref[...] = jnp.zeros_like(acc_ref)
    carry_ref[...] = jnp.zeros_like(carry_ref)
    row = lax.broadcasted_iota(jnp.int32, (n, n), 0)
    col = lax.broadcasted_iota(jnp.int32, (n, n), 1)
    live = _sb_block(q_ref, _head_cols(kd_ref), _head_cols(vd_ref), tri, carry_ref, acc_ref, col < row)
    _sb_old_blocks(i % blocks_per_seq, fetch, old_block, live)
    o_ref[...] = acc_ref[...].astype(BF16)


def _attn_b_prompt(q_bf, k_bf, v_bf, seq):
    n = q_bf.shape[0]
    blk = SB_BLOCK
    row = lambda i: (i, 0)
    return pl.pallas_call(
        functools.partial(_sb_prompt_kernel, blocks_per_seq=seq // blk),
        grid=(n // blk,),
        in_specs=[pl.BlockSpec((blk, D_MODEL), row),
                  pl.BlockSpec((blk, D_MODEL), row),
                  pl.BlockSpec((blk, D_MODEL), row),
                  pl.BlockSpec(memory_space=pl.ANY),
                  pl.BlockSpec(memory_space=pl.ANY)],
        out_specs=pl.BlockSpec((blk, D_MODEL), row),
        out_shape=jax.ShapeDtypeStruct((n, D_MODEL), BF16),
        scratch_shapes=[pltpu.VMEM((2, blk, D_MODEL), BF16),
                        pltpu.VMEM((2, blk, D_MODEL), BF16),
                        pltpu.SemaphoreType.DMA((2, 2)),
                        pltpu.VMEM((blk, D_MODEL), F32),
                        pltpu.VMEM((N_HEADS // 2, blk, LANES), F32)],
        compiler_params=_params(1),
        name="attn_b_prompt",
    )(q_bf, k_bf, v_bf, k_bf, v_bf)


def _sb_sample_kernel(q_ref, kn_ref, vn_ref, ck_hbm, cv_hbm, o_ref, kbuf, vbuf, sem, acc_ref, carry_ref,
                      *, n_cache):
    b = pl.program_id(0)
    t = q_ref.shape[0]
    n = SB_BLOCK
    tri = _strict_lower_ones(n)

    def fetch(j, slot):
        rows = pl.ds((n_cache - j) * n, n)
        return (pltpu.make_async_copy(ck_hbm.at[b, rows], kbuf.at[slot], sem.at[0, slot]),
                pltpu.make_async_copy(cv_hbm.at[b, rows], vbuf.at[slot], sem.at[1, slot]))

    def cache_heads(buf):
        return lambda h: buf[:, h, :].astype(BF16)

    def old_block(slot):
        return _sb_block(q_ref, cache_heads(kbuf.at[slot]), cache_heads(vbuf.at[slot]), tri, carry_ref, acc_ref,
                         None)

    acc_ref[...] = jnp.zeros_like(acc_ref)
    carry_ref[...] = jnp.zeros_like(carry_ref)
    row = lax.broadcasted_iota(jnp.int32, (t, t), 0)
    col = lax.broadcasted_iota(jnp.int32, (t, t), 1)
    live = _sb_block(q_ref, _head_cols(kn_ref), _head_cols(vn_ref), _strict_lower_ones(t), carry_ref, acc_ref,
                     col < row)
    _sb_old_blocks(jnp.int32(n_cache), fetch, old_block, live)
    o_ref[...] = acc_ref[...].astype(BF16)


def _attn_b_sample(q_bf, k_bf, v_bf, cache_k, cache_v):
    streams, past = cache_k.shape[:2]
    t = q_bf.shape[0] // streams
    blk = SB_BLOCK
    new = lambda b: (b, 0)
    return pl.pallas_call(
        functools.partial(_sb_sample_kernel, n_cache=past // blk),
        grid=(streams,),
        in_specs=[pl.BlockSpec((t, D_MODEL), new),
                  pl.BlockSpec((t, D_MODEL), new),
                  pl.BlockSpec((t, D_MODEL), new),
                  pl.BlockSpec(memory_space=pl.ANY),
                  pl.BlockSpec(memory_space=pl.ANY)],
        out_specs=pl.BlockSpec((t, D_MODEL), new),
        out_shape=jax.ShapeDtypeStruct(q_bf.shape, BF16),
        scratch_shapes=[pltpu.VMEM((2, blk, N_HEADS, HEAD_DIM), F32),
                        pltpu.VMEM((2, blk, N_HEADS, HEAD_DIM), F32),
                        pltpu.SemaphoreType.DMA((2, 2)),
                        pltpu.VMEM((t, D_MODEL), F32),
                        pltpu.VMEM((N_HEADS // 2, t, LANES), F32)],
        compiler_params=_params(1),
        name="attn_b_sample",
    )(q_bf, k_bf, v_bf, cache_k, cache_v)


ROUTE_COLS = N_GROUPS + N_EXPERTS


def _oproj_route_kernel(o_ref, wo_ref, h_ref, g_ref, wr_hi_ref, wr_lo_ref, base_ref,
                        h_out_ref, xn_ref, route_ref, count_ref):
    @pl.when(pl.program_id(0) == 0)
    def _():
        count_ref[...] = base_ref[...]

    h = h_ref[...] + jnp.dot(o_ref[...], wo_ref[...], preferred_element_type=F32)
    h_out_ref[...] = h
    xn = _rmsnorm(h, g_ref[...])
    xn_ref[...] = xn
    x_hi = xn.astype(BF16)
    x_lo = (xn - x_hi.astype(F32)).astype(BF16)
    logits = (jnp.dot(x_hi, wr_hi_ref[...], preferred_element_type=F32)
              + jnp.dot(x_hi, wr_lo_ref[...], preferred_element_type=F32)
              + jnp.dot(x_lo, wr_hi_ref[...], preferred_element_type=F32))
    lane = lax.broadcasted_iota(jnp.int32, logits.shape, 1)
    big = jnp.int32(2 * LANES)

    def first_argmax(vals, peak):
        return jnp.min(jnp.where(vals == peak, lane, big), axis=-1, keepdims=True)

    g_mask = lane < N_GROUPS
    g_logits = jnp.where(g_mask, logits, -jnp.inf)
    g_max = jnp.max(g_logits, axis=-1, keepdims=True)
    g_idx = first_argmax(g_logits, g_max)
    g_sum = jnp.sum(jnp.where(g_mask, jnp.exp(logits - g_max), 0.0), axis=-1, keepdims=True)
    g_p = 1.0 / g_sum

    e_lo = N_GROUPS + g_idx * EXPERTS_PER_GROUP
    e_mask = (lane >= e_lo) & (lane < e_lo + EXPERTS_PER_GROUP)
    e_logits = jnp.where(e_mask, logits, -jnp.inf)
    m1 = jnp.max(e_logits, axis=-1, keepdims=True)
    i1 = first_argmax(e_logits, m1)
    rest = jnp.where(lane == i1, -jnp.inf, e_logits)
    m2 = jnp.max(rest, axis=-1, keepdims=True)
    i2 = first_argmax(rest, m2)
    e_sum = jnp.sum(jnp.where(e_mask, jnp.exp(logits - m1), 0.0), axis=-1, keepdims=True)
    p1 = 1.0 / e_sum
    p2 = jnp.exp(m2 - m1) / e_sum
    top_sum = p1 + p2
    w1 = g_p * p1 / top_sum
    w2 = g_p * p2 / top_sum
    e1 = i1 - N_GROUPS
    e2 = i2 - N_GROUPS
    pick1 = lane == e1
    pick2 = lane == e2
    picks = jnp.where(pick1 | pick2, 1.0, 0.0)
    tm = picks.shape[0]
    earlier = jnp.where(lax.broadcasted_iota(jnp.int32, (tm, tm), 1) < lax.broadcasted_iota(jnp.int32, (tm, tm), 0),
                        1.0, 0.0).astype(BF16)
    counts = count_ref[...]
    before = jnp.dot(earlier, picks.astype(BF16), preferred_element_type=F32) + counts
    rank1 = jnp.sum(jnp.where(pick1, before, 0.0), axis=-1, keepdims=True)
    rank2 = jnp.sum(jnp.where(pick2, before, 0.0), axis=-1, keepdims=True)
    count_ref[...] = counts + jnp.sum(picks, axis=0, keepdims=True)
    cols = (e1.astype(F32), e2.astype(F32), w1, w2, rank1, rank2)
    route = jnp.zeros_like(logits)
    for c, val in enumerate(cols):
        route = jnp.where(lane == c, val, route)
    route_ref[...] = route


def _oproj_route(o_bf, wo_bf, h, g, wr_hi, wr_lo, base_counts):
    n = h.shape[0]
    tm = min(ROW_TILE, n)
    row = lambda i: (i, 0)
    fixed = lambda i: (0, 0)
    return pl.pallas_call(
        _oproj_route_kernel,
        grid=(n // tm,),
        in_specs=[pl.BlockSpec((tm, D_MODEL), row),
                  pl.BlockSpec((D_MODEL, D_MODEL), fixed),
                  pl.BlockSpec((tm, D_MODEL), row),
                  pl.BlockSpec((1, D_MODEL), fixed),
                  pl.BlockSpec((D_MODEL, LANES), fixed),
                  pl.BlockSpec((D_MODEL, LANES), fixed),
                  pl.BlockSpec((1, LANES), fixed)],
        out_specs=[pl.BlockSpec((tm, D_MODEL), row),
                   pl.BlockSpec((tm, D_MODEL), row),
                   pl.BlockSpec((tm, LANES), row),
                   pl.BlockSpec((1, LANES), fixed)],
        out_shape=[jax.ShapeDtypeStruct((n, D_MODEL), F32),
                   jax.ShapeDtypeStruct((n, D_MODEL), F32),
                   jax.ShapeDtypeStruct((n, LANES), F32),
                   jax.ShapeDtypeStruct((1, LANES), F32)],
        compiler_params=_params(1),
        name="oproj_route",
    )(o_bf, wo_bf, h, g, wr_hi, wr_lo, base_counts)


def _dispatch_kernel(dest_ref, xn_ref, buf_in_ref, buf_ref, sem):
    del buf_in_ref
    i = pl.program_id(0)
    tm = xn_ref.shape[0]
    base = i * (tm * TOP_K)

    def row_copy(t, slot):
        return pltpu.make_async_copy(xn_ref.at[pl.ds(t, 1)], buf_ref.at[pl.ds(slot, 1)], sem)

    def issue(t, carry):
        for k in range(TOP_K):
            row_copy(t, dest_ref[base + TOP_K * t + k]).start()
        return carry

    lax.fori_loop(0, tm, issue, 0, unroll=8)
    for k in range(TOP_K):
        pltpu.make_async_copy(xn_ref, buf_ref.at[pl.ds(0, tm)], sem).wait()


def _dispatch(dest, xn, buf):
    n = xn.shape[0]
    tm = min(ROW_TILE, n)
    grid_spec = pltpu.PrefetchScalarGridSpec(
        num_scalar_prefetch=1,
        grid=(n // tm,),
        in_specs=[pl.BlockSpec((tm, D_MODEL), lambda i, d: (i, 0)),
                  pl.BlockSpec(memory_space=pl.ANY)],
        out_specs=pl.BlockSpec(memory_space=pl.ANY),
        scratch_shapes=[pltpu.SemaphoreType.DMA(())])
    return pl.pallas_call(
        _dispatch_kernel,
        grid_spec=grid_spec,
        out_shape=jax.ShapeDtypeStruct(buf.shape, buf.dtype),
        input_output_aliases={2: 0},
        compiler_params=_params(1, disable_bounds_checks=True, has_side_effects=True),
        name="moe_dispatch",
    )(dest, xn, buf)


def _expert_kernel(be_ref, nused_ref, x_ref, wg_ref, wu_ref, wd_ref, o_ref, wg_bf, wu_bf, wd_bf):
    i = pl.program_id(0)
    used = i < nused_ref[0]

    @pl.when(used)
    def _():
        @pl.when(jnp.logical_or(i == 0, be_ref[i] != be_ref[jnp.maximum(i - 1, 0)]))
        def _():
            wg_bf[...] = wg_ref[0].astype(BF16)
            wu_bf[...] = wu_ref[0].astype(BF16)
            wd_bf[...] = wd_ref[0].astype(BF16)

        x = x_ref[...].astype(BF16)
        a = jnp.dot(x, wg_bf[...], preferred_element_type=F32)
        b = jnp.dot(x, wu_bf[...], preferred_element_type=F32)
        mid = (a * (1.0 / (1.0 + jnp.exp(-a))) * b).astype(BF16)
        o_ref[...] = jnp.dot(mid, wd_bf[...], preferred_element_type=F32)

    @pl.when(jnp.logical_not(used))
    def _():
        o_ref[...] = jnp.zeros_like(o_ref)


def _experts(block_e, n_used, buf, w_gate, w_up, w_down, layer):
    bm = EXPERT_BLOCK
    n_blocks = buf.shape[0] // bm
    blk = lambda i, be, nu: (jnp.minimum(i, nu[0] - 1), 0)
    wsel = lambda i, be, nu: (layer, be[jnp.minimum(i, nu[0] - 1)], 0, 0)
    grid_spec = pltpu.PrefetchScalarGridSpec(
        num_scalar_prefetch=2,
        grid=(n_blocks,),
        in_specs=[pl.BlockSpec((bm, D_MODEL), blk),
                  pl.BlockSpec((None, 1, D_MODEL, D_EXPERT), wsel),
                  pl.BlockSpec((None, 1, D_MODEL, D_EXPERT), wsel),
                  pl.BlockSpec((None, 1, D_EXPERT, D_MODEL), wsel)],
        out_specs=pl.BlockSpec((bm, D_MODEL), lambda i, be, nu: (i, 0)),
        scratch_shapes=[pltpu.VMEM((D_MODEL, D_EXPERT), BF16),
                        pltpu.VMEM((D_MODEL, D_EXPERT), BF16),
                        pltpu.VMEM((D_EXPERT, D_MODEL), BF16)])
    return pl.pallas_call(
        _expert_kernel,
        grid_spec=grid_spec,
        out_shape=jax.ShapeDtypeStruct(buf.shape, F32),
        compiler_params=_params(1),
        name="moe_experts",
    )(block_e, n_used, buf, w_gate, w_up, w_down)


def _combine_kernel(dest_ref, h_ref, route_ref, g_ref, yb_ref, o_ref, rows_ref, sem, *, final_norm):
    i = pl.program_id(0)
    n_steps = pl.num_programs(0)
    tm = h_ref.shape[0]

    def row_copy(step, t, k):
        slot = step % 2
        src = dest_ref[step * (tm * TOP_K) + TOP_K * t + k]
        return pltpu.make_async_copy(yb_ref.at[pl.ds(src, 1)], rows_ref.at[slot, k, pl.ds(t, 1)], sem.at[slot])

    def issue(step):
        def body(t, carry):
            for k in range(TOP_K):
                row_copy(step, t, k).start()
            return carry
        lax.fori_loop(0, tm, body, 0, unroll=8)

    @pl.when(i == 0)
    def _():
        issue(i)

    @pl.when(i + 1 < n_steps)
    def _():
        issue(i + 1)

    slot = i % 2
    for k in range(TOP_K):
        pltpu.make_async_copy(yb_ref.at[pl.ds(0, tm)], rows_ref.at[slot, k], sem.at[slot]).wait()

    route = route_ref[...]
    w0 = route[:, 2:3]
    w1 = route[:, 3:4]
    out = h_ref[...] + (rows_ref[slot, 0] * w0 + rows_ref[slot, 1] * w1)
    if final_norm:
        out = _rmsnorm(out, g_ref[...])
    o_ref[...] = out


def _combine(dest, h, route, g, yb, final_norm):
    n = h.shape[0]
    tm = min(COMBINE_TILE, n)
    grid_spec = pltpu.PrefetchScalarGridSpec(
        num_scalar_prefetch=1,
        grid=(n // tm,),
        in_specs=[pl.BlockSpec((tm, D_MODEL), lambda i, d: (i, 0)),
                  pl.BlockSpec((tm, LANES), lambda i, d: (i, 0)),
                  pl.BlockSpec((1, D_MODEL), lambda i, d: (0, 0)),
                  pl.BlockSpec(memory_space=pl.ANY)],
        out_specs=pl.BlockSpec((tm, D_MODEL), lambda i, d: (i, 0)),
        scratch_shapes=[pltpu.VMEM((2, TOP_K, tm, D_MODEL), F32),
                        pltpu.SemaphoreType.DMA((2,))])
    return pl.pallas_call(
        functools.partial(_combine_kernel, final_norm=final_norm),
        grid_spec=grid_spec,
        out_shape=jax.ShapeDtypeStruct(h.shape, F32),
        compiler_params=_params(1, disable_bounds_checks=True),
        name="moe_combine",
    )(dest, h, route, g, yb)


def _route_plan(route_p, route_s, counts):
    route = jnp.concatenate([route_p[:, :3 * TOP_K], route_s[:, :3 * TOP_K]], axis=0)
    flat_e = route[:, :TOP_K].astype(jnp.int32).reshape(-1)
    rank = route[:, 2 * TOP_K:].astype(jnp.int32).reshape(-1)
    n_assign = flat_e.shape[0]
    counts = counts[0, :N_EXPERTS].astype(jnp.int32)
    bm = EXPERT_BLOCK
    padded = (counts + bm - 1) // bm * bm
    pends = jnp.cumsum(padded)
    onehot = flat_e[:, None] == jnp.arange(N_EXPERTS, dtype=jnp.int32)[None, :]
    dest = jnp.sum(jnp.where(onehot, (pends - padded)[None, :], 0), axis=1) + rank
    n_blocks = -(-n_assign // bm) + N_EXPERTS
    first_rows = jnp.arange(n_blocks, dtype=jnp.int32) * bm
    block_e = jnp.minimum(jnp.sum((pends[None, :] <= first_rows[:, None]).astype(jnp.int32), axis=1), N_EXPERTS - 1)
    n_used = (pends[-1] // bm).astype(jnp.int32).reshape(1)
    return dest.astype(jnp.int32), block_e, n_used, n_blocks


def _moe(hp, hs, route_p, route_s, counts, xn_p, xn_s, g_final, w_gate, w_up, w_down, layer, final_norm):
    dest, block_e, n_used, n_blocks = _route_plan(route_p, route_s, counts)
    n_p = hp.shape[0] * TOP_K
    dest_p, dest_s = dest[:n_p], dest[n_p:]
    buf = jnp.zeros((n_blocks * EXPERT_BLOCK, D_MODEL), F32)
    buf = _dispatch(dest_p, xn_p, buf)
    buf = _dispatch(dest_s, xn_s, buf)
    yb = _experts(block_e, n_used, buf, w_gate, w_up, w_down, layer)
    out_p = _combine(dest_p, hp, route_p, g_final, yb, final_norm)
    out_s = _combine(dest_s, hs, route_s, g_final, yb, final_norm)
    return out_p, out_s


def _router_weights(w_group, w_router):
    w_exp = jnp.transpose(w_router, (1, 0, 2)).reshape(D_MODEL, N_EXPERTS)
    w = jnp.concatenate([w_group, w_exp, jnp.zeros((D_MODEL, LANES - ROUTE_COLS), F32)], axis=1)
    hi = w.astype(BF16)
    lo = (w - hi.astype(F32)).astype(BF16)
    return hi, lo


def kernel(x_prompt, x_sample, cache_a_k, cache_a_v, cache_b_k, cache_b_v, norm_mix, norm_ffn, norm_final,
           a_w_qkv, a_w_o, a_sinks, b_w_qkv, b_w_o, moe_w_group, moe_w_router, moe_w_gate, moe_w_up, moe_w_down):
    batch, seq, _ = x_prompt.shape
    streams, t_new, _ = x_sample.shape
    past = cache_b_k.shape[2]
    hp = x_prompt.reshape(batch * seq, D_MODEL)
    hs = x_sample.reshape(streams * t_new, D_MODEL)
    g_final = norm_final.reshape(1, D_MODEL)

    cs_p = _rope_table(jnp.arange(seq, dtype=jnp.int32))
    cs_s = _rope_table(jnp.tile(past + jnp.arange(t_new, dtype=jnp.int32), streams))

    def moe_layer(i, hp, hs, op, os_, w_o, final_norm):
        wo_bf = w_o.astype(BF16)
        g = norm_ffn[i].reshape(1, D_MODEL)
        wr_hi, wr_lo = _router_weights(moe_w_group[i], moe_w_router[i])
        hp, xn_p, route_p, counts = _oproj_route(op, wo_bf, hp, g, wr_hi, wr_lo, jnp.zeros((1, LANES), F32))
        hs, xn_s, route_s, counts = _oproj_route(os_, wo_bf, hs, g, wr_hi, wr_lo, counts)
        return _moe(hp, hs, route_p, route_s, counts, xn_p, xn_s, g_final,
                    moe_w_gate, moe_w_up, moe_w_down, i, final_norm)

    g0 = norm_mix[0].reshape(1, D_MODEL)
    wa_bf = a_w_qkv[0].astype(BF16)
    qkv_p = _proj_a(hp, g0, wa_bf, cs_p)
    qkv_s = _proj_a(hs, g0, wa_bf, cs_s)
    ck = cache_a_k[0].reshape(streams, WINDOW, A_KV_DIM)
    cv = cache_a_v[0].reshape(streams, WINDOW, A_KV_DIM)
    op = _attn_a_prompt(qkv_p, a_sinks[0], seq)
    os_ = _attn_a_sample(qkv_s, ck, cv, a_sinks[0])
    tail_p = qkv_p.reshape(batch, seq, -1)[:, -WINDOW:, D_MODEL:]
    new_a_k_prompt = tail_p[:, :, :A_KV_DIM].reshape(1, batch, WINDOW, A_KV_HEADS, HEAD_DIM)
    new_a_v_prompt = tail_p[:, :, A_KV_DIM:].reshape(1, batch, WINDOW, A_KV_HEADS, HEAD_DIM)
    k_s = qkv_s[:, D_MODEL:D_MODEL + A_KV_DIM].reshape(streams, t_new, A_KV_HEADS, HEAD_DIM)
    v_s = qkv_s[:, D_MODEL + A_KV_DIM:].reshape(streams, t_new, A_KV_HEADS, HEAD_DIM)
    new_a_k_sample = jnp.concatenate([cache_a_k[0], k_s], axis=1)[None, :, -WINDOW:]
    new_a_v_sample = jnp.concatenate([cache_a_v[0], v_s], axis=1)[None, :, -WINDOW:]
    hp, hs = moe_layer(0, hp, hs, op, os_, a_w_o[0], False)

    g1 = norm_mix[1].reshape(1, D_MODEL)
    wb_bf = b_w_qkv[0].astype(BF16)
    q_p, kf_p, vf_p, kb_p, vb_p = _proj_b(hp, g1, wb_bf)
    q_s, kf_s, vf_s, kb_s, vb_s = _proj_b(hs, g1, wb_bf)
    op = _attn_b_prompt(q_p, kb_p, vb_p, seq)
    os_ = _attn_b_sample(q_s, kb_s, vb_s, cache_b_k[0], cache_b_v[0])
    new_b_k_prompt = kf_p.reshape(1, batch, seq, N_HEADS, HEAD_DIM)
    new_b_v_prompt = vf_p.reshape(1, batch, seq, N_HEADS, HEAD_DIM)
    new_b_k_sample = kf_s.reshape(1, streams, t_new, N_HEADS, HEAD_DIM)
    new_b_v_sample = vf_s.reshape(1, streams, t_new, N_HEADS, HEAD_DIM)
    hp, hs = moe_layer(1, hp, hs, op, os_, b_w_o[0], True)

    y_prompt = hp.reshape(batch, seq, D_MODEL)
    y_sample = hs.reshape(streams, t_new, D_MODEL)
    return (y_prompt, y_sample, new_a_k_prompt, new_a_v_prompt, new_a_k_sample, new_a_v_sample,
            new_b_k_prompt, new_b_v_prompt, new_b_k_sample, new_b_v_sample)
```

```python
import functools

import jax
import jax.numpy as jnp
import numpy as np
from jax import lax
from jax.experimental import pallas as pl
from jax.experimental.pallas import tpu as pltpu

F32 = jnp.float32
BF16 = jnp.bfloat16

D_MODEL = 1024
HEAD_DIM = 64
N_HEADS = D_MODEL // HEAD_DIM
A_KV_HEADS = 4
A_GROUP = N_HEADS // A_KV_HEADS
A_KV_DIM = A_KV_HEADS * HEAD_DIM
CHUNK = 64
WINDOW = 128
ROT_DIM = HEAD_DIM // 4
ROPE_THETA = 500000.0
N_GROUPS = 4
EXPERTS_PER_GROUP = 8
N_EXPERTS = N_GROUPS * EXPERTS_PER_GROUP
TOP_K = 2
D_EXPERT = D_MODEL // 2
RMS_EPS = 1e-6
NEG_INF = -1e30
SCALE = HEAD_DIM ** -0.5

LANES = 128
ROW_TILE = 512
ATTN_A_BLOCK = WINDOW
SB_BLOCK = 256
EXPERT_BLOCK = 256
COMBINE_TILE = 256
VMEM_LIMIT = 48 * 1024 * 1024
SB_STAGE_SKEW = 1
SB_DECAY_LIMIT = 105.0


def _params(n_axes, **kw):
    return pltpu.CompilerParams(dimension_semantics=("arbitrary",) * n_axes,
                                vmem_limit_bytes=VMEM_LIMIT, **kw)


def _rmsnorm(x, g):
    return x * lax.rsqrt(jnp.mean(x * x, axis=-1, keepdims=True) + RMS_EPS) * g


def _proj_a_kernel(x_ref, g_ref, w_ref, cs_ref, o_ref):
    xn = _rmsnorm(x_ref[...], g_ref[...]).astype(BF16)
    cos = cs_ref[:, :LANES]
    sin = cs_ref[:, LANES:]
    lane = lax.broadcasted_iota(jnp.int32, cos.shape, 1) % HEAD_DIM
    first_half = lane < ROT_DIM // 2
    n_rot = (D_MODEL + A_KV_DIM) // LANES
    n_all = o_ref.shape[1] // LANES
    for j in range(n_all):
        cols = slice(j * LANES, (j + 1) * LANES)
        blk = jnp.dot(xn, w_ref[:, cols], preferred_element_type=F32)
        if j < n_rot:
            partner = jnp.where(first_half, pltpu.roll(blk, LANES - ROT_DIM // 2, 1),
                                pltpu.roll(blk, ROT_DIM // 2, 1))
            blk = blk * cos + partner * sin
        o_ref[:, cols] = blk


def _rope_table(pos):
    half = ROT_DIM // 2
    inv = ROPE_THETA ** (-jnp.arange(0, ROT_DIM, 2, dtype=F32) / ROT_DIM)
    lane = jnp.arange(LANES, dtype=jnp.int32) % HEAD_DIM
    ang = pos.astype(F32)[:, None] * inv[lane % half][None, :]
    cos = jnp.where(lane < ROT_DIM, jnp.cos(ang), 1.0)
    sin = jnp.sin(ang)
    sin = jnp.where(lane < half, -sin, jnp.where(lane < ROT_DIM, sin, 0.0))
    return jnp.concatenate([cos, sin], axis=1)


def _proj_a(x, g, w_bf, cs):
    n = x.shape[0]
    tm = min(ROW_TILE, n)
    n_out = w_bf.shape[1]
    cs_blocks = cs.shape[0] // tm
    return pl.pallas_call(
        _proj_a_kernel,
        grid=(n // tm,),
        in_specs=[pl.BlockSpec((tm, D_MODEL), lambda i: (i, 0)),
                  pl.BlockSpec((1, D_MODEL), lambda i: (0, 0)),
                  pl.BlockSpec((D_MODEL, n_out), lambda i: (0, 0)),
                  pl.BlockSpec((tm, 2 * LANES), lambda i: (i % cs_blocks, 0))],
        out_specs=pl.BlockSpec((tm, n_out), lambda i: (i, 0)),
        out_shape=jax.ShapeDtypeStruct((n, n_out), F32),
        compiler_params=_params(1),
        name="proj_a",
    )(x, g, w_bf, cs)


def _sink_attention(q, k_bf, v_bf, sink_ref, valid):
    def scores(h):
        g = h // A_GROUP
        qh = q[:, h * HEAD_DIM:(h + 1) * HEAD_DIM].astype(BF16)
        kh = k_bf[:, g * HEAD_DIM:(g + 1) * HEAD_DIM]
        return lax.dot_general(qh, kh, (((1,), (1,)), ((), ())), preferred_element_type=F32) * SCALE

    def attend(h, s):
        g = h // A_GROUP
        vh = v_bf[:, g * HEAD_DIM:(g + 1) * HEAD_DIM]
        if valid is not None:
            s = jnp.where(valid, s, NEG_INF)
        sink = sink_ref[h]
        m = jnp.maximum(jnp.max(s, axis=-1, keepdims=True), sink)
        e = jnp.exp(s - m)
        den = jnp.sum(e, axis=-1, keepdims=True) + jnp.exp(sink - m)
        return jnp.dot(e.astype(BF16), vh, preferred_element_type=F32) / den

    s, outs = {}, []
    for t in range(N_HEADS + 1):
        if t < N_HEADS:
            s[t] = scores(t)
        if t >= 1:
            outs.append(attend(t - 1, s.pop(t - 1)))
    return jnp.concatenate(outs, axis=1).astype(BF16)


def _attn_a_prompt_kernel(sink_ref, q_ref, kp_ref, kc_ref, vp_ref, vc_ref, o_ref, *, blocks_per_seq):
    i = pl.program_id(0)
    has_prev = (i % blocks_per_seq) != 0
    k = jnp.concatenate([kp_ref[...], kc_ref[...]], axis=0).astype(BF16)
    v = jnp.concatenate([vp_ref[...], vc_ref[...]], axis=0).astype(BF16)
    rows, keys = ATTN_A_BLOCK, 2 * ATTN_A_BLOCK
    q_chunk = lax.broadcasted_iota(jnp.int32, (rows, keys), 0) // CHUNK
    col = lax.broadcasted_iota(jnp.int32, (rows, keys), 1)
    k_chunk = col // CHUNK
    valid = (k_chunk >= q_chunk) & (k_chunk <= q_chunk + WINDOW // CHUNK)
    valid = valid & ((col >= ATTN_A_BLOCK) | has_prev)
    o_ref[...] = _sink_attention(q_ref[...], k, v, sink_ref, valid)


def _attn_a_prompt(qkv, sinks, seq):
    n = qkv.shape[0]
    blk = ATTN_A_BLOCK
    blocks_per_seq = seq // blk
    kcol = D_MODEL // A_KV_DIM
    prev = lambda i: jnp.maximum(i - 1, 0)
    return pl.pallas_call(
        functools.partial(_attn_a_prompt_kernel, blocks_per_seq=blocks_per_seq),
        grid=(n // blk,),
        in_specs=[pl.BlockSpec(memory_space=pltpu.SMEM),
                  pl.BlockSpec((blk, D_MODEL), lambda i: (i, 0)),
                  pl.BlockSpec((blk, A_KV_DIM), lambda i: (prev(i), kcol)),
                  pl.BlockSpec((blk, A_KV_DIM), lambda i: (i, kcol)),
                  pl.BlockSpec((blk, A_KV_DIM), lambda i: (prev(i), kcol + 1)),
                  pl.BlockSpec((blk, A_KV_DIM), lambda i: (i, kcol + 1))],
        out_specs=pl.BlockSpec((blk, D_MODEL), lambda i: (i, 0)),
        out_shape=jax.ShapeDtypeStruct((n, D_MODEL), BF16),
        compiler_params=_params(1),
        name="attn_a_prompt",
    )(sinks, qkv, qkv, qkv, qkv, qkv)


def _attn_a_sample_kernel(sink_ref, q_ref, kn_ref, vn_ref, ck_ref, cv_ref, o_ref):
    k = jnp.concatenate([ck_ref[0], kn_ref[...]], axis=0).astype(BF16)
    v = jnp.concatenate([cv_ref[0], vn_ref[...]], axis=0).astype(BF16)
    o_ref[...] = _sink_attention(q_ref[...], k, v, sink_ref, None)


def _attn_a_sample(qkv, cache_k, cache_v, sinks):
    streams = cache_k.shape[0]
    t = qkv.shape[0] // streams
    kcol = D_MODEL // A_KV_DIM
    return pl.pallas_call(
        _attn_a_sample_kernel,
        grid=(streams,),
        in_specs=[pl.BlockSpec(memory_space=pltpu.SMEM),
                  pl.BlockSpec((t, D_MODEL), lambda b: (b, 0)),
                  pl.BlockSpec((t, A_KV_DIM), lambda b: (b, kcol)),
                  pl.BlockSpec((t, A_KV_DIM), lambda b: (b, kcol + 1)),
                  pl.BlockSpec((1, WINDOW, A_KV_DIM), lambda b: (b, 0, 0)),
                  pl.BlockSpec((1, WINDOW, A_KV_DIM), lambda b: (b, 0, 0))],
        out_specs=pl.BlockSpec((t, D_MODEL), lambda b: (b, 0)),
        out_shape=jax.ShapeDtypeStruct((qkv.shape[0], D_MODEL), BF16),
        compiler_params=_params(1),
        name="attn_a_sample",
    )(sinks, qkv, qkv, qkv, cache_k, cache_v)


def _proj_b_kernel(x_ref, g_ref, w_ref, q_ref, k_ref, v_ref, kb_ref, vb_ref):
    xn = _rmsnorm(x_ref[...], g_ref[...]).astype(BF16)
    for j in range(D_MODEL // LANES):
        cols = slice(j * LANES, (j + 1) * LANES)
        q = jnp.dot(xn, w_ref[:, cols], preferred_element_type=F32)
        q_ref[:, cols] = (q * SCALE).astype(BF16)
        k = jnp.dot(xn, w_ref[:, D_MODEL + j * LANES:D_MODEL + (j + 1) * LANES], preferred_element_type=F32)
        kb_ref[:, cols] = k.astype(BF16)
        v = jnp.dot(xn, w_ref[:, 2 * D_MODEL + j * LANES:2 * D_MODEL + (j + 1) * LANES],
                    preferred_element_type=F32)
        vb_ref[:, cols] = v.astype(BF16)
        for half in range(LANES // HEAD_DIM):
            h = j * (LANES // HEAD_DIM) + half
            k_ref[:, h, :] = k[:, half * HEAD_DIM:(half + 1) * HEAD_DIM]
            v_ref[:, h, :] = v[:, half * HEAD_DIM:(half + 1) * HEAD_DIM]


def _proj_b(x, g, w_bf):
    n = x.shape[0]
    tm = min(ROW_TILE, n)
    row = lambda i: (i, 0)
    return pl.pallas_call(
        _proj_b_kernel,
        grid=(n // tm,),
        in_specs=[pl.BlockSpec((tm, D_MODEL), row),
                  pl.BlockSpec((1, D_MODEL), lambda i: (0, 0)),
                  pl.BlockSpec((D_MODEL, 3 * D_MODEL), lambda i: (0, 0))],
        out_specs=[pl.BlockSpec((tm, D_MODEL), row),
                   pl.BlockSpec((tm, N_HEADS, HEAD_DIM), lambda i: (i, 0, 0)),
                   pl.BlockSpec((tm, N_HEADS, HEAD_DIM), lambda i: (i, 0, 0)),
                   pl.BlockSpec((tm, D_MODEL), row),
                   pl.BlockSpec((tm, D_MODEL), row)],
        out_shape=[jax.ShapeDtypeStruct((n, D_MODEL), BF16),
                   jax.ShapeDtypeStruct((n, N_HEADS, HEAD_DIM), F32),
                   jax.ShapeDtypeStruct((n, N_HEADS, HEAD_DIM), F32),
                   jax.ShapeDtypeStruct((n, D_MODEL), BF16),
                   jax.ShapeDtypeStruct((n, D_MODEL), BF16)],
        compiler_params=_params(1),
        name="proj_b",
    )(x, g, w_bf)


def _proj_bt_kernel(x_ref, g_ref, wq_ref, wkt_ref, wvt_ref, q_ref, kt_ref, vt_ref, ktb_ref, vtb_ref):
    xn = _rmsnorm(x_ref[...], g_ref[...]).astype(BF16)
    q = jnp.dot(xn, wq_ref[...], preferred_element_type=F32)
    q_ref[...] = (q * SCALE).astype(BF16)
    kt = lax.dot_general(wkt_ref[...], xn, _CONTRACT_LAST, preferred_element_type=F32)
    kt_ref[...] = kt
    ktb_ref[...] = kt.astype(BF16)
    vt = lax.dot_general(wvt_ref[...], xn, _CONTRACT_LAST, preferred_element_type=F32)
    vt_ref[...] = vt
    vtb_ref[...] = vt.astype(BF16)


def _proj_bt(x, g, wq_bf, wkt_bf, wvt_bf, seq):
    n = x.shape[0]
    tm = min(ROW_TILE, seq)
    tiles = seq // tm
    row = lambda i: (i, 0)
    fixed = lambda i: (0, 0)
    col = lambda i: (i // tiles, 0, i % tiles)
    feature_major = lambda dtype: jax.ShapeDtypeStruct((n // seq, D_MODEL, seq), dtype)
    return pl.pallas_call(
        _proj_bt_kernel,
        grid=(n // tm,),
        in_specs=[pl.BlockSpec((tm, D_MODEL), row),
                  pl.BlockSpec((1, D_MODEL), fixed),
                  pl.BlockSpec((D_MODEL, D_MODEL), fixed),
                  pl.BlockSpec((D_MODEL, D_MODEL), fixed),
                  pl.BlockSpec((D_MODEL, D_MODEL), fixed)],
        out_specs=[pl.BlockSpec((tm, D_MODEL), row)] + [pl.BlockSpec((None, D_MODEL, tm), col)] * 4,
        out_shape=[jax.ShapeDtypeStruct((n, D_MODEL), BF16),
                   feature_major(F32), feature_major(F32), feature_major(BF16), feature_major(BF16)],
        compiler_params=_params(1),
        name="proj_bt",
    )(x, g, wq_bf, wkt_bf, wvt_bf)


def _strict_lower_ones(n):
    j = lax.broadcasted_iota(jnp.int32, (2 * n, n), 0) % n
    s = lax.broadcasted_iota(jnp.int32, (2 * n, n), 1)
    return jnp.where(j > s, 1.0, 0.0).astype(BF16)


_CONTRACT_LAST = (((1,), (1,)), ((), ()))


def _sb_scores(qh, kh, transposed):
    if transposed:
        return jnp.dot(qh, kh, preferred_element_type=F32)
    return lax.dot_general(qh, kh, _CONTRACT_LAST, preferred_element_type=F32)


def _sb_suffix(z, tri, before):
    softplus = jnp.maximum(z, 0.0) + jnp.log(1.0 + jnp.exp(-jnp.abs(z)))
    log_beta = z - softplus
    if before is not None:
        softplus = jnp.where(before, softplus, 0.0)
    hi = softplus.astype(BF16)
    lo = (softplus - hi.astype(F32)).astype(BF16)
    later = jnp.dot(jnp.concatenate([hi, lo], axis=1), tri, preferred_element_type=F32)
    return log_beta, later, later[:, 0:1] + softplus[:, 0:1]


def _sb_values(log_beta, later, vh, before, transposed):
    a = jnp.exp(log_beta - later)
    if before is not None:
        a = jnp.where(before, a, 0.0)
    if transposed:
        return lax.dot_general(a.astype(BF16), vh, _CONTRACT_LAST, preferred_element_type=F32)
    return jnp.dot(a.astype(BF16), vh, preferred_element_type=F32)


def _head_cols(ref):
    return lambda h: ref[:, h * HEAD_DIM:(h + 1) * HEAD_DIM].astype(BF16)


def _head_rows(ref):
    return lambda h: ref[h * HEAD_DIM:(h + 1) * HEAD_DIM, :].astype(BF16)


def _sb_block(q_ref, k_head, v_head, tri, carry_ref, acc_ref, before, transposed):
    first_head = lax.broadcasted_iota(jnp.int32, carry_ref.shape[1:], 1) < HEAD_DIM
    q_head = _head_cols(q_ref)
    carries = [carry_ref[pair] for pair in range(N_HEADS // 2)]
    z, mid, outs, masses = {}, {}, {}, {}
    for t in range(N_HEADS + 2 * SB_STAGE_SKEW):
        if t < N_HEADS:
            z[t] = _sb_scores(q_head(t), k_head(t), transposed)
        h = t - SB_STAGE_SKEW
        if 0 <= h < N_HEADS:
            log_beta, later, masses[h] = _sb_suffix(z.pop(h), tri, before)
            mid[h] = (log_beta, later)
        h = t - 2 * SB_STAGE_SKEW
        if 0 <= h < N_HEADS:
            outs[h] = _sb_values(*mid.pop(h), v_head(h), before, transposed)
    adds = []
    for pair in range(N_HEADS // 2):
        h = 2 * pair
        adds.append(jnp.exp(-carries[pair]) * jnp.concatenate([outs[h], outs[h + 1]], axis=1))
        carries[pair] = carries[pair] + jnp.where(first_head, masses[h], masses[h + 1])
    acc_ref[...] += jnp.concatenate(adds, axis=1)
    least = carries[0]
    for pair in range(N_HEADS // 2):
        carry_ref[pair] = carries[pair]
        least = jnp.minimum(least, carries[pair])
    return (jnp.min(least) <= SB_DECAY_LIMIT).astype(jnp.int32)


def _sb_old_blocks(n_old, fetch, block, live):
    @pl.when(n_old > 0)
    def _():
        for copy in fetch(1, 0):
            copy.start()

    def cond(state):
        j, live = state
        return jnp.logical_and(j <= n_old, live == 1)

    def body(state):
        j, _ = state
        slot = (j - 1) % 2
        for copy in fetch(j, slot):
            copy.wait()

        @pl.when(j < n_old)
        def _():
            for copy in fetch(j + 1, 1 - slot):
                copy.start()

        return j + 1, block(slot)

    j_end, _ = lax.while_loop(cond, body, (jnp.int32(1), live))

    @pl.when(j_end <= n_old)
    def _():
        for copy in fetch(j_end, (j_end - 1) % 2):
            copy.wait()


def _sb_prompt_kernel(q_ref, kd_ref, vd_ref, k_hbm, v_hbm, o_ref, kbuf, vbuf, sem, acc_ref, carry_ref,
                      *, blocks_per_seq):
    i = pl.program_id(0)
    n = SB_BLOCK
    tri = _strict_lower_ones(n)
    b = i // blocks_per_seq
    qi = i % blocks_per_seq

    def fetch(j, slot):
        cols = pl.ds((qi - j) * n, n)
        return (pltpu.make_async_copy(k_hbm.at[b, :, cols], kbuf.at[slot], sem.at[0, slot]),
                pltpu.make_async_copy(v_hbm.at[b, :, cols], vbuf.at[slot], sem.at[1, slot]))

    def old_block(slot):
        return _sb_block(q_ref, _head_rows(kbuf.at[slot]), _head_rows(vbuf.at[slot]), tri, carry_ref, acc_ref,
                         None, True)

    acc_ref[...] = jnp.zeros_like(acc_ref)
    carry_ref[...] = jnp.zeros_like(carry_ref)
    row = lax.broadcasted_iota(jnp.int32, (n, n), 0)
    col = lax.broadcasted_iota(jnp.int32, (n, n), 1)
    live = _sb_block(q_ref, _head_rows(kd_ref), _head_rows(vd_ref), tri, carry_ref, acc_ref, col < row, True)
    _sb_old_blocks(qi, fetch, old_block, live)
    o_ref[...] = acc_ref[...].astype(BF16)


def _attn_b_prompt(q_bf, kt_bf, vt_bf):
    n = q_bf.shape[0]
    seq = kt_bf.shape[2]
    blk = SB_BLOCK
    bps = seq // blk
    row = lambda i: (i, 0)
    diag = lambda i: (i // bps, 0, i % bps)
    return pl.pallas_call(
        functools.partial(_sb_prompt_kernel, blocks_per_seq=bps),
        grid=(n // blk,),
        in_specs=[pl.BlockSpec((blk, D_MODEL), row),
                  pl.BlockSpec((None, D_MODEL, blk), diag),
                  pl.BlockSpec((None, D_MODEL, blk), diag),
                  pl.BlockSpec(memory_space=pl.ANY),
                  pl.BlockSpec(memory_space=pl.ANY)],
        out_specs=pl.BlockSpec((blk, D_MODEL), row),
        out_shape=jax.ShapeDtypeStruct((n, D_MODEL), BF16),
        scratch_shapes=[pltpu.VMEM((2, D_MODEL, blk), BF16),
                        pltpu.VMEM((2, D_MODEL, blk), BF16),
                        pltpu.SemaphoreType.DMA((2, 2)),
                        pltpu.VMEM((blk, D_MODEL), F32),
                        pltpu.VMEM((N_HEADS // 2, blk, LANES), F32)],
        compiler_params=_params(1),
        name="attn_b_prompt",
    )(q_bf, kt_bf, vt_bf, kt_bf, vt_bf)


def _sb_sample_kernel(q_ref, kn_ref, vn_ref, ck_hbm, cv_hbm, o_ref, kbuf, vbuf, sem, acc_ref, carry_ref,
                      *, n_cache):
    b = pl.program_id(0)
    t = q_ref.shape[0]
    n = SB_BLOCK
    tri = _strict_lower_ones(n)

    def fetch(j, slot):
        cols = pl.ds((n_cache - j) * n, n)
        return (pltpu.make_async_copy(ck_hbm.at[b, :, cols], kbuf.at[slot], sem.at[0, slot]),
                pltpu.make_async_copy(cv_hbm.at[b, :, cols], vbuf.at[slot], sem.at[1, slot]))

    def old_block(slot):
        return _sb_block(q_ref, _head_rows(kbuf.at[slot]), _head_rows(vbuf.at[slot]), tri, carry_ref, acc_ref,
                         None, True)

    acc_ref[...] = jnp.zeros_like(acc_ref)
    carry_ref[...] = jnp.zeros_like(carry_ref)
    row = lax.broadcasted_iota(jnp.int32, (t, t), 0)
    col = lax.broadcasted_iota(jnp.int32, (t, t), 1)
    live = _sb_block(q_ref, _head_cols(kn_ref), _head_cols(vn_ref), _strict_lower_ones(t), carry_ref, acc_ref,
                     col < row, False)
    _sb_old_blocks(jnp.int32(n_cache), fetch, old_block, live)
    o_ref[...] = acc_ref[...].astype(BF16)


def _attn_b_sample(q_bf, k_bf, v_bf, cache_k, cache_v):
    streams, _, past = cache_k.shape
    t = q_bf.shape[0] // streams
    blk = SB_BLOCK
    new = lambda b: (b, 0)
    return pl.pallas_call(
        functools.partial(_sb_sample_kernel, n_cache=past // blk),
        grid=(streams,),
        in_specs=[pl.BlockSpec((t, D_MODEL), new),
                  pl.BlockSpec((t, D_MODEL), new),
                  pl.BlockSpec((t, D_MODEL), new),
                  pl.BlockSpec(memory_space=pl.ANY),
                  pl.BlockSpec(memory_space=pl.ANY)],
        out_specs=pl.BlockSpec((t, D_MODEL), new),
        out_shape=jax.ShapeDtypeStruct(q_bf.shape, BF16),
        scratch_shapes=[pltpu.VMEM((2, D_MODEL, blk), F32),
                        pltpu.VMEM((2, D_MODEL, blk), F32),
                        pltpu.SemaphoreType.DMA((2, 2)),
                        pltpu.VMEM((t, D_MODEL), F32),
                        pltpu.VMEM((N_HEADS // 2, t, LANES), F32)],
        compiler_params=_params(1),
        name="attn_b_sample",
    )(q_bf, k_bf, v_bf, cache_k, cache_v)


ROUTE_COLS = N_GROUPS + N_EXPERTS


def _oproj_route_kernel(o_ref, wo_ref, h_ref, g_ref, wr_hi_ref, wr_lo_ref, base_ref,
                        h_out_ref, xn_ref, route_ref, count_ref):
    @pl.when(pl.program_id(0) == 0)
    def _():
        count_ref[...] = base_ref[...]

    h = h_ref[...] + jnp.dot(o_ref[...], wo_ref[...], preferred_element_type=F32)
    h_out_ref[...] = h
    xn = _rmsnorm(h, g_ref[...])
    xn_ref[...] = xn
    x_hi = xn.astype(BF16)
    x_lo = (xn - x_hi.astype(F32)).astype(BF16)
    logits = (jnp.dot(x_hi, wr_hi_ref[...], preferred_element_type=F32)
              + jnp.dot(x_hi, wr_lo_ref[...], preferred_element_type=F32)
              + jnp.dot(x_lo, wr_hi_ref[...], preferred_element_type=F32))
    lane = lax.broadcasted_iota(jnp.int32, logits.shape, 1)
    big = jnp.int32(2 * LANES)

    def first_argmax(vals, peak):
        return jnp.min(jnp.where(vals == peak, lane, big), axis=-1, keepdims=True)

    g_mask = lane < N_GROUPS
    g_logits = jnp.where(g_mask, logits, -jnp.inf)
    g_max = jnp.max(g_logits, axis=-1, keepdims=True)
    g_idx = first_argmax(g_logits, g_max)
    g_sum = jnp.sum(jnp.where(g_mask, jnp.exp(logits - g_max), 0.0), axis=-1, keepdims=True)
    g_p = 1.0 / g_sum

    e_lo = N_GROUPS + g_idx * EXPERTS_PER_GROUP
    e_mask = (lane >= e_lo) & (lane < e_lo + EXPERTS_PER_GROUP)
    e_logits = jnp.where(e_mask, logits, -jnp.inf)
    m1 = jnp.max(e_logits, axis=-1, keepdims=True)
    i1 = first_argmax(e_logits, m1)
    rest = jnp.where(lane == i1, -jnp.inf, e_logits)
    m2 = jnp.max(rest, axis=-1, keepdims=True)
    i2 = first_argmax(rest, m2)
    e_sum = jnp.sum(jnp.where(e_mask, jnp.exp(logits - m1), 0.0), axis=-1, keepdims=True)
    p1 = 1.0 / e_sum
    p2 = jnp.exp(m2 - m1) / e_sum
    top_sum = p1 + p2
    w1 = g_p * p1 / top_sum
    w2 = g_p * p2 / top_sum
    e1 = i1 - N_GROUPS
    e2 = i2 - N_GROUPS
    pick1 = lane == e1
    pick2 = lane == e2
    picks = jnp.where(pick1 | pick2, 1.0, 0.0)
    tm = picks.shape[0]
    earlier = jnp.where(lax.broadcasted_iota(jnp.int32, (tm, tm), 1) < lax.broadcasted_iota(jnp.int32, (tm, tm), 0),
                        1.0, 0.0).astype(BF16)
    counts = count_ref[...]
    before = jnp.dot(earlier, picks.astype(BF16), preferred_element_type=F32) + counts
    rank1 = jnp.sum(jnp.where(pick1, before, 0.0), axis=-1, keepdims=True)
    rank2 = jnp.sum(jnp.where(pick2, before, 0.0), axis=-1, keepdims=True)
    count_ref[...] = counts + jnp.sum(picks, axis=0, keepdims=True)
    cols = (e1.astype(F32), e2.astype(F32), w1, w2, rank1, rank2)
    route = jnp.zeros_like(logits)
    for c, val in enumerate(cols):
        route = jnp.where(lane == c, val, route)
    route_ref[...] = route


def _oproj_route(o_bf, wo_bf, h, g, wr_hi, wr_lo, base_counts):
    n = h.shape[0]
    tm = min(ROW_TILE, n)
    row = lambda i: (i, 0)
    fixed = lambda i: (0, 0)
    return pl.pallas_call(
        _oproj_route_kernel,
        grid=(n // tm,),
        in_specs=[pl.BlockSpec((tm, D_MODEL), row),
                  pl.BlockSpec((D_MODEL, D_MODEL), fixed),
                  pl.BlockSpec((tm, D_MODEL), row),
                  pl.BlockSpec((1, D_MODEL), fixed),
                  pl.BlockSpec((D_MODEL, LANES), fixed),
                  pl.BlockSpec((D_MODEL, LANES), fixed),
                  pl.BlockSpec((1, LANES), fixed)],
        out_specs=[pl.BlockSpec((tm, D_MODEL), row),
                   pl.BlockSpec((tm, D_MODEL), row),
                   pl.BlockSpec((tm, LANES), row),
                   pl.BlockSpec((1, LANES), fixed)],
        out_shape=[jax.ShapeDtypeStruct((n, D_MODEL), F32),
                   jax.ShapeDtypeStruct((n, D_MODEL), F32),
                   jax.ShapeDtypeStruct((n, LANES), F32),
                   jax.ShapeDtypeStruct((1, LANES), F32)],
        compiler_params=_params(1),
        name="oproj_route",
    )(o_bf, wo_bf, h, g, wr_hi, wr_lo, base_counts)


def _dispatch_kernel(dest_ref, xn_ref, buf_in_ref, buf_ref, sem):
    del buf_in_ref
    i = pl.program_id(0)
    tm = xn_ref.shape[0]
    base = i * (tm * TOP_K)

    def row_copy(t, slot):
        return pltpu.make_async_copy(xn_ref.at[pl.ds(t, 1)], buf_ref.at[pl.ds(slot, 1)], sem)

    def issue(t, carry):
        for k in range(TOP_K):
            row_copy(t, dest_ref[base + TOP_K * t + k]).start(priority=k % 2)
        return carry

    lax.fori_loop(0, tm, issue, 0, unroll=8)
    for k in range(TOP_K):
        pltpu.make_async_copy(xn_ref, buf_ref.at[pl.ds(0, tm)], sem).wait()


def _dispatch(dest, xn, buf):
    n = xn.shape[0]
    tm = min(ROW_TILE, n)
    grid_spec = pltpu.PrefetchScalarGridSpec(
        num_scalar_prefetch=1,
        grid=(n // tm,),
        in_specs=[pl.BlockSpec((tm, D_MODEL), lambda i, d: (i, 0)),
                  pl.BlockSpec(memory_space=pl.ANY)],
        out_specs=pl.BlockSpec(memory_space=pl.ANY),
        scratch_shapes=[pltpu.SemaphoreType.DMA(())])
    return pl.pallas_call(
        _dispatch_kernel,
        grid_spec=grid_spec,
        out_shape=jax.ShapeDtypeStruct(buf.shape, buf.dtype),
        input_output_aliases={2: 0},
        compiler_params=_params(1, disable_bounds_checks=True, has_side_effects=True),
        name="moe_dispatch",
    )(dest, xn, buf)


def _expert_kernel(be_ref, nused_ref, x_ref, wg_ref, wu_ref, wd_ref, o_ref, wg_bf, wu_bf, wd_bf):
    i = pl.program_id(0)
    used = i < nused_ref[0]

    @pl.when(used)
    def _():
        @pl.when(jnp.logical_or(i == 0, be_ref[i] != be_ref[jnp.maximum(i - 1, 0)]))
        def _():
            wg_bf[...] = wg_ref[0].astype(BF16)
            wu_bf[...] = wu_ref[0].astype(BF16)
            wd_bf[...] = wd_ref[0].astype(BF16)

        x = x_ref[...].astype(BF16)
        a = jnp.dot(x, wg_bf[...], preferred_element_type=F32)
        b = jnp.dot(x, wu_bf[...], preferred_element_type=F32)
        mid = (a * (1.0 / (1.0 + jnp.exp(-a))) * b).astype(BF16)
        o_ref[...] = jnp.dot(mid, wd_bf[...], preferred_element_type=F32)

    @pl.when(jnp.logical_not(used))
    def _():
        o_ref[...] = jnp.zeros_like(o_ref)


def _experts(block_e, n_used, buf, w_gate, w_up, w_down, layer):
    bm = EXPERT_BLOCK
    n_blocks = buf.shape[0] // bm
    blk = lambda i, be, nu: (jnp.minimum(i, nu[0] - 1), 0)
    wsel = lambda i, be, nu: (layer, be[jnp.minimum(i, nu[0] - 1)], 0, 0)
    grid_spec = pltpu.PrefetchScalarGridSpec(
        num_scalar_prefetch=2,
        grid=(n_blocks,),
        in_specs=[pl.BlockSpec((bm, D_MODEL), blk),
                  pl.BlockSpec((None, 1, D_MODEL, D_EXPERT), wsel),
                  pl.BlockSpec((None, 1, D_MODEL, D_EXPERT), wsel),
                  pl.BlockSpec((None, 1, D_EXPERT, D_MODEL), wsel)],
        out_specs=pl.BlockSpec((bm, D_MODEL), lambda i, be, nu: (i, 0)),
        scratch_shapes=[pltpu.VMEM((D_MODEL, D_EXPERT), BF16),
                        pltpu.VMEM((D_MODEL, D_EXPERT), BF16),
                        pltpu.VMEM((D_EXPERT, D_MODEL), BF16)])
    return pl.pallas_call(
        _expert_kernel,
        grid_spec=grid_spec,
        out_shape=jax.ShapeDtypeStruct(buf.shape, F32),
        compiler_params=_params(1),
        name="moe_experts",
    )(block_e, n_used, buf, w_gate, w_up, w_down)


def _combine_kernel(dest_ref, h_ref, route_ref, g_ref, yb_ref, o_ref, rows_ref, sem, *, final_norm):
    i = pl.program_id(0)
    n_steps = pl.num_programs(0)
    tm = h_ref.shape[0]

    def row_copy(step, t, k):
        slot = step % 2
        src = dest_ref[step * (tm * TOP_K) + TOP_K * t + k]
        return pltpu.make_async_copy(yb_ref.at[pl.ds(src, 1)], rows_ref.at[slot, k, pl.ds(t, 1)], sem.at[slot])

    def issue(step):
        def body(t, carry):
            for k in range(TOP_K):
                row_copy(step, t, k).start(priority=k % 2)
            return carry
        lax.fori_loop(0, tm, body, 0, unroll=8)

    @pl.when(i == 0)
    def _():
        issue(i)

    @pl.when(i + 1 < n_steps)
    def _():
        issue(i + 1)

    slot = i % 2
    for k in range(TOP_K):
        pltpu.make_async_copy(yb_ref.at[pl.ds(0, tm)], rows_ref.at[slot, k], sem.at[slot]).wait()

    route = route_ref[...]
    w0 = route[:, 2:3]
    w1 = route[:, 3:4]
    out = h_ref[...] + (rows_ref[slot, 0] * w0 + rows_ref[slot, 1] * w1)
    if final_norm:
        out = _rmsnorm(out, g_ref[...])
    o_ref[...] = out


def _combine(dest, h, route, g, yb, final_norm):
    n = h.shape[0]
    tm = min(COMBINE_TILE, n)
    grid_spec = pltpu.PrefetchScalarGridSpec(
        num_scalar_prefetch=1,
        grid=(n // tm,),
        in_specs=[pl.BlockSpec((tm, D_MODEL), lambda i, d: (i, 0)),
                  pl.BlockSpec((tm, LANES), lambda i, d: (i, 0)),
                  pl.BlockSpec((1, D_MODEL), lambda i, d: (0, 0)),
                  pl.BlockSpec(memory_space=pl.ANY)],
        out_specs=pl.BlockSpec((tm, D_MODEL), lambda i, d: (i, 0)),
        scratch_shapes=[pltpu.VMEM((2, TOP_K, tm, D_MODEL), F32),
                        pltpu.SemaphoreType.DMA((2,))])
    return pl.pallas_call(
        functools.partial(_combine_kernel, final_norm=final_norm),
        grid_spec=grid_spec,
        out_shape=jax.ShapeDtypeStruct(h.shape, F32),
        compiler_params=_params(1, disable_bounds_checks=True),
        name="moe_combine",
    )(dest, h, route, g, yb)


def _route_plan(route_p, route_s, counts):
    route = jnp.concatenate([route_p[:, :3 * TOP_K], route_s[:, :3 * TOP_K]], axis=0)
    flat_e = route[:, :TOP_K].astype(jnp.int32).reshape(-1)
    rank = route[:, 2 * TOP_K:].astype(jnp.int32).reshape(-1)
    n_assign = flat_e.shape[0]
    counts = counts[0, :N_EXPERTS].astype(jnp.int32)
    bm = EXPERT_BLOCK
    padded = (counts + bm - 1) // bm * bm
    pends = jnp.cumsum(padded)
    onehot = flat_e[:, None] == jnp.arange(N_EXPERTS, dtype=jnp.int32)[None, :]
    dest = jnp.sum(jnp.where(onehot, (pends - padded)[None, :], 0), axis=1) + rank
    n_blocks = -(-n_assign // bm) + N_EXPERTS
    first_rows = jnp.arange(n_blocks, dtype=jnp.int32) * bm
    block_e = jnp.minimum(jnp.sum((pends[None, :] <= first_rows[:, None]).astype(jnp.int32), axis=1), N_EXPERTS - 1)
    n_used = (pends[-1] // bm).astype(jnp.int32).reshape(1)
    return dest.astype(jnp.int32), block_e, n_used, n_blocks


def _moe(hp, hs, route_p, route_s, counts, xn_p, xn_s, g_final, w_gate, w_up, w_down, layer, final_norm):
    dest, block_e, n_used, n_blocks = _route_plan(route_p, route_s, counts)
    n_p = hp.shape[0] * TOP_K
    dest_p, dest_s = dest[:n_p], dest[n_p:]
    buf = jnp.zeros((n_blocks * EXPERT_BLOCK, D_MODEL), F32)
    buf = _dispatch(dest_p, xn_p, buf)
    buf = _dispatch(dest_s, xn_s, buf)
    yb = _experts(block_e, n_used, buf, w_gate, w_up, w_down, layer)
    out_p = _combine(dest_p, hp, route_p, g_final, yb, final_norm)
    out_s = _combine(dest_s, hs, route_s, g_final, yb, final_norm)
    return out_p, out_s


def _router_weights(w_group, w_router):
    w_exp = jnp.transpose(w_router, (1, 0, 2)).reshape(D_MODEL, N_EXPERTS)
    w = jnp.concatenate([w_group, w_exp, jnp.zeros((D_MODEL, LANES - ROUTE_COLS), F32)], axis=1)
    hi = w.astype(BF16)
    lo = (w - hi.astype(F32)).astype(BF16)
    return hi, lo


def kernel(x_prompt, x_sample, cache_a_k, cache_a_v, cache_b_k, cache_b_v, norm_mix, norm_ffn, norm_final,
           a_w_qkv, a_w_o, a_sinks, b_w_qkv, b_w_o, moe_w_group, moe_w_router, moe_w_gate, moe_w_up, moe_w_down):
    batch, seq, _ = x_prompt.shape
    streams, t_new, _ = x_sample.shape
    past = cache_b_k.shape[2]
    hp = x_prompt.reshape(batch * seq, D_MODEL)
    hs = x_sample.reshape(streams * t_new, D_MODEL)
    g_final = norm_final.reshape(1, D_MODEL)

    cs_p = _rope_table(jnp.arange(seq, dtype=jnp.int32))
    cs_s = _rope_table(jnp.tile(past + jnp.arange(t_new, dtype=jnp.int32), streams))

    def moe_layer(i, hp, hs, op, os_, w_o, final_norm):
        wo_bf = w_o.astype(BF16)
        g = norm_ffn[i].reshape(1, D_MODEL)
        wr_hi, wr_lo = _router_weights(moe_w_group[i], moe_w_router[i])
        hp, xn_p, route_p, counts = _oproj_route(op, wo_bf, hp, g, wr_hi, wr_lo, jnp.zeros((1, LANES), F32))
        hs, xn_s, route_s, counts = _oproj_route(os_, wo_bf, hs, g, wr_hi, wr_lo, counts)
        return _moe(hp, hs, route_p, route_s, counts, xn_p, xn_s, g_final,
                    moe_w_gate, moe_w_up, moe_w_down, i, final_norm)

    g0 = norm_mix[0].reshape(1, D_MODEL)
    wa_bf = a_w_qkv[0].astype(BF16)
    qkv_p = _proj_a(hp, g0, wa_bf, cs_p)
    qkv_s = _proj_a(hs, g0, wa_bf, cs_s)
    ck = cache_a_k[0].reshape(streams, WINDOW, A_KV_DIM)
    cv = cache_a_v[0].reshape(streams, WINDOW, A_KV_DIM)
    op = _attn_a_prompt(qkv_p, a_sinks[0], seq)
    os_ = _attn_a_sample(qkv_s, ck, cv, a_sinks[0])
    tail_p = qkv_p.reshape(batch, seq, -1)[:, -WINDOW:, D_MODEL:]
    new_a_k_prompt = tail_p[:, :, :A_KV_DIM].reshape(1, batch, WINDOW, A_KV_HEADS, HEAD_DIM)
    new_a_v_prompt = tail_p[:, :, A_KV_DIM:].reshape(1, batch, WINDOW, A_KV_HEADS, HEAD_DIM)
    k_s = qkv_s[:, D_MODEL:D_MODEL + A_KV_DIM].reshape(streams, t_new, A_KV_HEADS, HEAD_DIM)
    v_s = qkv_s[:, D_MODEL + A_KV_DIM:].reshape(streams, t_new, A_KV_HEADS, HEAD_DIM)
    new_a_k_sample = jnp.concatenate([cache_a_k[0], k_s], axis=1)[None, :, -WINDOW:]
    new_a_v_sample = jnp.concatenate([cache_a_v[0], v_s], axis=1)[None, :, -WINDOW:]
    hp, hs = moe_layer(0, hp, hs, op, os_, a_w_o[0], False)

    g1 = norm_mix[1].reshape(1, D_MODEL)
    wb_bf = b_w_qkv[0].astype(BF16)
    wq_bf = wb_bf[:, :D_MODEL]
    wkt_bf = wb_bf[:, D_MODEL:2 * D_MODEL].T
    wvt_bf = wb_bf[:, 2 * D_MODEL:].T
    q_p, kt_p, vt_p, ktb_p, vtb_p = _proj_bt(hp, g1, wq_bf, wkt_bf, wvt_bf, seq)
    q_s, kf_s, vf_s, kb_s, vb_s = _proj_b(hs, g1, wb_bf)
    op = _attn_b_prompt(q_p, ktb_p, vtb_p)

    def feature_major(cache):
        return jnp.transpose(cache, (0, 1, 3, 4, 2)).reshape(streams, D_MODEL, past)

    def time_major(xt):
        return jnp.transpose(xt.reshape(1, batch, N_HEADS, HEAD_DIM, seq), (0, 1, 4, 2, 3))

    os_ = _attn_b_sample(q_s, kb_s, vb_s, feature_major(cache_b_k), feature_major(cache_b_v))
    new_b_k_prompt = time_major(kt_p)
    new_b_v_prompt = time_major(vt_p)
    new_b_k_sample = kf_s.reshape(1, streams, t_new, N_HEADS, HEAD_DIM)
    new_b_v_sample = vf_s.reshape(1, streams, t_new, N_HEADS, HEAD_DIM)
    hp, hs = moe_layer(1, hp, hs, op, os_, b_w_o[0], True)

    y_prompt = hp.reshape(batch, seq, D_MODEL)
    y_sample = hs.reshape(streams, t_new, D_MODEL)
    return (y_prompt, y_sample, new_a_k_prompt, new_a_v_prompt, new_a_k_sample, new_a_v_sample,
            new_b_k_prompt, new_b_v_prompt, new_b_k_sample, new_b_v_sample)
```

```python
import functools

import jax
import jax.numpy as jnp
import numpy as np
from jax import lax
from jax.experimental import pallas as pl
from jax.experimental.pallas import tpu as pltpu

F32 = jnp.float32
BF16 = jnp.bfloat16

D_MODEL = 1024
HEAD_DIM = 64
N_HEADS = D_MODEL // HEAD_DIM
A_KV_HEADS = 4
A_GROUP = N_HEADS // A_KV_HEADS
A_KV_DIM = A_KV_HEADS * HEAD_DIM
CHUNK = 64
WINDOW = 128
ROT_DIM = HEAD_DIM // 4
ROPE_THETA = 500000.0
N_GROUPS = 4
EXPERTS_PER_GROUP = 8
N_EXPERTS = N_GROUPS * EXPERTS_PER_GROUP
TOP_K = 2
D_EXPERT = D_MODEL // 2
RMS_EPS = 1e-6
NEG_INF = -1e30
SCALE = HEAD_DIM ** -0.5

LANES = 128
MXU_COLS = 256
ROW_TILE = 512
ATTN_A_BLOCK = WINDOW
SB_BLOCK = 256
EXPERT_BLOCK = 256
COMBINE_TILE = 256
VMEM_LIMIT = 48 * 1024 * 1024
SB_STAGE_SKEW = 1
SB_DECAY_LIMIT = 105.0


def _params(n_axes, **kw):
    return pltpu.CompilerParams(dimension_semantics=("arbitrary",) * n_axes,
                                vmem_limit_bytes=VMEM_LIMIT, **kw)


def _rmsnorm(x, g):
    return x * lax.rsqrt(jnp.mean(x * x, axis=-1, keepdims=True) + RMS_EPS) * g


def _proj_a_kernel(x_ref, g_ref, w_ref, cs_ref, o_ref):
    xn = _rmsnorm(x_ref[...], g_ref[...]).astype(BF16)
    cos = cs_ref[:, :LANES]
    sin = cs_ref[:, LANES:]
    lane = lax.broadcasted_iota(jnp.int32, cos.shape, 1) % HEAD_DIM
    first_half = lane < ROT_DIM // 2
    n_rot = (D_MODEL + A_KV_DIM) // LANES
    wide = MXU_COLS // LANES
    for c in range(o_ref.shape[1] // MXU_COLS):
        both = jnp.dot(xn, w_ref[:, c * MXU_COLS:(c + 1) * MXU_COLS], preferred_element_type=F32)
        for j in range(c * wide, (c + 1) * wide):
            blk = both[:, (j - c * wide) * LANES:(j - c * wide + 1) * LANES]
            if j < n_rot:
                partner = jnp.where(first_half, pltpu.roll(blk, LANES - ROT_DIM // 2, 1),
                                    pltpu.roll(blk, ROT_DIM // 2, 1))
                blk = blk * cos + partner * sin
            o_ref[:, j * LANES:(j + 1) * LANES] = blk


def _rope_table(pos):
    half = ROT_DIM // 2
    inv = ROPE_THETA ** (-jnp.arange(0, ROT_DIM, 2, dtype=F32) / ROT_DIM)
    lane = jnp.arange(LANES, dtype=jnp.int32) % HEAD_DIM
    ang = pos.astype(F32)[:, None] * inv[lane % half][None, :]
    cos = jnp.where(lane < ROT_DIM, jnp.cos(ang), 1.0)
    sin = jnp.sin(ang)
    sin = jnp.where(lane < half, -sin, jnp.where(lane < ROT_DIM, sin, 0.0))
    return jnp.concatenate([cos, sin], axis=1)


def _proj_a(x, g, w_bf, cs):
    n = x.shape[0]
    tm = min(ROW_TILE, n)
    n_out = w_bf.shape[1]
    cs_blocks = cs.shape[0] // tm
    return pl.pallas_call(
        _proj_a_kernel,
        grid=(n // tm,),
        in_specs=[pl.BlockSpec((tm, D_MODEL), lambda i: (i, 0)),
                  pl.BlockSpec((1, D_MODEL), lambda i: (0, 0)),
                  pl.BlockSpec((D_MODEL, n_out), lambda i: (0, 0)),
                  pl.BlockSpec((tm, 2 * LANES), lambda i: (i % cs_blocks, 0))],
        out_specs=pl.BlockSpec((tm, n_out), lambda i: (i, 0)),
        out_shape=jax.ShapeDtypeStruct((n, n_out), F32),
        compiler_params=_params(1),
        name="proj_a",
    )(x, g, w_bf, cs)


def _sink_attention(q, k_bf, v_bf, sink_ref, valid):
    def scores(h):
        g = h // A_GROUP
        qh = (q[:, h * HEAD_DIM:(h + 1) * HEAD_DIM] * SCALE).astype(BF16)
        kh = k_bf[:, g * HEAD_DIM:(g + 1) * HEAD_DIM]
        return lax.dot_general(qh, kh, (((1,), (1,)), ((), ())), preferred_element_type=F32)

    def attend(h, s):
        g = h // A_GROUP
        vh = v_bf[:, g * HEAD_DIM:(g + 1) * HEAD_DIM]
        if valid is not None:
            s = jnp.where(valid, s, NEG_INF)
        sink = sink_ref[h]
        m = jnp.maximum(jnp.max(s, axis=-1, keepdims=True), sink)
        e = jnp.exp(s - m)
        den = jnp.sum(e, axis=-1, keepdims=True) + jnp.exp(sink - m)
        return jnp.dot(e.astype(BF16), vh, preferred_element_type=F32) / den

    s, outs = {}, []
    for t in range(N_HEADS + 1):
        if t < N_HEADS:
            s[t] = scores(t)
        if t >= 1:
            outs.append(attend(t - 1, s.pop(t - 1)))
    return jnp.concatenate(outs, axis=1).astype(BF16)


def _attn_a_prompt_kernel(sink_ref, q_ref, kp_ref, kc_ref, vp_ref, vc_ref, o_ref, *, blocks_per_seq):
    i = pl.program_id(0)
    has_prev = (i % blocks_per_seq) != 0
    k = jnp.concatenate([kp_ref[...], kc_ref[...]], axis=0).astype(BF16)
    v = jnp.concatenate([vp_ref[...], vc_ref[...]], axis=0).astype(BF16)
    rows, keys = ATTN_A_BLOCK, 2 * ATTN_A_BLOCK
    q_chunk = lax.broadcasted_iota(jnp.int32, (rows, keys), 0) // CHUNK
    col = lax.broadcasted_iota(jnp.int32, (rows, keys), 1)
    k_chunk = col // CHUNK
    valid = (k_chunk >= q_chunk) & (k_chunk <= q_chunk + WINDOW // CHUNK)
    valid = valid & ((col >= ATTN_A_BLOCK) | has_prev)
    o_ref[...] = _sink_attention(q_ref[...], k, v, sink_ref, valid)


def _attn_a_prompt(qkv, sinks, seq):
    n = qkv.shape[0]
    blk = ATTN_A_BLOCK
    blocks_per_seq = seq // blk
    kcol = D_MODEL // A_KV_DIM
    prev = lambda i: jnp.maximum(i - 1, 0)
    return pl.pallas_call(
        functools.partial(_attn_a_prompt_kernel, blocks_per_seq=blocks_per_seq),
        grid=(n // blk,),
        in_specs=[pl.BlockSpec(memory_space=pltpu.SMEM),
                  pl.BlockSpec((blk, D_MODEL), lambda i: (i, 0)),
                  pl.BlockSpec((blk, A_KV_DIM), lambda i: (prev(i), kcol)),
                  pl.BlockSpec((blk, A_KV_DIM), lambda i: (i, kcol)),
                  pl.BlockSpec((blk, A_KV_DIM), lambda i: (prev(i), kcol + 1)),
                  pl.BlockSpec((blk, A_KV_DIM), lambda i: (i, kcol + 1))],
        out_specs=pl.BlockSpec((blk, D_MODEL), lambda i: (i, 0)),
        out_shape=jax.ShapeDtypeStruct((n, D_MODEL), BF16),
        compiler_params=_params(1),
        name="attn_a_prompt",
    )(sinks, qkv, qkv, qkv, qkv, qkv)


def _attn_a_sample_kernel(sink_ref, q_ref, kn_ref, vn_ref, ck_ref, cv_ref, o_ref):
    k = jnp.concatenate([ck_ref[0], kn_ref[...]], axis=0).astype(BF16)
    v = jnp.concatenate([cv_ref[0], vn_ref[...]], axis=0).astype(BF16)
    o_ref[...] = _sink_attention(q_ref[...], k, v, sink_ref, None)


def _attn_a_sample(qkv, cache_k, cache_v, sinks):
    streams = cache_k.shape[0]
    t = qkv.shape[0] // streams
    kcol = D_MODEL // A_KV_DIM
    return pl.pallas_call(
        _attn_a_sample_kernel,
        grid=(streams,),
        in_specs=[pl.BlockSpec(memory_space=pltpu.SMEM),
                  pl.BlockSpec((t, D_MODEL), lambda b: (b, 0)),
                  pl.BlockSpec((t, A_KV_DIM), lambda b: (b, kcol)),
                  pl.BlockSpec((t, A_KV_DIM), lambda b: (b, kcol + 1)),
                  pl.BlockSpec((1, WINDOW, A_KV_DIM), lambda b: (b, 0, 0)),
                  pl.BlockSpec((1, WINDOW, A_KV_DIM), lambda b: (b, 0, 0))],
        out_specs=pl.BlockSpec((t, D_MODEL), lambda b: (b, 0)),
        out_shape=jax.ShapeDtypeStruct((qkv.shape[0], D_MODEL), BF16),
        compiler_params=_params(1),
        name="attn_a_sample",
    )(sinks, qkv, qkv, qkv, cache_k, cache_v)


def _proj_b_kernel(x_ref, g_ref, w_ref, q_ref, k_ref, v_ref, kb_ref, vb_ref):
    xn = _rmsnorm(x_ref[...], g_ref[...]).astype(BF16)
    for j in range(D_MODEL // LANES):
        cols = slice(j * LANES, (j + 1) * LANES)
        q = jnp.dot(xn, w_ref[:, cols], preferred_element_type=F32)
        q_ref[:, cols] = (q * SCALE).astype(BF16)
        k = jnp.dot(xn, w_ref[:, D_MODEL + j * LANES:D_MODEL + (j + 1) * LANES], preferred_element_type=F32)
        kb_ref[:, cols] = k.astype(BF16)
        v = jnp.dot(xn, w_ref[:, 2 * D_MODEL + j * LANES:2 * D_MODEL + (j + 1) * LANES],
                    preferred_element_type=F32)
        vb_ref[:, cols] = v.astype(BF16)
        for half in range(LANES // HEAD_DIM):
            h = j * (LANES // HEAD_DIM) + half
            k_ref[:, h, :] = k[:, half * HEAD_DIM:(half + 1) * HEAD_DIM]
            v_ref[:, h, :] = v[:, half * HEAD_DIM:(half + 1) * HEAD_DIM]


def _proj_b(x, g, w_bf):
    n = x.shape[0]
    tm = min(ROW_TILE, n)
    row = lambda i: (i, 0)
    return pl.pallas_call(
        _proj_b_kernel,
        grid=(n // tm,),
        in_specs=[pl.BlockSpec((tm, D_MODEL), row),
                  pl.BlockSpec((1, D_MODEL), lambda i: (0, 0)),
                  pl.BlockSpec((D_MODEL, 3 * D_MODEL), lambda i: (0, 0))],
        out_specs=[pl.BlockSpec((tm, D_MODEL), row),
                   pl.BlockSpec((tm, N_HEADS, HEAD_DIM), lambda i: (i, 0, 0)),
                   pl.BlockSpec((tm, N_HEADS, HEAD_DIM), lambda i: (i, 0, 0)),
                   pl.BlockSpec((tm, D_MODEL), row),
                   pl.BlockSpec((tm, D_MODEL), row)],
        out_shape=[jax.ShapeDtypeStruct((n, D_MODEL), BF16),
                   jax.ShapeDtypeStruct((n, N_HEADS, HEAD_DIM), F32),
                   jax.ShapeDtypeStruct((n, N_HEADS, HEAD_DIM), F32),
                   jax.ShapeDtypeStruct((n, D_MODEL), BF16),
                   jax.ShapeDtypeStruct((n, D_MODEL), BF16)],
        compiler_params=_params(1),
        name="proj_b",
    )(x, g, w_bf)


def _proj_bt_kernel(x_ref, g_ref, wq_ref, wkt_ref, wvt_ref, q_ref, kt_ref, vt_ref, ktb_ref, vtb_ref):
    xn = _rmsnorm(x_ref[...], g_ref[...]).astype(BF16)
    q = jnp.dot(xn, wq_ref[...], preferred_element_type=F32)
    q_ref[...] = (q * SCALE).astype(BF16)
    kt = lax.dot_general(wkt_ref[...], xn, _CONTRACT_LAST, preferred_element_type=F32)
    kt_ref[...] = kt
    ktb_ref[...] = kt.astype(BF16)
    vt = lax.dot_general(wvt_ref[...], xn, _CONTRACT_LAST, preferred_element_type=F32)
    vt_ref[...] = vt
    vtb_ref[...] = vt.astype(BF16)


def _proj_bt(x, g, wq_bf, wkt_bf, wvt_bf, seq):
    n = x.shape[0]
    tm = min(ROW_TILE, seq)
    tiles = seq // tm
    row = lambda i: (i, 0)
    fixed = lambda i: (0, 0)
    col = lambda i: (i // tiles, 0, i % tiles)
    feature_major = lambda dtype: jax.ShapeDtypeStruct((n // seq, D_MODEL, seq), dtype)
    return pl.pallas_call(
        _proj_bt_kernel,
        grid=(n // tm,),
        in_specs=[pl.BlockSpec((tm, D_MODEL), row),
                  pl.BlockSpec((1, D_MODEL), fixed),
                  pl.BlockSpec((D_MODEL, D_MODEL), fixed),
                  pl.BlockSpec((D_MODEL, D_MODEL), fixed),
                  pl.BlockSpec((D_MODEL, D_MODEL), fixed)],
        out_specs=[pl.BlockSpec((tm, D_MODEL), row)] + [pl.BlockSpec((None, D_MODEL, tm), col)] * 4,
        out_shape=[jax.ShapeDtypeStruct((n, D_MODEL), BF16),
                   feature_major(F32), feature_major(F32), feature_major(BF16), feature_major(BF16)],
        compiler_params=_params(1),
        name="proj_bt",
    )(x, g, wq_bf, wkt_bf, wvt_bf)


def _strict_lower_ones(n):
    j = lax.broadcasted_iota(jnp.int32, (2 * n, n), 0) % n
    s = lax.broadcasted_iota(jnp.int32, (2 * n, n), 1)
    return jnp.where(j > s, 1.0, 0.0).astype(BF16)


_CONTRACT_LAST = (((1,), (1,)), ((), ()))


def _sb_scores(qh, kh, transposed):
    if transposed:
        return jnp.dot(qh, kh, preferred_element_type=F32)
    return lax.dot_general(qh, kh, _CONTRACT_LAST, preferred_element_type=F32)


def _sb_suffix(z, tri, before):
    softplus = jnp.maximum(z, 0.0) + jnp.log(1.0 + jnp.exp(-jnp.abs(z)))
    log_beta = z - softplus
    if before is not None:
        softplus = jnp.where(before, softplus, 0.0)
    hi = softplus.astype(BF16)
    lo = (softplus - hi.astype(F32)).astype(BF16)
    later = jnp.dot(jnp.concatenate([hi, lo], axis=1), tri, preferred_element_type=F32)
    return log_beta, later, later[:, 0:1] + softplus[:, 0:1]


def _sb_values(log_beta, later, vh, before, transposed):
    a = jnp.exp(log_beta - later)
    if before is not None:
        a = jnp.where(before, a, 0.0)
    if transposed:
        return lax.dot_general(a.astype(BF16), vh, _CONTRACT_LAST, preferred_element_type=F32)
    return jnp.dot(a.astype(BF16), vh, preferred_element_type=F32)


def _head_cols(ref):
    return lambda h: ref[:, h * HEAD_DIM:(h + 1) * HEAD_DIM].astype(BF16)


def _head_rows(ref):
    return lambda h: ref[h * HEAD_DIM:(h + 1) * HEAD_DIM, :].astype(BF16)


def _sb_block(q_ref, k_head, v_head, tri, carry_ref, acc_ref, before, transposed):
    first_head = lax.broadcasted_iota(jnp.int32, carry_ref.shape[1:], 1) < HEAD_DIM
    q_head = _head_cols(q_ref)
    carries = [carry_ref[pair] for pair in range(N_HEADS // 2)]
    z, mid, outs, masses = {}, {}, {}, {}
    for t in range(N_HEADS + 2 * SB_STAGE_SKEW):
        if t < N_HEADS:
            z[t] = _sb_scores(q_head(t), k_head(t), transposed)
        h = t - SB_STAGE_SKEW
        if 0 <= h < N_HEADS:
            log_beta, later, masses[h] = _sb_suffix(z.pop(h), tri, before)
            mid[h] = (log_beta, later)
        h = t - 2 * SB_STAGE_SKEW
        if 0 <= h < N_HEADS:
            outs[h] = _sb_values(*mid.pop(h), v_head(h), before, transposed)
    adds = []
    for pair in range(N_HEADS // 2):
        h = 2 * pair
        adds.append(jnp.exp(-carries[pair]) * jnp.concatenate([outs[h], outs[h + 1]], axis=1))
        carries[pair] = carries[pair] + jnp.where(first_head, masses[h], masses[h + 1])
    acc_ref[...] += jnp.concatenate(adds, axis=1)
    least = carries[0]
    for pair in range(N_HEADS // 2):
        carry_ref[pair] = carries[pair]
        least = jnp.minimum(least, carries[pair])
    return (jnp.min(least) <= SB_DECAY_LIMIT).astype(jnp.int32)


def _sb_old_blocks(n_old, fetch, block, live):
    @pl.when(n_old > 0)
    def _():
        for copy in fetch(1, 0):
            copy.start()

    def cond(state):
        j, live = state
        return jnp.logical_and(j <= n_old, live == 1)

    def body(state):
        j, _ = state
        slot = (j - 1) % 2
        for copy in fetch(j, slot):
            copy.wait()

        @pl.when(j < n_old)
        def _():
            for copy in fetch(j + 1, 1 - slot):
                copy.start()

        return j + 1, block(slot)

    j_end, _ = lax.while_loop(cond, body, (jnp.int32(1), live))

    @pl.when(j_end <= n_old)
    def _():
        for copy in fetch(j_end, (j_end - 1) % 2):
            copy.wait()


def _sb_prompt_kernel(q_ref, kd_ref, vd_ref, k_hbm, v_hbm, o_ref, kbuf, vbuf, sem, acc_ref, carry_ref,
                      *, blocks_per_seq):
    i = pl.program_id(0)
    n = SB_BLOCK
    tri = _strict_lower_ones(n)
    b = i // blocks_per_seq
    qi = i % blocks_per_seq

    def fetch(j, slot):
        cols = pl.ds((qi - j) * n, n)
        return (pltpu.make_async_copy(k_hbm.at[b, :, cols], kbuf.at[slot], sem.at[0, slot]),
                pltpu.make_async_copy(v_hbm.at[b, :, cols], vbuf.at[slot], sem.at[1, slot]))

    def old_block(slot):
        return _sb_block(q_ref, _head_rows(kbuf.at[slot]), _head_rows(vbuf.at[slot]), tri, carry_ref, acc_ref,
                         None, True)

    acc_ref[...] = jnp.zeros_like(acc_ref)
    carry_ref[...] = jnp.zeros_like(carry_ref)
    row = lax.broadcasted_iota(jnp.int32, (n, n), 0)
    col = lax.broadcasted_iota(jnp.int32, (n, n), 1)
    live = _sb_block(q_ref, _head_rows(kd_ref), _head_rows(vd_ref), tri, carry_ref, acc_ref, col < row, True)
    _sb_old_blocks(qi, fetch, old_block, live)
    o_ref[...] = acc_ref[...].astype(BF16)


def _attn_b_prompt(q_bf, kt_bf, vt_bf):
    n = q_bf.shape[0]
    seq = kt_bf.shape[2]
    blk = SB_BLOCK
    bps = seq // blk
    row = lambda i: (i, 0)
    diag = lambda i: (i // bps, 0, i % bps)
    return pl.pallas_call(
        functools.partial(_sb_prompt_kernel, blocks_per_seq=bps),
        grid=(n // blk,),
        in_specs=[pl.BlockSpec((blk, D_MODEL), row),
                  pl.BlockSpec((None, D_MODEL, blk), diag),
                  pl.BlockSpec((None, D_MODEL, blk), diag),
                  pl.BlockSpec(memory_space=pl.ANY),
                  pl.BlockSpec(memory_space=pl.ANY)],
        out_specs=pl.BlockSpec((blk, D_MODEL), row),
        out_shape=jax.ShapeDtypeStruct((n, D_MODEL), BF16),
        scratch_shapes=[pltpu.VMEM((2, D_MODEL, blk), BF16),
                        pltpu.VMEM((2, D_MODEL, blk), BF16),
                        pltpu.SemaphoreType.DMA((2, 2)),
                        pltpu.VMEM((blk, D_MODEL), F32),
                        pltpu.VMEM((N_HEADS // 2, blk, LANES), F32)],
        compiler_params=_params(1),
        name="attn_b_prompt",
    )(q_bf, kt_bf, vt_bf, kt_bf, vt_bf)


def _sb_sample_kernel(q_ref, kn_ref, vn_ref, ck_hbm, cv_hbm, o_ref, kbuf, vbuf, sem, acc_ref, carry_ref,
                      *, n_cache):
    b = pl.program_id(0)
    t = q_ref.shape[0]
    n = SB_BLOCK
    tri = _strict_lower_ones(n)

    def fetch(j, slot):
        cols = pl.ds((n_cache - j) * n, n)
        return (pltpu.make_async_copy(ck_hbm.at[b, :, cols], kbuf.at[slot], sem.at[0, slot]),
                pltpu.make_async_copy(cv_hbm.at[b, :, cols], vbuf.at[slot], sem.at[1, slot]))

    def old_block(slot):
        return _sb_block(q_ref, _head_rows(kbuf.at[slot]), _head_rows(vbuf.at[slot]), tri, carry_ref, acc_ref,
                         None, True)

    acc_ref[...] = jnp.zeros_like(acc_ref)
    carry_ref[...] = jnp.zeros_like(carry_ref)
    row = lax.broadcasted_iota(jnp.int32, (t, t), 0)
    col = lax.broadcasted_iota(jnp.int32, (t, t), 1)
    live = _sb_block(q_ref, _head_cols(kn_ref), _head_cols(vn_ref), _strict_lower_ones(t), carry_ref, acc_ref,
                     col < row, False)
    _sb_old_blocks(jnp.int32(n_cache), fetch, old_block, live)
    o_ref[...] = acc_ref[...].astype(BF16)


def _attn_b_sample(q_bf, k_bf, v_bf, cache_k, cache_v):
    streams, _, past = cache_k.shape
    t = q_bf.shape[0] // streams
    blk = SB_BLOCK
    new = lambda b: (b, 0)
    return pl.pallas_call(
        functools.partial(_sb_sample_kernel, n_cache=past // blk),
        grid=(streams,),
        in_specs=[pl.BlockSpec((t, D_MODEL), new),
                  pl.BlockSpec((t, D_MODEL), new),
                  pl.BlockSpec((t, D_MODEL), new),
                  pl.BlockSpec(memory_space=pl.ANY),
                  pl.BlockSpec(memory_space=pl.ANY)],
        out_specs=pl.BlockSpec((t, D_MODEL), new),
        out_shape=jax.ShapeDtypeStruct(q_bf.shape, BF16),
        scratch_shapes=[pltpu.VMEM((2, D_MODEL, blk), F32),
                        pltpu.VMEM((2, D_MODEL, blk), F32),
                        pltpu.SemaphoreType.DMA((2, 2)),
                        pltpu.VMEM((t, D_MODEL), F32),
                        pltpu.VMEM((N_HEADS // 2, t, LANES), F32)],
        compiler_params=_params(1),
        name="attn_b_sample",
    )(q_bf, k_bf, v_bf, cache_k, cache_v)


ROUTE_ROWS = 48


def _oproj_route_kernel(o_ref, wo_ref, h_ref, g_ref, wr_hi_ref, wr_lo_ref, base_ref,
                        h_out_ref, xn_ref, route_ref, count_ref):
    @pl.when(pl.program_id(0) == 0)
    def _():
        count_ref[...] = base_ref[...]

    h = h_ref[...] + jnp.dot(o_ref[...], wo_ref[...], preferred_element_type=F32)
    h_out_ref[...] = h
    xn = _rmsnorm(h, g_ref[...])
    xn_ref[...] = xn
    x_hi = xn.astype(BF16)
    x_lo = (xn - x_hi.astype(F32)).astype(BF16)
    logits = (lax.dot_general(wr_hi_ref[...], x_hi, _CONTRACT_LAST, preferred_element_type=F32)
              + lax.dot_general(wr_lo_ref[...], x_hi, _CONTRACT_LAST, preferred_element_type=F32)
              + lax.dot_general(wr_hi_ref[...], x_lo, _CONTRACT_LAST, preferred_element_type=F32))
    tm = logits.shape[1]
    sub = EXPERTS_PER_GROUP
    row = lax.broadcasted_iota(jnp.int32, (sub, tm), 0)

    def top1(vals):
        peak = jnp.max(vals, axis=0, keepdims=True)
        return peak, jnp.min(jnp.where(vals == peak, row, sub), axis=0, keepdims=True)

    g_mask = row < N_GROUPS
    g_logits = jnp.where(g_mask, logits[:sub], -jnp.inf)
    g_max, g_idx = top1(g_logits)
    g_sum = jnp.sum(jnp.where(g_mask, jnp.exp(g_logits - g_max), 0.0), axis=0, keepdims=True)
    g_p = 1.0 / g_sum

    e_logits = logits[sub:2 * sub]
    for g in range(1, N_GROUPS):
        e_logits = jnp.where(g_idx == g, logits[(g + 1) * sub:(g + 2) * sub], e_logits)
    m1, i1 = top1(e_logits)
    rest = jnp.where(row == i1, -jnp.inf, e_logits)
    m2, i2 = top1(rest)
    e_sum = jnp.sum(jnp.exp(e_logits - m1), axis=0, keepdims=True)
    p1 = 1.0 / e_sum
    p2 = jnp.exp(m2 - m1) / e_sum
    top_sum = p1 + p2
    w1 = g_p * p1 / top_sum
    w2 = g_p * p2 / top_sum
    e1 = g_idx * sub + i1
    e2 = g_idx * sub + i2
    expert = lax.broadcasted_iota(jnp.int32, (N_EXPERTS, tm), 0)
    pick1 = expert == e1
    pick2 = expert == e2
    picks = jnp.where(pick1 | pick2, 1.0, 0.0)
    earlier = jnp.where(lax.broadcasted_iota(jnp.int32, (tm, tm), 0) < lax.broadcasted_iota(jnp.int32, (tm, tm), 1),
                        1.0, 0.0).astype(BF16)
    counts = count_ref[...]
    before = jnp.dot(picks.astype(BF16), earlier, preferred_element_type=F32) + counts[:, 0:1]
    rank1 = jnp.sum(jnp.where(pick1, before, 0.0), axis=0, keepdims=True)
    rank2 = jnp.sum(jnp.where(pick2, before, 0.0), axis=0, keepdims=True)
    count_ref[...] = counts + jnp.sum(picks, axis=1, keepdims=True)
    vals = (e1.astype(F32), e2.astype(F32), w1, w2, rank1, rank2)
    route = jnp.zeros((sub, tm), F32)
    for c, val in enumerate(vals):
        route = jnp.where(row == c, val, route)
    route_ref[...] = route


def _oproj_route(o_bf, wo_bf, h, g, wr_hi, wr_lo, base_counts):
    n = h.shape[0]
    tm = min(ROW_TILE, n)
    row = lambda i: (i, 0)
    fixed = lambda i: (0, 0)
    return pl.pallas_call(
        _oproj_route_kernel,
        grid=(n // tm,),
        in_specs=[pl.BlockSpec((tm, D_MODEL), row),
                  pl.BlockSpec((D_MODEL, D_MODEL), fixed),
                  pl.BlockSpec((tm, D_MODEL), row),
                  pl.BlockSpec((1, D_MODEL), fixed),
                  pl.BlockSpec((ROUTE_ROWS, D_MODEL), fixed),
                  pl.BlockSpec((ROUTE_ROWS, D_MODEL), fixed),
                  pl.BlockSpec((N_EXPERTS, LANES), fixed)],
        out_specs=[pl.BlockSpec((tm, D_MODEL), row),
                   pl.BlockSpec((tm, D_MODEL), row),
                   pl.BlockSpec((EXPERTS_PER_GROUP, tm), lambda i: (0, i)),
                   pl.BlockSpec((N_EXPERTS, LANES), fixed)],
        out_shape=[jax.ShapeDtypeStruct((n, D_MODEL), F32),
                   jax.ShapeDtypeStruct((n, D_MODEL), F32),
                   jax.ShapeDtypeStruct((EXPERTS_PER_GROUP, n), F32),
                   jax.ShapeDtypeStruct((N_EXPERTS, LANES), F32)],
        compiler_params=_params(1),
        name="oproj_route",
    )(o_bf, wo_bf, h, g, wr_hi, wr_lo, base_counts)


def _dispatch_kernel(dest_ref, xn_ref, buf_in_ref, buf_ref, sem):
    del buf_in_ref
    i = pl.program_id(0)
    tm = xn_ref.shape[0]
    base = i * (tm * TOP_K)

    def row_copy(t, slot):
        return pltpu.make_async_copy(xn_ref.at[pl.ds(t, 1)], buf_ref.at[pl.ds(slot, 1)], sem)

    def issue(t, carry):
        for k in range(TOP_K):
            row_copy(t, dest_ref[base + TOP_K * t + k]).start(priority=k % 2)
        return carry

    lax.fori_loop(0, tm, issue, 0, unroll=8)
    for k in range(TOP_K):
        pltpu.make_async_copy(xn_ref, buf_ref.at[pl.ds(0, tm)], sem).wait()


def _dispatch(dest, xn, buf):
    n = xn.shape[0]
    tm = min(ROW_TILE, n)
    grid_spec = pltpu.PrefetchScalarGridSpec(
        num_scalar_prefetch=1,
        grid=(n // tm,),
        in_specs=[pl.BlockSpec((tm, D_MODEL), lambda i, d: (i, 0)),
                  pl.BlockSpec(memory_space=pl.ANY)],
        out_specs=pl.BlockSpec(memory_space=pl.ANY),
        scratch_shapes=[pltpu.SemaphoreType.DMA(())])
    return pl.pallas_call(
        _dispatch_kernel,
        grid_spec=grid_spec,
        out_shape=jax.ShapeDtypeStruct(buf.shape, buf.dtype),
        input_output_aliases={2: 0},
        compiler_params=_params(1, disable_bounds_checks=True, has_side_effects=True),
        name="moe_dispatch",
    )(dest, xn, buf)


def _expert_kernel(be_ref, nused_ref, x_ref, wg_ref, wu_ref, wd_ref, o_ref, wg_bf, wu_bf, wd_bf):
    i = pl.program_id(0)
    used = i < nused_ref[0]

    @pl.when(used)
    def _():
        @pl.when(jnp.logical_or(i == 0, be_ref[i] != be_ref[jnp.maximum(i - 1, 0)]))
        def _():
            wg_bf[...] = wg_ref[0].astype(BF16)
            wu_bf[...] = wu_ref[0].astype(BF16)
            wd_bf[...] = wd_ref[0].astype(BF16)

        x = x_ref[...].astype(BF16)
        a = jnp.dot(x, wg_bf[...], preferred_element_type=F32)
        b = jnp.dot(x, wu_bf[...], preferred_element_type=F32)
        mid = (a * (1.0 / (1.0 + jnp.exp(-a))) * b).astype(BF16)
        o_ref[...] = jnp.dot(mid, wd_bf[...], preferred_element_type=F32)

    @pl.when(jnp.logical_not(used))
    def _():
        o_ref[...] = jnp.zeros_like(o_ref)


def _experts(block_e, n_used, buf, w_gate, w_up, w_down, layer):
    bm = EXPERT_BLOCK
    n_blocks = buf.shape[0] // bm
    blk = lambda i, be, nu: (jnp.minimum(i, nu[0] - 1), 0)
    wsel = lambda i, be, nu: (layer, be[jnp.minimum(i, nu[0] - 1)], 0, 0)
    grid_spec = pltpu.PrefetchScalarGridSpec(
        num_scalar_prefetch=2,
        grid=(n_blocks,),
        in_specs=[pl.BlockSpec((bm, D_MODEL), blk),
                  pl.BlockSpec((None, 1, D_MODEL, D_EXPERT), wsel),
                  pl.BlockSpec((None, 1, D_MODEL, D_EXPERT), wsel),
                  pl.BlockSpec((None, 1, D_EXPERT, D_MODEL), wsel)],
        out_specs=pl.BlockSpec((bm, D_MODEL), lambda i, be, nu: (i, 0)),
        scratch_shapes=[pltpu.VMEM((D_MODEL, D_EXPERT), BF16),
                        pltpu.VMEM((D_MODEL, D_EXPERT), BF16),
                        pltpu.VMEM((D_EXPERT, D_MODEL), BF16)])
    return pl.pallas_call(
        _expert_kernel,
        grid_spec=grid_spec,
        out_shape=jax.ShapeDtypeStruct(buf.shape, F32),
        compiler_params=_params(1),
        name="moe_experts",
    )(block_e, n_used, buf, w_gate, w_up, w_down)


def _combine_kernel(dest_ref, h_ref, gate_ref, g_ref, yb_ref, o_ref, rows_ref, sem, *, final_norm):
    i = pl.program_id(0)
    n_steps = pl.num_programs(0)
    tm = h_ref.shape[0]

    def row_copy(step, t, k):
        slot = step % 2
        src = dest_ref[step * (tm * TOP_K) + TOP_K * t + k]
        return pltpu.make_async_copy(yb_ref.at[pl.ds(src, 1)], rows_ref.at[slot, k, pl.ds(t, 1)], sem.at[slot])

    def issue(step):
        def body(t, carry):
            for k in range(TOP_K):
                row_copy(step, t, k).start(priority=k % 2)
            return carry
        lax.fori_loop(0, tm, body, 0, unroll=8)

    @pl.when(i == 0)
    def _():
        issue(i)

    @pl.when(i + 1 < n_steps)
    def _():
        issue(i + 1)

    slot = i % 2
    for k in range(TOP_K):
        pltpu.make_async_copy(yb_ref.at[pl.ds(0, tm)], rows_ref.at[slot, k], sem.at[slot]).wait()

    gates = gate_ref[...]
    out = h_ref[...] + (rows_ref[slot, 0] * gates[:, 0:1] + rows_ref[slot, 1] * gates[:, 1:2])
    if final_norm:
        out = _rmsnorm(out, g_ref[...])
    o_ref[...] = out


def _combine(dest, h, gates, g, yb, final_norm):
    n = h.shape[0]
    tm = min(COMBINE_TILE, n)
    grid_spec = pltpu.PrefetchScalarGridSpec(
        num_scalar_prefetch=1,
        grid=(n // tm,),
        in_specs=[pl.BlockSpec((tm, D_MODEL), lambda i, d: (i, 0)),
                  pl.BlockSpec((tm, TOP_K), lambda i, d: (i, 0)),
                  pl.BlockSpec((1, D_MODEL), lambda i, d: (0, 0)),
                  pl.BlockSpec(memory_space=pl.ANY)],
        out_specs=pl.BlockSpec((tm, D_MODEL), lambda i, d: (i, 0)),
        scratch_shapes=[pltpu.VMEM((2, TOP_K, tm, D_MODEL), F32),
                        pltpu.SemaphoreType.DMA((2,))])
    return pl.pallas_call(
        functools.partial(_combine_kernel, final_norm=final_norm),
        grid_spec=grid_spec,
        out_shape=jax.ShapeDtypeStruct(h.shape, F32),
        compiler_params=_params(1, disable_bounds_checks=True),
        name="moe_combine",
    )(dest, h, gates, g, yb)


def _route_plan(route_p, route_s, counts):
    route = jnp.concatenate([route_p, route_s], axis=1)
    flat_e = route[:TOP_K].T.astype(jnp.int32).reshape(-1)
    rank = route[2 * TOP_K:3 * TOP_K].T.astype(jnp.int32).reshape(-1)
    n_assign = flat_e.shape[0]
    counts = counts[:, 0].astype(jnp.int32)
    bm = EXPERT_BLOCK
    padded = (counts + bm - 1) // bm * bm
    pends = jnp.cumsum(padded)
    onehot = flat_e[:, None] == jnp.arange(N_EXPERTS, dtype=jnp.int32)[None, :]
    dest = jnp.sum(jnp.where(onehot, (pends - padded)[None, :], 0), axis=1) + rank
    n_blocks = -(-n_assign // bm) + N_EXPERTS
    first_rows = jnp.arange(n_blocks, dtype=jnp.int32) * bm
    block_e = jnp.minimum(jnp.sum((pends[None, :] <= first_rows[:, None]).astype(jnp.int32), axis=1), N_EXPERTS - 1)
    n_used = (pends[-1] // bm).astype(jnp.int32).reshape(1)
    return dest.astype(jnp.int32), block_e, n_used, n_blocks


def _moe(hp, hs, route_p, route_s, counts, xn_p, xn_s, g_final, w_gate, w_up, w_down, layer, final_norm):
    dest, block_e, n_used, n_blocks = _route_plan(route_p, route_s, counts)
    n_p = hp.shape[0] * TOP_K
    dest_p, dest_s = dest[:n_p], dest[n_p:]
    buf = jnp.zeros((n_blocks * EXPERT_BLOCK, D_MODEL), F32)
    buf = _dispatch(dest_p, xn_p, buf)
    buf = _dispatch(dest_s, xn_s, buf)
    yb = _experts(block_e, n_used, buf, w_gate, w_up, w_down, layer)
    out_p = _combine(dest_p, hp, route_p[TOP_K:2 * TOP_K].T, g_final, yb, final_norm)
    out_s = _combine(dest_s, hs, route_s[TOP_K:2 * TOP_K].T, g_final, yb, final_norm)
    return out_p, out_s


def _router_weights(w_group, w_router):
    w_exp = jnp.transpose(w_router, (0, 2, 1)).reshape(N_EXPERTS, D_MODEL)
    zeros = lambda rows: jnp.zeros((rows, D_MODEL), F32)
    w = jnp.concatenate([w_group.T, zeros(EXPERTS_PER_GROUP - N_GROUPS), w_exp,
                         zeros(ROUTE_ROWS - EXPERTS_PER_GROUP - N_EXPERTS)], axis=0)
    hi = w.astype(BF16)
    lo = (w - hi.astype(F32)).astype(BF16)
    return hi, lo


def kernel(x_prompt, x_sample, cache_a_k, cache_a_v, cache_b_k, cache_b_v, norm_mix, norm_ffn, norm_final,
           a_w_qkv, a_w_o, a_sinks, b_w_qkv, b_w_o, moe_w_group, moe_w_router, moe_w_gate, moe_w_up, moe_w_down):
    batch, seq, _ = x_prompt.shape
    streams, t_new, _ = x_sample.shape
    past = cache_b_k.shape[2]
    hp = x_prompt.reshape(batch * seq, D_MODEL)
    hs = x_sample.reshape(streams * t_new, D_MODEL)
    g_final = norm_final.reshape(1, D_MODEL)

    cs_p = _rope_table(jnp.arange(seq, dtype=jnp.int32))
    cs_s = _rope_table(jnp.tile(past + jnp.arange(t_new, dtype=jnp.int32), streams))

    def moe_layer(i, hp, hs, op, os_, w_o, final_norm):
        wo_bf = w_o.astype(BF16)
        g = norm_ffn[i].reshape(1, D_MODEL)
        wr_hi, wr_lo = _router_weights(moe_w_group[i], moe_w_router[i])
        hp, xn_p, route_p, counts = _oproj_route(op, wo_bf, hp, g, wr_hi, wr_lo, jnp.zeros((N_EXPERTS, LANES), F32))
        hs, xn_s, route_s, counts = _oproj_route(os_, wo_bf, hs, g, wr_hi, wr_lo, counts)
        return _moe(hp, hs, route_p, route_s, counts, xn_p, xn_s, g_final,
                    moe_w_gate, moe_w_up, moe_w_down, i, final_norm)

    g0 = norm_mix[0].reshape(1, D_MODEL)
    wa_bf = a_w_qkv[0].astype(BF16)
    qkv_p = _proj_a(hp, g0, wa_bf, cs_p)
    qkv_s = _proj_a(hs, g0, wa_bf, cs_s)
    ck = cache_a_k[0].reshape(streams, WINDOW, A_KV_DIM)
    cv = cache_a_v[0].reshape(streams, WINDOW, A_KV_DIM)
    op = _attn_a_prompt(qkv_p, a_sinks[0], seq)
    os_ = _attn_a_sample(qkv_s, ck, cv, a_sinks[0])
    tail_p = qkv_p.reshape(batch, seq, -1)[:, -WINDOW:, D_MODEL:]
    new_a_k_prompt = tail_p[:, :, :A_KV_DIM].reshape(1, batch, WINDOW, A_KV_HEADS, HEAD_DIM)
    new_a_v_prompt = tail_p[:, :, A_KV_DIM:].reshape(1, batch, WINDOW, A_KV_HEADS, HEAD_DIM)
    k_s = qkv_s[:, D_MODEL:D_MODEL + A_KV_DIM].reshape(streams, t_new, A_KV_HEADS, HEAD_DIM)
    v_s = qkv_s[:, D_MODEL + A_KV_DIM:].reshape(streams, t_new, A_KV_HEADS, HEAD_DIM)
    new_a_k_sample = jnp.concatenate([cache_a_k[0], k_s], axis=1)[None, :, -WINDOW:]
    new_a_v_sample = jnp.concatenate([cache_a_v[0], v_s], axis=1)[None, :, -WINDOW:]
    hp, hs = moe_layer(0, hp, hs, op, os_, a_w_o[0], False)

    g1 = norm_mix[1].reshape(1, D_MODEL)
    wb_bf = b_w_qkv[0].astype(BF16)
    wq_bf = wb_bf[:, :D_MODEL]
    wkt_bf = wb_bf[:, D_MODEL:2 * D_MODEL].T
    wvt_bf = wb_bf[:, 2 * D_MODEL:].T
    q_p, kt_p, vt_p, ktb_p, vtb_p = _proj_bt(hp, g1, wq_bf, wkt_bf, wvt_bf, seq)
    q_s, kf_s, vf_s, kb_s, vb_s = _proj_b(hs, g1, wb_bf)
    op = _attn_b_prompt(q_p, ktb_p, vtb_p)

    def feature_major(cache):
        return jnp.transpose(cache, (0, 1, 3, 4, 2)).reshape(streams, D_MODEL, past)

    def time_major(xt):
        return jnp.transpose(xt.reshape(1, batch, N_HEADS, HEAD_DIM, seq), (0, 1, 4, 2, 3))

    os_ = _attn_b_sample(q_s, kb_s, vb_s, feature_major(cache_b_k), feature_major(cache_b_v))
    new_b_k_prompt = time_major(kt_p)
    new_b_v_prompt = time_major(vt_p)
    new_b_k_sample = kf_s.reshape(1, streams, t_new, N_HEADS, HEAD_DIM)
    new_b_v_sample = vf_s.reshape(1, streams, t_new, N_HEADS, HEAD_DIM)
    hp, hs = moe_layer(1, hp, hs, op, os_, b_w_o[0], True)

    y_prompt = hp.reshape(batch, seq, D_MODEL)
    y_sample = hs.reshape(streams, t_new, D_MODEL)
    return (y_prompt, y_sample, new_a_k_prompt, new_a_v_prompt, new_a_k_sample, new_a_v_sample,
            new_b_k_prompt, new_b_v_prompt, new_b_k_sample, new_b_v_sample)
```

```python
import functools

import jax
import jax.numpy as jnp
import numpy as np
from jax import lax
from jax.experimental import pallas as pl
from jax.experimental.pallas import tpu as pltpu

F32 = jnp.float32
BF16 = jnp.bfloat16

D_MODEL = 1024
HEAD_DIM = 64
N_HEADS = D_MODEL // HEAD_DIM
A_KV_HEADS = 4
A_GROUP = N_HEADS // A_KV_HEADS
A_KV_DIM = A_KV_HEADS * HEAD_DIM
CHUNK = 64
WINDOW = 128
ROT_DIM = HEAD_DIM // 4
ROPE_THETA = 500000.0
N_GROUPS = 4
EXPERTS_PER_GROUP = 8
N_EXPERTS = N_GROUPS * EXPERTS_PER_GROUP
TOP_K = 2
D_EXPERT = D_MODEL // 2
RMS_EPS = 1e-6
NEG_INF = -1e30
SCALE = HEAD_DIM ** -0.5

LANES = 128
MXU_COLS = 256
ROW_TILE = 512
ATTN_A_BLOCK = WINDOW
SB_BLOCK = 256
EXPERT_BLOCK = 512
COMBINE_TILE = 512
VMEM_LIMIT = 48 * 1024 * 1024
SB_STAGE_SKEW = 1
SB_DECAY_LIMIT = 105.0


def _params(n_axes, **kw):
    return pltpu.CompilerParams(dimension_semantics=("arbitrary",) * n_axes,
                                vmem_limit_bytes=VMEM_LIMIT, **kw)


def _rmsnorm(x, g):
    return x * lax.rsqrt(jnp.mean(x * x, axis=-1, keepdims=True) + RMS_EPS) * g


def _proj_a_kernel(x_ref, g_ref, w_ref, cs_ref, o_ref):
    xn = _rmsnorm(x_ref[...], g_ref[...]).astype(BF16)
    cos = cs_ref[:, :LANES]
    sin = cs_ref[:, LANES:]
    lane = lax.broadcasted_iota(jnp.int32, cos.shape, 1) % HEAD_DIM
    first_half = lane < ROT_DIM // 2
    n_rot = (D_MODEL + A_KV_DIM) // LANES
    wide = MXU_COLS // LANES
    for c in range(o_ref.shape[1] // MXU_COLS):
        both = jnp.dot(xn, w_ref[:, c * MXU_COLS:(c + 1) * MXU_COLS], preferred_element_type=F32)
        for j in range(c * wide, (c + 1) * wide):
            blk = both[:, (j - c * wide) * LANES:(j - c * wide + 1) * LANES]
            if j < n_rot:
                partner = jnp.where(first_half, pltpu.roll(blk, LANES - ROT_DIM // 2, 1),
                                    pltpu.roll(blk, ROT_DIM // 2, 1))
                blk = blk * cos + partner * sin
            o_ref[:, j * LANES:(j + 1) * LANES] = blk


def _rope_table(pos):
    half = ROT_DIM // 2
    inv = ROPE_THETA ** (-jnp.arange(0, ROT_DIM, 2, dtype=F32) / ROT_DIM)
    lane = jnp.arange(LANES, dtype=jnp.int32) % HEAD_DIM
    ang = pos.astype(F32)[:, None] * inv[lane % half][None, :]
    cos = jnp.where(lane < ROT_DIM, jnp.cos(ang), 1.0)
    sin = jnp.sin(ang)
    sin = jnp.where(lane < half, -sin, jnp.where(lane < ROT_DIM, sin, 0.0))
    return jnp.concatenate([cos, sin], axis=1)


def _proj_a(x, g, w_bf, cs):
    n = x.shape[0]
    tm = min(ROW_TILE, n)
    n_out = w_bf.shape[1]
    cs_blocks = cs.shape[0] // tm
    return pl.pallas_call(
        _proj_a_kernel,
        grid=(n // tm,),
        in_specs=[pl.BlockSpec((tm, D_MODEL), lambda i: (i, 0)),
                  pl.BlockSpec((1, D_MODEL), lambda i: (0, 0)),
                  pl.BlockSpec((D_MODEL, n_out), lambda i: (0, 0)),
                  pl.BlockSpec((tm, 2 * LANES), lambda i: (i % cs_blocks, 0))],
        out_specs=pl.BlockSpec((tm, n_out), lambda i: (i, 0)),
        out_shape=jax.ShapeDtypeStruct((n, n_out), F32),
        compiler_params=_params(1),
        name="proj_a",
    )(x, g, w_bf, cs)


def _sink_attention(q, k_bf, v_bf, sink_ref, valid):
    def scores(h):
        g = h // A_GROUP
        qh = (q[:, h * HEAD_DIM:(h + 1) * HEAD_DIM] * SCALE).astype(BF16)
        kh = k_bf[:, g * HEAD_DIM:(g + 1) * HEAD_DIM]
        return lax.dot_general(qh, kh, (((1,), (1,)), ((), ())), preferred_element_type=F32)

    def attend(h, s):
        g = h // A_GROUP
        vh = v_bf[:, g * HEAD_DIM:(g + 1) * HEAD_DIM]
        if valid is not None:
            s = jnp.where(valid, s, NEG_INF)
        sink = sink_ref[h]
        m = jnp.maximum(jnp.max(s, axis=-1, keepdims=True), sink)
        e = jnp.exp(s - m)
        den = jnp.sum(e, axis=-1, keepdims=True) + jnp.exp(sink - m)
        return jnp.dot(e.astype(BF16), vh, preferred_element_type=F32) / den

    s, outs = {}, []
    for t in range(N_HEADS + 1):
        if t < N_HEADS:
            s[t] = scores(t)
        if t >= 1:
            outs.append(attend(t - 1, s.pop(t - 1)))
    return jnp.concatenate(outs, axis=1).astype(BF16)


def _attn_a_prompt_kernel(sink_ref, q_ref, kp_ref, kc_ref, vp_ref, vc_ref, o_ref, *, blocks_per_seq):
    i = pl.program_id(0)
    has_prev = (i % blocks_per_seq) != 0
    k = jnp.concatenate([kp_ref[...], kc_ref[...]], axis=0).astype(BF16)
    v = jnp.concatenate([vp_ref[...], vc_ref[...]], axis=0).astype(BF16)
    rows, keys = ATTN_A_BLOCK, 2 * ATTN_A_BLOCK
    q_chunk = lax.broadcasted_iota(jnp.int32, (rows, keys), 0) // CHUNK
    col = lax.broadcasted_iota(jnp.int32, (rows, keys), 1)
    k_chunk = col // CHUNK
    valid = (k_chunk >= q_chunk) & (k_chunk <= q_chunk + WINDOW // CHUNK)
    valid = valid & ((col >= ATTN_A_BLOCK) | has_prev)
    o_ref[...] = _sink_attention(q_ref[...], k, v, sink_ref, valid)


def _attn_a_prompt(qkv, sinks, seq):
    n = qkv.shape[0]
    blk = ATTN_A_BLOCK
    blocks_per_seq = seq // blk
    kcol = D_MODEL // A_KV_DIM
    prev = lambda i: jnp.maximum(i - 1, 0)
    return pl.pallas_call(
        functools.partial(_attn_a_prompt_kernel, blocks_per_seq=blocks_per_seq),
        grid=(n // blk,),
        in_specs=[pl.BlockSpec(memory_space=pltpu.SMEM),
                  pl.BlockSpec((blk, D_MODEL), lambda i: (i, 0)),
                  pl.BlockSpec((blk, A_KV_DIM), lambda i: (prev(i), kcol)),
                  pl.BlockSpec((blk, A_KV_DIM), lambda i: (i, kcol)),
                  pl.BlockSpec((blk, A_KV_DIM), lambda i: (prev(i), kcol + 1)),
                  pl.BlockSpec((blk, A_KV_DIM), lambda i: (i, kcol + 1))],
        out_specs=pl.BlockSpec((blk, D_MODEL), lambda i: (i, 0)),
        out_shape=jax.ShapeDtypeStruct((n, D_MODEL), BF16),
        compiler_params=_params(1),
        name="attn_a_prompt",
    )(sinks, qkv, qkv, qkv, qkv, qkv)


def _attn_a_sample_kernel(sink_ref, q_ref, kn_ref, vn_ref, ck_ref, cv_ref, o_ref):
    k = jnp.concatenate([ck_ref[0], kn_ref[...]], axis=0).astype(BF16)
    v = jnp.concatenate([cv_ref[0], vn_ref[...]], axis=0).astype(BF16)
    o_ref[...] = _sink_attention(q_ref[...], k, v, sink_ref, None)


def _attn_a_sample(qkv, cache_k, cache_v, sinks):
    streams = cache_k.shape[0]
    t = qkv.shape[0] // streams
    kcol = D_MODEL // A_KV_DIM
    return pl.pallas_call(
        _attn_a_sample_kernel,
        grid=(streams,),
        in_specs=[pl.BlockSpec(memory_space=pltpu.SMEM),
                  pl.BlockSpec((t, D_MODEL), lambda b: (b, 0)),
                  pl.BlockSpec((t, A_KV_DIM), lambda b: (b, kcol)),
                  pl.BlockSpec((t, A_KV_DIM), lambda b: (b, kcol + 1)),
                  pl.BlockSpec((1, WINDOW, A_KV_DIM), lambda b: (b, 0, 0)),
                  pl.BlockSpec((1, WINDOW, A_KV_DIM), lambda b: (b, 0, 0))],
        out_specs=pl.BlockSpec((t, D_MODEL), lambda b: (b, 0)),
        out_shape=jax.ShapeDtypeStruct((qkv.shape[0], D_MODEL), BF16),
        compiler_params=_params(1),
        name="attn_a_sample",
    )(sinks, qkv, qkv, qkv, cache_k, cache_v)


def _proj_b_kernel(x_ref, g_ref, w_ref, q_ref, k_ref, v_ref, kb_ref, vb_ref):
    xn = _rmsnorm(x_ref[...], g_ref[...]).astype(BF16)
    for j in range(D_MODEL // LANES):
        cols = slice(j * LANES, (j + 1) * LANES)
        q = jnp.dot(xn, w_ref[:, cols], preferred_element_type=F32)
        q_ref[:, cols] = (q * SCALE).astype(BF16)
        k = jnp.dot(xn, w_ref[:, D_MODEL + j * LANES:D_MODEL + (j + 1) * LANES], preferred_element_type=F32)
        kb_ref[:, cols] = k.astype(BF16)
        v = jnp.dot(xn, w_ref[:, 2 * D_MODEL + j * LANES:2 * D_MODEL + (j + 1) * LANES],
                    preferred_element_type=F32)
        vb_ref[:, cols] = v.astype(BF16)
        for half in range(LANES // HEAD_DIM):
            h = j * (LANES // HEAD_DIM) + half
            k_ref[:, h, :] = k[:, half * HEAD_DIM:(half + 1) * HEAD_DIM]
            v_ref[:, h, :] = v[:, half * HEAD_DIM:(half + 1) * HEAD_DIM]


def _proj_b(x, g, w_bf):
    n = x.shape[0]
    tm = min(ROW_TILE, n)
    row = lambda i: (i, 0)
    return pl.pallas_call(
        _proj_b_kernel,
        grid=(n // tm,),
        in_specs=[pl.BlockSpec((tm, D_MODEL), row),
                  pl.BlockSpec((1, D_MODEL), lambda i: (0, 0)),
                  pl.BlockSpec((D_MODEL, 3 * D_MODEL), lambda i: (0, 0))],
        out_specs=[pl.BlockSpec((tm, D_MODEL), row),
                   pl.BlockSpec((tm, N_HEADS, HEAD_DIM), lambda i: (i, 0, 0)),
                   pl.BlockSpec((tm, N_HEADS, HEAD_DIM), lambda i: (i, 0, 0)),
                   pl.BlockSpec((tm, D_MODEL), row),
                   pl.BlockSpec((tm, D_MODEL), row)],
        out_shape=[jax.ShapeDtypeStruct((n, D_MODEL), BF16),
                   jax.ShapeDtypeStruct((n, N_HEADS, HEAD_DIM), F32),
                   jax.ShapeDtypeStruct((n, N_HEADS, HEAD_DIM), F32),
                   jax.ShapeDtypeStruct((n, D_MODEL), BF16),
                   jax.ShapeDtypeStruct((n, D_MODEL), BF16)],
        compiler_params=_params(1),
        name="proj_b",
    )(x, g, w_bf)


def _proj_bt_kernel(x_ref, g_ref, wq_ref, wkt_ref, wvt_ref, q_ref, kt_ref, vt_ref, ktb_ref, vtb_ref):
    xn = _rmsnorm(x_ref[...], g_ref[...]).astype(BF16)
    q = jnp.dot(xn, wq_ref[...], preferred_element_type=F32)
    q_ref[...] = (q * SCALE).astype(BF16)
    kt = lax.dot_general(wkt_ref[...], xn, _CONTRACT_LAST, preferred_element_type=F32)
    kt_ref[...] = kt
    ktb_ref[...] = kt.astype(BF16)
    vt = lax.dot_general(wvt_ref[...], xn, _CONTRACT_LAST, preferred_element_type=F32)
    vt_ref[...] = vt
    vtb_ref[...] = vt.astype(BF16)


def _proj_bt(x, g, wq_bf, wkt_bf, wvt_bf, seq):
    n = x.shape[0]
    tm = min(ROW_TILE, seq)
    tiles = seq // tm
    row = lambda i: (i, 0)
    fixed = lambda i: (0, 0)
    col = lambda i: (i // tiles, 0, i % tiles)
    feature_major = lambda dtype: jax.ShapeDtypeStruct((n // seq, D_MODEL, seq), dtype)
    return pl.pallas_call(
        _proj_bt_kernel,
        grid=(n // tm,),
        in_specs=[pl.BlockSpec((tm, D_MODEL), row),
                  pl.BlockSpec((1, D_MODEL), fixed),
                  pl.BlockSpec((D_MODEL, D_MODEL), fixed),
                  pl.BlockSpec((D_MODEL, D_MODEL), fixed),
                  pl.BlockSpec((D_MODEL, D_MODEL), fixed)],
        out_specs=[pl.BlockSpec((tm, D_MODEL), row)] + [pl.BlockSpec((None, D_MODEL, tm), col)] * 4,
        out_shape=[jax.ShapeDtypeStruct((n, D_MODEL), BF16),
                   feature_major(F32), feature_major(F32), feature_major(BF16), feature_major(BF16)],
        compiler_params=_params(1),
        name="proj_bt",
    )(x, g, wq_bf, wkt_bf, wvt_bf)


def _strict_lower_ones(n):
    j = lax.broadcasted_iota(jnp.int32, (2 * n, n), 0) % n
    s = lax.broadcasted_iota(jnp.int32, (2 * n, n), 1)
    return jnp.where(j > s, 1.0, 0.0).astype(BF16)


_CONTRACT_LAST = (((1,), (1,)), ((), ()))


def _sb_scores(qh, kh, transposed):
    if transposed:
        return jnp.dot(qh, kh, preferred_element_type=F32)
    return lax.dot_general(qh, kh, _CONTRACT_LAST, preferred_element_type=F32)


def _sb_suffix(z, tri, before):
    softplus = jnp.maximum(z, 0.0) + jnp.log(1.0 + jnp.exp(-jnp.abs(z)))
    log_beta = z - softplus
    if before is not None:
        softplus = jnp.where(before, softplus, 0.0)
    hi = softplus.astype(BF16)
    lo = (softplus - hi.astype(F32)).astype(BF16)
    later = jnp.dot(jnp.concatenate([hi, lo], axis=1), tri, preferred_element_type=F32)
    return log_beta, later, later[:, 0:1] + softplus[:, 0:1]


def _sb_values(log_beta, later, vh, before, transposed):
    a = jnp.exp(log_beta - later)
    if before is not None:
        a = jnp.where(before, a, 0.0)
    if transposed:
        return lax.dot_general(a.astype(BF16), vh, _CONTRACT_LAST, preferred_element_type=F32)
    return jnp.dot(a.astype(BF16), vh, preferred_element_type=F32)


def _head_cols(ref):
    return lambda h: ref[:, h * HEAD_DIM:(h + 1) * HEAD_DIM].astype(BF16)


def _head_rows(ref):
    return lambda h: ref[h * HEAD_DIM:(h + 1) * HEAD_DIM, :].astype(BF16)


def _sb_block(q_ref, k_head, v_head, tri, carry_ref, acc_ref, before, transposed):
    first_head = lax.broadcasted_iota(jnp.int32, carry_ref.shape[1:], 1) < HEAD_DIM
    q_head = _head_cols(q_ref)
    carries = [carry_ref[pair] for pair in range(N_HEADS // 2)]
    z, mid, outs, masses = {}, {}, {}, {}
    for t in range(N_HEADS + 2 * SB_STAGE_SKEW):
        if t < N_HEADS:
            z[t] = _sb_scores(q_head(t), k_head(t), transposed)
        h = t - SB_STAGE_SKEW
        if 0 <= h < N_HEADS:
            log_beta, later, masses[h] = _sb_suffix(z.pop(h), tri, before)
            mid[h] = (log_beta, later)
        h = t - 2 * SB_STAGE_SKEW
        if 0 <= h < N_HEADS:
            outs[h] = _sb_values(*mid.pop(h), v_head(h), before, transposed)
    adds = []
    for pair in range(N_HEADS // 2):
        h = 2 * pair
        adds.append(jnp.exp(-carries[pair]) * jnp.concatenate([outs[h], outs[h + 1]], axis=1))
        carries[pair] = carries[pair] + jnp.where(first_head, masses[h], masses[h + 1])
    acc_ref[...] += jnp.concatenate(adds, axis=1)
    least = carries[0]
    for pair in range(N_HEADS // 2):
        carry_ref[pair] = carries[pair]
        least = jnp.minimum(least, carries[pair])
    return (jnp.min(least) <= SB_DECAY_LIMIT).astype(jnp.int32)


def _sb_old_blocks(n_old, fetch, block, live):
    @pl.when(n_old > 0)
    def _():
        for copy in fetch(1, 0):
            copy.start()

    def cond(state):
        j, live = state
        return jnp.logical_and(j <= n_old, live == 1)

    def body(state):
        j, _ = state
        slot = (j - 1) % 2
        for copy in fetch(j, slot):
            copy.wait()

        @pl.when(j < n_old)
        def _():
            for copy in fetch(j + 1, 1 - slot):
                copy.start()

        return j + 1, block(slot)

    j_end, _ = lax.while_loop(cond, body, (jnp.int32(1), live))

    @pl.when(j_end <= n_old)
    def _():
        for copy in fetch(j_end, (j_end - 1) % 2):
            copy.wait()


def _sb_prompt_kernel(q_ref, kd_ref, vd_ref, k_hbm, v_hbm, o_ref, kbuf, vbuf, sem, acc_ref, carry_ref,
                      *, blocks_per_seq):
    i = pl.program_id(0)
    n = SB_BLOCK
    tri = _strict_lower_ones(n)
    b = i // blocks_per_seq
    qi = i % blocks_per_seq

    def fetch(j, slot):
        cols = pl.ds((qi - j) * n, n)
        return (pltpu.make_async_copy(k_hbm.at[b, :, cols], kbuf.at[slot], sem.at[0, slot]),
                pltpu.make_async_copy(v_hbm.at[b, :, cols], vbuf.at[slot], sem.at[1, slot]))

    def old_block(slot):
        return _sb_block(q_ref, _head_rows(kbuf.at[slot]), _head_rows(vbuf.at[slot]), tri, carry_ref, acc_ref,
                         None, True)

    acc_ref[...] = jnp.zeros_like(acc_ref)
    carry_ref[...] = jnp.zeros_like(carry_ref)
    row = lax.broadcasted_iota(jnp.int32, (n, n), 0)
    col = lax.broadcasted_iota(jnp.int32, (n, n), 1)
    live = _sb_block(q_ref, _head_rows(kd_ref), _head_rows(vd_ref), tri, carry_ref, acc_ref, col < row, True)
    _sb_old_blocks(qi, fetch, old_block, live)
    o_ref[...] = acc_ref[...].astype(BF16)


def _attn_b_prompt(q_bf, kt_bf, vt_bf):
    n = q_bf.shape[0]
    seq = kt_bf.shape[2]
    blk = SB_BLOCK
    bps = seq // blk
    row = lambda i: (i, 0)
    diag = lambda i: (i // bps, 0, i % bps)
    return pl.pallas_call(
        functools.partial(_sb_prompt_kernel, blocks_per_seq=bps),
        grid=(n // blk,),
        in_specs=[pl.BlockSpec((blk, D_MODEL), row),
                  pl.BlockSpec((None, D_MODEL, blk), diag),
                  pl.BlockSpec((None, D_MODEL, blk), diag),
                  pl.BlockSpec(memory_space=pl.ANY),
                  pl.BlockSpec(memory_space=pl.ANY)],
        out_specs=pl.BlockSpec((blk, D_MODEL), row),
        out_shape=jax.ShapeDtypeStruct((n, D_MODEL), BF16),
        scratch_shapes=[pltpu.VMEM((2, D_MODEL, blk), BF16),
                        pltpu.VMEM((2, D_MODEL, blk), BF16),
                        pltpu.SemaphoreType.DMA((2, 2)),
                        pltpu.VMEM((blk, D_MODEL), F32),
                        pltpu.VMEM((N_HEADS // 2, blk, LANES), F32)],
        compiler_params=_params(1),
        name="attn_b_prompt",
    )(q_bf, kt_bf, vt_bf, kt_bf, vt_bf)


def _sb_sample_kernel(q_ref, kn_ref, vn_ref, ck_hbm, cv_hbm, o_ref, kbuf, vbuf, sem, acc_ref, carry_ref,
                      *, n_cache):
    b = pl.program_id(0)
    t = q_ref.shape[0]
    n = SB_BLOCK
    tri = _strict_lower_ones(n)

    def fetch(j, slot):
        cols = pl.ds((n_cache - j) * n, n)
        return (pltpu.make_async_copy(ck_hbm.at[b, :, cols], kbuf.at[slot], sem.at[0, slot]),
                pltpu.make_async_copy(cv_hbm.at[b, :, cols], vbuf.at[slot], sem.at[1, slot]))

    def old_block(slot):
        return _sb_block(q_ref, _head_rows(kbuf.at[slot]), _head_rows(vbuf.at[slot]), tri, carry_ref, acc_ref,
                         None, True)

    acc_ref[...] = jnp.zeros_like(acc_ref)
    carry_ref[...] = jnp.zeros_like(carry_ref)
    row = lax.broadcasted_iota(jnp.int32, (t, t), 0)
    col = lax.broadcasted_iota(jnp.int32, (t, t), 1)
    live = _sb_block(q_ref, _head_cols(kn_ref), _head_cols(vn_ref), _strict_lower_ones(t), carry_ref, acc_ref,
                     col < row, False)
    _sb_old_blocks(jnp.int32(n_cache), fetch, old_block, live)
    o_ref[...] = acc_ref[...].astype(BF16)


def _attn_b_sample(q_bf, k_bf, v_bf, cache_k, cache_v):
    streams, _, past = cache_k.shape
    t = q_bf.shape[0] // streams
    blk = SB_BLOCK
    new = lambda b: (b, 0)
    return pl.pallas_call(
        functools.partial(_sb_sample_kernel, n_cache=past // blk),
        grid=(streams,),
        in_specs=[pl.BlockSpec((t, D_MODEL), new),
                  pl.BlockSpec((t, D_MODEL), new),
                  pl.BlockSpec((t, D_MODEL), new),
                  pl.BlockSpec(memory_space=pl.ANY),
                  pl.BlockSpec(memory_space=pl.ANY)],
        out_specs=pl.BlockSpec((t, D_MODEL), new),
        out_shape=jax.ShapeDtypeStruct(q_bf.shape, BF16),
        scratch_shapes=[pltpu.VMEM((2, D_MODEL, blk), F32),
                        pltpu.VMEM((2, D_MODEL, blk), F32),
                        pltpu.SemaphoreType.DMA((2, 2)),
                        pltpu.VMEM((t, D_MODEL), F32),
                        pltpu.VMEM((N_HEADS // 2, t, LANES), F32)],
        compiler_params=_params(1),
        name="attn_b_sample",
    )(q_bf, k_bf, v_bf, cache_k, cache_v)


ROUTE_ROWS = 48


def _oproj_route_kernel(o_ref, wo_ref, h_ref, g_ref, wr_hi_ref, wr_lo_ref, base_ref,
                        h_out_ref, xn_ref, route_ref, count_ref):
    @pl.when(pl.program_id(0) == 0)
    def _():
        count_ref[...] = base_ref[...]

    h = h_ref[...] + jnp.dot(o_ref[...], wo_ref[...], preferred_element_type=F32)
    h_out_ref[...] = h
    xn = _rmsnorm(h, g_ref[...])
    xn_ref[...] = xn
    x_hi = xn.astype(BF16)
    x_lo = (xn - x_hi.astype(F32)).astype(BF16)
    logits = (lax.dot_general(wr_hi_ref[...], x_hi, _CONTRACT_LAST, preferred_element_type=F32)
              + lax.dot_general(wr_lo_ref[...], x_hi, _CONTRACT_LAST, preferred_element_type=F32)
              + lax.dot_general(wr_hi_ref[...], x_lo, _CONTRACT_LAST, preferred_element_type=F32))
    tm = logits.shape[1]
    sub = EXPERTS_PER_GROUP
    row = lax.broadcasted_iota(jnp.int32, (sub, tm), 0)

    def top1(vals):
        peak = jnp.max(vals, axis=0, keepdims=True)
        return peak, jnp.min(jnp.where(vals == peak, row, sub), axis=0, keepdims=True)

    g_mask = row < N_GROUPS
    g_logits = jnp.where(g_mask, logits[:sub], -jnp.inf)
    g_max, g_idx = top1(g_logits)
    g_sum = jnp.sum(jnp.where(g_mask, jnp.exp(g_logits - g_max), 0.0), axis=0, keepdims=True)
    g_p = 1.0 / g_sum

    e_logits = logits[sub:2 * sub]
    for g in range(1, N_GROUPS):
        e_logits = jnp.where(g_idx == g, logits[(g + 1) * sub:(g + 2) * sub], e_logits)
    m1, i1 = top1(e_logits)
    rest = jnp.where(row == i1, -jnp.inf, e_logits)
    m2, i2 = top1(rest)
    e_sum = jnp.sum(jnp.exp(e_logits - m1), axis=0, keepdims=True)
    p1 = 1.0 / e_sum
    p2 = jnp.exp(m2 - m1) / e_sum
    top_sum = p1 + p2
    w1 = g_p * p1 / top_sum
    w2 = g_p * p2 / top_sum
    e1 = g_idx * sub + i1
    e2 = g_idx * sub + i2
    expert = lax.broadcasted_iota(jnp.int32, (N_EXPERTS, tm), 0)
    pick1 = expert == e1
    pick2 = expert == e2
    picks = jnp.where(pick1 | pick2, 1.0, 0.0)
    earlier = jnp.where(lax.broadcasted_iota(jnp.int32, (tm, tm), 0) < lax.broadcasted_iota(jnp.int32, (tm, tm), 1),
                        1.0, 0.0).astype(BF16)
    counts = count_ref[...]
    before = jnp.dot(picks.astype(BF16), earlier, preferred_element_type=F32) + counts[:, 0:1]
    rank1 = jnp.sum(jnp.where(pick1, before, 0.0), axis=0, keepdims=True)
    rank2 = jnp.sum(jnp.where(pick2, before, 0.0), axis=0, keepdims=True)
    count_ref[...] = counts + jnp.sum(picks, axis=1, keepdims=True)
    vals = (e1.astype(F32), e2.astype(F32), w1, w2, rank1, rank2)
    route = jnp.zeros((sub, tm), F32)
    for c, val in enumerate(vals):
        route = jnp.where(row == c, val, route)
    route_ref[...] = route


def _oproj_route(o_bf, wo_bf, h, g, wr_hi, wr_lo, base_counts):
    n = h.shape[0]
    tm = min(ROW_TILE, n)
    row = lambda i: (i, 0)
    fixed = lambda i: (0, 0)
    return pl.pallas_call(
        _oproj_route_kernel,
        grid=(n // tm,),
        in_specs=[pl.BlockSpec((tm, D_MODEL), row),
                  pl.BlockSpec((D_MODEL, D_MODEL), fixed),
                  pl.BlockSpec((tm, D_MODEL), row),
                  pl.BlockSpec((1, D_MODEL), fixed),
                  pl.BlockSpec((ROUTE_ROWS, D_MODEL), fixed),
                  pl.BlockSpec((ROUTE_ROWS, D_MODEL), fixed),
                  pl.BlockSpec((N_EXPERTS, LANES), fixed)],
        out_specs=[pl.BlockSpec((tm, D_MODEL), row),
                   pl.BlockSpec((tm, D_MODEL), row),
                   pl.BlockSpec((EXPERTS_PER_GROUP, tm), lambda i: (0, i)),
                   pl.BlockSpec((N_EXPERTS, LANES), fixed)],
        out_shape=[jax.ShapeDtypeStruct((n, D_MODEL), F32),
                   jax.ShapeDtypeStruct((n, D_MODEL), F32),
                   jax.ShapeDtypeStruct((EXPERTS_PER_GROUP, n), F32),
                   jax.ShapeDtypeStruct((N_EXPERTS, LANES), F32)],
        compiler_params=_params(1),
        name="oproj_route",
    )(o_bf, wo_bf, h, g, wr_hi, wr_lo, base_counts)


def _dispatch_kernel(dest_ref, xn_ref, buf_in_ref, buf_ref, sem):
    del buf_in_ref
    i = pl.program_id(0)
    tm = xn_ref.shape[0]
    base = i * (tm * TOP_K)

    def row_copy(t, slot):
        return pltpu.make_async_copy(xn_ref.at[pl.ds(t, 1)], buf_ref.at[pl.ds(slot, 1)], sem)

    def issue(t, carry):
        for k in range(TOP_K):
            row_copy(t, dest_ref[base + TOP_K * t + k]).start(priority=k % 2)
        return carry

    lax.fori_loop(0, tm, issue, 0, unroll=8)
    for k in range(TOP_K):
        pltpu.make_async_copy(xn_ref, buf_ref.at[pl.ds(0, tm)], sem).wait()


def _dispatch(dest, xn, buf):
    n = xn.shape[0]
    tm = min(ROW_TILE, n)
    grid_spec = pltpu.PrefetchScalarGridSpec(
        num_scalar_prefetch=1,
        grid=(n // tm,),
        in_specs=[pl.BlockSpec((tm, D_MODEL), lambda i, d: (i, 0)),
                  pl.BlockSpec(memory_space=pl.ANY)],
        out_specs=pl.BlockSpec(memory_space=pl.ANY),
        scratch_shapes=[pltpu.SemaphoreType.DMA(())])
    return pl.pallas_call(
        _dispatch_kernel,
        grid_spec=grid_spec,
        out_shape=jax.ShapeDtypeStruct(buf.shape, buf.dtype),
        input_output_aliases={2: 0},
        compiler_params=_params(1, disable_bounds_checks=True, has_side_effects=True),
        name="moe_dispatch",
    )(dest, xn, buf)


def _expert_kernel(be_ref, nused_ref, x_ref, wg_ref, wu_ref, wd_ref, o_ref, wg_bf, wu_bf, wd_bf):
    i = pl.program_id(0)
    used = i < nused_ref[0]

    @pl.when(used)
    def _():
        @pl.when(jnp.logical_or(i == 0, be_ref[i] != be_ref[jnp.maximum(i - 1, 0)]))
        def _():
            wg_bf[...] = wg_ref[0].astype(BF16)
            wu_bf[...] = wu_ref[0].astype(BF16)
            wd_bf[...] = wd_ref[0].astype(BF16)

        x = x_ref[...].astype(BF16)
        a = jnp.dot(x, wg_bf[...], preferred_element_type=F32)
        b = jnp.dot(x, wu_bf[...], preferred_element_type=F32)
        mid = (a * (1.0 / (1.0 + jnp.exp(-a))) * b).astype(BF16)
        o_ref[...] = jnp.dot(mid, wd_bf[...], preferred_element_type=F32)

    @pl.when(jnp.logical_not(used))
    def _():
        o_ref[...] = jnp.zeros_like(o_ref)


def _experts(block_e, n_used, buf, w_gate, w_up, w_down, layer):
    bm = EXPERT_BLOCK
    n_blocks = buf.shape[0] // bm
    blk = lambda i, be, nu: (jnp.minimum(i, nu[0] - 1), 0)
    wsel = lambda i, be, nu: (layer, be[jnp.minimum(i, nu[0] - 1)], 0, 0)
    grid_spec = pltpu.PrefetchScalarGridSpec(
        num_scalar_prefetch=2,
        grid=(n_blocks,),
        in_specs=[pl.BlockSpec((bm, D_MODEL), blk),
                  pl.BlockSpec((None, 1, D_MODEL, D_EXPERT), wsel),
                  pl.BlockSpec((None, 1, D_MODEL, D_EXPERT), wsel),
                  pl.BlockSpec((None, 1, D_EXPERT, D_MODEL), wsel)],
        out_specs=pl.BlockSpec((bm, D_MODEL), lambda i, be, nu: (i, 0)),
        scratch_shapes=[pltpu.VMEM((D_MODEL, D_EXPERT), BF16),
                        pltpu.VMEM((D_MODEL, D_EXPERT), BF16),
                        pltpu.VMEM((D_EXPERT, D_MODEL), BF16)])
    return pl.pallas_call(
        _expert_kernel,
        grid_spec=grid_spec,
        out_shape=jax.ShapeDtypeStruct(buf.shape, F32),
        compiler_params=_params(1),
        name="moe_experts",
    )(block_e, n_used, buf, w_gate, w_up, w_down)


def _combine_kernel(dest_ref, h_ref, gate_ref, g_ref, yb_ref, o_ref, rows_ref, sem, *, final_norm):
    i = pl.program_id(0)
    n_steps = pl.num_programs(0)
    tm = h_ref.shape[0]

    def row_copy(step, t, k):
        slot = step % 2
        src = dest_ref[step * (tm * TOP_K) + TOP_K * t + k]
        return pltpu.make_async_copy(yb_ref.at[pl.ds(src, 1)], rows_ref.at[slot, k, pl.ds(t, 1)], sem.at[slot])

    def issue(step):
        def body(t, carry):
            for k in range(TOP_K):
                row_copy(step, t, k).start(priority=k % 2)
            return carry
        lax.fori_loop(0, tm, body, 0, unroll=8)

    @pl.when(i == 0)
    def _():
        issue(i)

    @pl.when(i + 1 < n_steps)
    def _():
        issue(i + 1)

    slot = i % 2
    for k in range(TOP_K):
        pltpu.make_async_copy(yb_ref.at[pl.ds(0, tm)], rows_ref.at[slot, k], sem.at[slot]).wait()

    gates = gate_ref[...]
    out = h_ref[...] + (rows_ref[slot, 0] * gates[:, 0:1] + rows_ref[slot, 1] * gates[:, 1:2])
    if final_norm:
        out = _rmsnorm(out, g_ref[...])
    o_ref[...] = out


def _combine(dest, h, gates, g, yb, final_norm):
    n = h.shape[0]
    tm = min(COMBINE_TILE, n)
    grid_spec = pltpu.PrefetchScalarGridSpec(
        num_scalar_prefetch=1,
        grid=(n // tm,),
        in_specs=[pl.BlockSpec((tm, D_MODEL), lambda i, d: (i, 0)),
                  pl.BlockSpec((tm, TOP_K), lambda i, d: (i, 0)),
                  pl.BlockSpec((1, D_MODEL), lambda i, d: (0, 0)),
                  pl.BlockSpec(memory_space=pl.ANY)],
        out_specs=pl.BlockSpec((tm, D_MODEL), lambda i, d: (i, 0)),
        scratch_shapes=[pltpu.VMEM((2, TOP_K, tm, D_MODEL), F32),
                        pltpu.SemaphoreType.DMA((2,))])
    return pl.pallas_call(
        functools.partial(_combine_kernel, final_norm=final_norm),
        grid_spec=grid_spec,
        out_shape=jax.ShapeDtypeStruct(h.shape, F32),
        compiler_params=_params(1, disable_bounds_checks=True),
        name="moe_combine",
    )(dest, h, gates, g, yb)


def _route_plan(route_p, route_s, counts):
    route = jnp.concatenate([route_p, route_s], axis=1)
    flat_e = route[:TOP_K].T.astype(jnp.int32).reshape(-1)
    rank = route[2 * TOP_K:3 * TOP_K].T.astype(jnp.int32).reshape(-1)
    n_assign = flat_e.shape[0]
    counts = counts[:, 0].astype(jnp.int32)
    bm = EXPERT_BLOCK
    padded = (counts + bm - 1) // bm * bm
    pends = jnp.cumsum(padded)
    onehot = flat_e[:, None] == jnp.arange(N_EXPERTS, dtype=jnp.int32)[None, :]
    dest = jnp.sum(jnp.where(onehot, (pends - padded)[None, :], 0), axis=1) + rank
    n_blocks = -(-n_assign // bm) + N_EXPERTS
    first_rows = jnp.arange(n_blocks, dtype=jnp.int32) * bm
    block_e = jnp.minimum(jnp.sum((pends[None, :] <= first_rows[:, None]).astype(jnp.int32), axis=1), N_EXPERTS - 1)
    n_used = (pends[-1] // bm).astype(jnp.int32).reshape(1)
    return dest.astype(jnp.int32), block_e, n_used, n_blocks


def _moe(hp, hs, route_p, route_s, counts, xn_p, xn_s, g_final, w_gate, w_up, w_down, layer, final_norm):
    dest, block_e, n_used, n_blocks = _route_plan(route_p, route_s, counts)
    n_p = hp.shape[0] * TOP_K
    dest_p, dest_s = dest[:n_p], dest[n_p:]
    buf = jnp.zeros((n_blocks * EXPERT_BLOCK, D_MODEL), F32)
    buf = _dispatch(dest_p, xn_p, buf)
    buf = _dispatch(dest_s, xn_s, buf)
    yb = _experts(block_e, n_used, buf, w_gate, w_up, w_down, layer)
    out_p = _combine(dest_p, hp, route_p[TOP_K:2 * TOP_K].T, g_final, yb, final_norm)
    out_s = _combine(dest_s, hs, route_s[TOP_K:2 * TOP_K].T, g_final, yb, final_norm)
    return out_p, out_s


def _router_weights(w_group, w_router):
    w_exp = jnp.transpose(w_router, (0, 2, 1)).reshape(N_EXPERTS, D_MODEL)
    zeros = lambda rows: jnp.zeros((rows, D_MODEL), F32)
    w = jnp.concatenate([w_group.T, zeros(EXPERTS_PER_GROUP - N_GROUPS), w_exp,
                         zeros(ROUTE_ROWS - EXPERTS_PER_GROUP - N_EXPERTS)], axis=0)
    hi = w.astype(BF16)
    lo = (w - hi.astype(F32)).astype(BF16)
    return hi, lo


def kernel(x_prompt, x_sample, cache_a_k, cache_a_v, cache_b_k, cache_b_v, norm_mix, norm_ffn, norm_final,
           a_w_qkv, a_w_o, a_sinks, b_w_qkv, b_w_o, moe_w_group, moe_w_router, moe_w_gate, moe_w_up, moe_w_down):
    batch, seq, _ = x_prompt.shape
    streams, t_new, _ = x_sample.shape
    past = cache_b_k.shape[2]
    hp = x_prompt.reshape(batch * seq, D_MODEL)
    hs = x_sample.reshape(streams * t_new, D_MODEL)
    g_final = norm_final.reshape(1, D_MODEL)

    cs_p = _rope_table(jnp.arange(seq, dtype=jnp.int32))
    cs_s = _rope_table(jnp.tile(past + jnp.arange(t_new, dtype=jnp.int32), streams))

    def moe_layer(i, hp, hs, op, os_, w_o, final_norm):
        wo_bf = w_o.astype(BF16)
        g = norm_ffn[i].reshape(1, D_MODEL)
        wr_hi, wr_lo = _router_weights(moe_w_group[i], moe_w_router[i])
        hp, xn_p, route_p, counts = _oproj_route(op, wo_bf, hp, g, wr_hi, wr_lo, jnp.zeros((N_EXPERTS, LANES), F32))
        hs, xn_s, route_s, counts = _oproj_route(os_, wo_bf, hs, g, wr_hi, wr_lo, counts)
        return _moe(hp, hs, route_p, route_s, counts, xn_p, xn_s, g_final,
                    moe_w_gate, moe_w_up, moe_w_down, i, final_norm)

    g0 = norm_mix[0].reshape(1, D_MODEL)
    wa_bf = a_w_qkv[0].astype(BF16)
    qkv_p = _proj_a(hp, g0, wa_bf, cs_p)
    qkv_s = _proj_a(hs, g0, wa_bf, cs_s)
    ck = cache_a_k[0].reshape(streams, WINDOW, A_KV_DIM)
    cv = cache_a_v[0].reshape(streams, WINDOW, A_KV_DIM)
    op = _attn_a_prompt(qkv_p, a_sinks[0], seq)
    os_ = _attn_a_sample(qkv_s, ck, cv, a_sinks[0])
    tail_p = qkv_p.reshape(batch, seq, -1)[:, -WINDOW:, D_MODEL:]
    new_a_k_prompt = tail_p[:, :, :A_KV_DIM].reshape(1, batch, WINDOW, A_KV_HEADS, HEAD_DIM)
    new_a_v_prompt = tail_p[:, :, A_KV_DIM:].reshape(1, batch, WINDOW, A_KV_HEADS, HEAD_DIM)
    k_s = qkv_s[:, D_MODEL:D_MODEL + A_KV_DIM].reshape(streams, t_new, A_KV_HEADS, HEAD_DIM)
    v_s = qkv_s[:, D_MODEL + A_KV_DIM:].reshape(streams, t_new, A_KV_HEADS, HEAD_DIM)
    new_a_k_sample = jnp.concatenate([cache_a_k[0], k_s], axis=1)[None, :, -WINDOW:]
    new_a_v_sample = jnp.concatenate([cache_a_v[0], v_s], axis=1)[None, :, -WINDOW:]
    hp, hs = moe_layer(0, hp, hs, op, os_, a_w_o[0], False)

    g1 = norm_mix[1].reshape(1, D_MODEL)
    wb_bf = b_w_qkv[0].astype(BF16)
    wq_bf = wb_bf[:, :D_MODEL]
    wkt_bf = wb_bf[:, D_MODEL:2 * D_MODEL].T
    wvt_bf = wb_bf[:, 2 * D_MODEL:].T
    q_p, kt_p, vt_p, ktb_p, vtb_p = _proj_bt(hp, g1, wq_bf, wkt_bf, wvt_bf, seq)
    q_s, kf_s, vf_s, kb_s, vb_s = _proj_b(hs, g1, wb_bf)
    op = _attn_b_prompt(q_p, ktb_p, vtb_p)

    def feature_major(cache):
        return jnp.transpose(cache, (0, 1, 3, 4, 2)).reshape(streams, D_MODEL, past)

    def time_major(xt):
        return jnp.transpose(xt.reshape(1, batch, N_HEADS, HEAD_DIM, seq), (0, 1, 4, 2, 3))

    os_ = _attn_b_sample(q_s, kb_s, vb_s, feature_major(cache_b_k), feature_major(cache_b_v))
    new_b_k_prompt = time_major(kt_p)
    new_b_v_prompt = time_major(vt_p)
    new_b_k_sample = kf_s.reshape(1, streams, t_new, N_HEADS, HEAD_DIM)
    new_b_v_sample = vf_s.reshape(1, streams, t_new, N_HEADS, HEAD_DIM)
    hp, hs = moe_layer(1, hp, hs, op, os_, b_w_o[0], True)

    y_prompt = hp.reshape(batch, seq, D_MODEL)
    y_sample = hs.reshape(streams, t_new, D_MODEL)
    return (y_prompt, y_sample, new_a_k_prompt, new_a_v_prompt, new_a_k_sample, new_a_v_sample,
            new_b_k_prompt, new_b_v_prompt, new_b_k_sample, new_b_v_sample)
```

```python
import functools

import jax
import jax.numpy as jnp
import numpy as np
from jax import lax
from jax.experimental import pallas as pl
from jax.experimental.pallas import tpu as pltpu

F32 = jnp.float32
BF16 = jnp.bfloat16

D_MODEL = 1024
HEAD_DIM = 64
N_HEADS = D_MODEL // HEAD_DIM
A_KV_HEADS = 4
A_GROUP = N_HEADS // A_KV_HEADS
A_KV_DIM = A_KV_HEADS * HEAD_DIM
CHUNK = 64
WINDOW = 128
ROT_DIM = HEAD_DIM // 4
ROPE_THETA = 500000.0
N_GROUPS = 4
EXPERTS_PER_GROUP = 8
N_EXPERTS = N_GROUPS * EXPERTS_PER_GROUP
TOP_K = 2
D_EXPERT = D_MODEL // 2
RMS_EPS = 1e-6
NEG_INF = -1e30
SCALE = HEAD_DIM ** -0.5

LANES = 128
MXU_COLS = 256
ROW_TILE = 512
ATTN_A_BLOCK = WINDOW
SB_BLOCK = 256
EXPERT_BLOCK = 512
COMBINE_TILE = 512
VMEM_LIMIT = 48 * 1024 * 1024
SB_STAGE_SKEW = 1
SB_DECAY_LIMIT = 105.0


def _params(n_axes, **kw):
    return pltpu.CompilerParams(dimension_semantics=("arbitrary",) * n_axes,
                                vmem_limit_bytes=VMEM_LIMIT, **kw)


def _rmsnorm(x, g):
    return x * lax.rsqrt(jnp.mean(x * x, axis=-1, keepdims=True) + RMS_EPS) * g


def _proj_a_kernel(x_ref, g_ref, w_ref, cs_ref, o_ref):
    xn = _rmsnorm(x_ref[...], g_ref[...]).astype(BF16)
    cos = cs_ref[:, :LANES]
    sin = cs_ref[:, LANES:]
    lane = lax.broadcasted_iota(jnp.int32, cos.shape, 1) % HEAD_DIM
    first_half = lane < ROT_DIM // 2
    n_rot = (D_MODEL + A_KV_DIM) // LANES
    wide = MXU_COLS // LANES
    for c in range(o_ref.shape[1] // MXU_COLS):
        both = jnp.dot(xn, w_ref[:, c * MXU_COLS:(c + 1) * MXU_COLS], preferred_element_type=F32)
        for j in range(c * wide, (c + 1) * wide):
            blk = both[:, (j - c * wide) * LANES:(j - c * wide + 1) * LANES]
            if j < n_rot:
                partner = jnp.where(first_half, pltpu.roll(blk, LANES - ROT_DIM // 2, 1),
                                    pltpu.roll(blk, ROT_DIM // 2, 1))
                blk = blk * cos + partner * sin
            o_ref[:, j * LANES:(j + 1) * LANES] = blk


def _rope_table(pos):
    half = ROT_DIM // 2
    inv = ROPE_THETA ** (-jnp.arange(0, ROT_DIM, 2, dtype=F32) / ROT_DIM)
    lane = jnp.arange(LANES, dtype=jnp.int32) % HEAD_DIM
    ang = pos.astype(F32)[:, None] * inv[lane % half][None, :]
    cos = jnp.where(lane < ROT_DIM, jnp.cos(ang), 1.0)
    sin = jnp.sin(ang)
    sin = jnp.where(lane < half, -sin, jnp.where(lane < ROT_DIM, sin, 0.0))
    return jnp.concatenate([cos, sin], axis=1)


def _proj_a(x, g, w_bf, cs):
    n = x.shape[0]
    tm = min(ROW_TILE, n)
    n_out = w_bf.shape[1]
    cs_blocks = cs.shape[0] // tm
    return pl.pallas_call(
        _proj_a_kernel,
        grid=(n // tm,),
        in_specs=[pl.BlockSpec((tm, D_MODEL), lambda i: (i, 0)),
                  pl.BlockSpec((1, D_MODEL), lambda i: (0, 0)),
                  pl.BlockSpec((D_MODEL, n_out), lambda i: (0, 0)),
                  pl.BlockSpec((tm, 2 * LANES), lambda i: (i % cs_blocks, 0))],
        out_specs=pl.BlockSpec((tm, n_out), lambda i: (i, 0)),
        out_shape=jax.ShapeDtypeStruct((n, n_out), F32),
        compiler_params=_params(1),
        name="proj_a",
    )(x, g, w_bf, cs)


def _sink_attention(q, k_bf, v_bf, sink_ref, valid):
    def scores(h):
        g = h // A_GROUP
        qh = (q[:, h * HEAD_DIM:(h + 1) * HEAD_DIM] * SCALE).astype(BF16)
        kh = k_bf[:, g * HEAD_DIM:(g + 1) * HEAD_DIM]
        return lax.dot_general(qh, kh, (((1,), (1,)), ((), ())), preferred_element_type=F32)

    def attend(h, s):
        g = h // A_GROUP
        vh = v_bf[:, g * HEAD_DIM:(g + 1) * HEAD_DIM]
        if valid is not None:
            s = jnp.where(valid, s, NEG_INF)
        sink = sink_ref[h]
        m = jnp.maximum(jnp.max(s, axis=-1, keepdims=True), sink)
        e = jnp.exp(s - m)
        den = jnp.sum(e, axis=-1, keepdims=True) + jnp.exp(sink - m)
        return jnp.dot(e.astype(BF16), vh, preferred_element_type=F32) / den

    s, outs = {}, []
    for t in range(N_HEADS + 1):
        if t < N_HEADS:
            s[t] = scores(t)
        if t >= 1:
            outs.append(attend(t - 1, s.pop(t - 1)))
    return jnp.concatenate(outs, axis=1).astype(BF16)


def _attn_a_prompt_kernel(sink_ref, q_ref, kp_ref, kc_ref, vp_ref, vc_ref, o_ref, *, blocks_per_seq):
    i = pl.program_id(0)
    has_prev = (i % blocks_per_seq) != 0
    k = jnp.concatenate([kp_ref[...], kc_ref[...]], axis=0).astype(BF16)
    v = jnp.concatenate([vp_ref[...], vc_ref[...]], axis=0).astype(BF16)
    rows, keys = ATTN_A_BLOCK, 2 * ATTN_A_BLOCK
    q_chunk = lax.broadcasted_iota(jnp.int32, (rows, keys), 0) // CHUNK
    col = lax.broadcasted_iota(jnp.int32, (rows, keys), 1)
    k_chunk = col // CHUNK
    valid = (k_chunk >= q_chunk) & (k_chunk <= q_chunk + WINDOW // CHUNK)
    valid = valid & ((col >= ATTN_A_BLOCK) | has_prev)
    o_ref[...] = _sink_attention(q_ref[...], k, v, sink_ref, valid)


def _attn_a_prompt(qkv, sinks, seq):
    n = qkv.shape[0]
    blk = ATTN_A_BLOCK
    blocks_per_seq = seq // blk
    kcol = D_MODEL // A_KV_DIM
    prev = lambda i: jnp.maximum(i - 1, 0)
    return pl.pallas_call(
        functools.partial(_attn_a_prompt_kernel, blocks_per_seq=blocks_per_seq),
        grid=(n // blk,),
        in_specs=[pl.BlockSpec(memory_space=pltpu.SMEM),
                  pl.BlockSpec((blk, D_MODEL), lambda i: (i, 0)),
                  pl.BlockSpec((blk, A_KV_DIM), lambda i: (prev(i), kcol)),
                  pl.BlockSpec((blk, A_KV_DIM), lambda i: (i, kcol)),
                  pl.BlockSpec((blk, A_KV_DIM), lambda i: (prev(i), kcol + 1)),
                  pl.BlockSpec((blk, A_KV_DIM), lambda i: (i, kcol + 1))],
        out_specs=pl.BlockSpec((blk, D_MODEL), lambda i: (i, 0)),
        out_shape=jax.ShapeDtypeStruct((n, D_MODEL), BF16),
        compiler_params=_params(1),
        name="attn_a_prompt",
    )(sinks, qkv, qkv, qkv, qkv, qkv)


def _attn_a_sample_kernel(sink_ref, q_ref, kn_ref, vn_ref, ck_ref, cv_ref, o_ref):
    k = jnp.concatenate([ck_ref[0], kn_ref[...]], axis=0).astype(BF16)
    v = jnp.concatenate([cv_ref[0], vn_ref[...]], axis=0).astype(BF16)
    o_ref[...] = _sink_attention(q_ref[...], k, v, sink_ref, None)


def _attn_a_sample(qkv, cache_k, cache_v, sinks):
    streams = cache_k.shape[0]
    t = qkv.shape[0] // streams
    kcol = D_MODEL // A_KV_DIM
    return pl.pallas_call(
        _attn_a_sample_kernel,
        grid=(streams,),
        in_specs=[pl.BlockSpec(memory_space=pltpu.SMEM),
                  pl.BlockSpec((t, D_MODEL), lambda b: (b, 0)),
                  pl.BlockSpec((t, A_KV_DIM), lambda b: (b, kcol)),
                  pl.BlockSpec((t, A_KV_DIM), lambda b: (b, kcol + 1)),
                  pl.BlockSpec((1, WINDOW, A_KV_DIM), lambda b: (b, 0, 0)),
                  pl.BlockSpec((1, WINDOW, A_KV_DIM), lambda b: (b, 0, 0))],
        out_specs=pl.BlockSpec((t, D_MODEL), lambda b: (b, 0)),
        out_shape=jax.ShapeDtypeStruct((qkv.shape[0], D_MODEL), BF16),
        compiler_params=_params(1),
        name="attn_a_sample",
    )(sinks, qkv, qkv, qkv, cache_k, cache_v)


def _proj_b_kernel(x_ref, g_ref, w_ref, q_ref, k_ref, v_ref, kb_ref, vb_ref):
    xn = _rmsnorm(x_ref[...], g_ref[...]).astype(BF16)
    for j in range(D_MODEL // LANES):
        cols = slice(j * LANES, (j + 1) * LANES)
        q = jnp.dot(xn, w_ref[:, cols], preferred_element_type=F32)
        q_ref[:, cols] = (q * SCALE).astype(BF16)
        k = jnp.dot(xn, w_ref[:, D_MODEL + j * LANES:D_MODEL + (j + 1) * LANES], preferred_element_type=F32)
        kb_ref[:, cols] = k.astype(BF16)
        v = jnp.dot(xn, w_ref[:, 2 * D_MODEL + j * LANES:2 * D_MODEL + (j + 1) * LANES],
                    preferred_element_type=F32)
        vb_ref[:, cols] = v.astype(BF16)
        for half in range(LANES // HEAD_DIM):
            h = j * (LANES // HEAD_DIM) + half
            k_ref[:, h, :] = k[:, half * HEAD_DIM:(half + 1) * HEAD_DIM]
            v_ref[:, h, :] = v[:, half * HEAD_DIM:(half + 1) * HEAD_DIM]


def _proj_b(x, g, w_bf):
    n = x.shape[0]
    tm = min(ROW_TILE, n)
    row = lambda i: (i, 0)
    return pl.pallas_call(
        _proj_b_kernel,
        grid=(n // tm,),
        in_specs=[pl.BlockSpec((tm, D_MODEL), row),
                  pl.BlockSpec((1, D_MODEL), lambda i: (0, 0)),
                  pl.BlockSpec((D_MODEL, 3 * D_MODEL), lambda i: (0, 0))],
        out_specs=[pl.BlockSpec((tm, D_MODEL), row),
                   pl.BlockSpec((tm, N_HEADS, HEAD_DIM), lambda i: (i, 0, 0)),
                   pl.BlockSpec((tm, N_HEADS, HEAD_DIM), lambda i: (i, 0, 0)),
                   pl.BlockSpec((tm, D_MODEL), row),
                   pl.BlockSpec((tm, D_MODEL), row)],
        out_shape=[jax.ShapeDtypeStruct((n, D_MODEL), BF16),
                   jax.ShapeDtypeStruct((n, N_HEADS, HEAD_DIM), F32),
                   jax.ShapeDtypeStruct((n, N_HEADS, HEAD_DIM), F32),
                   jax.ShapeDtypeStruct((n, D_MODEL), BF16),
                   jax.ShapeDtypeStruct((n, D_MODEL), BF16)],
        compiler_params=_params(1),
        name="proj_b",
    )(x, g, w_bf)


def _proj_bt_kernel(x_ref, g_ref, wq_ref, wkt_ref, wvt_ref, q_ref, kt_ref, vt_ref, ktb_ref, vtb_ref):
    xn = _rmsnorm(x_ref[...], g_ref[...]).astype(BF16)
    q = jnp.dot(xn, wq_ref[...], preferred_element_type=F32)
    q_ref[...] = (q * SCALE).astype(BF16)
    kt = lax.dot_general(wkt_ref[...], xn, _CONTRACT_LAST, preferred_element_type=F32)
    kt_ref[...] = kt
    ktb_ref[...] = kt.astype(BF16)
    vt = lax.dot_general(wvt_ref[...], xn, _CONTRACT_LAST, preferred_element_type=F32)
    vt_ref[...] = vt
    vtb_ref[...] = vt.astype(BF16)


def _proj_bt(x, g, wq_bf, wkt_bf, wvt_bf, seq):
    n = x.shape[0]
    tm = min(ROW_TILE, seq)
    tiles = seq // tm
    row = lambda i: (i, 0)
    fixed = lambda i: (0, 0)
    col = lambda i: (i // tiles, 0, i % tiles)
    feature_major = lambda dtype: jax.ShapeDtypeStruct((n // seq, D_MODEL, seq), dtype)
    return pl.pallas_call(
        _proj_bt_kernel,
        grid=(n // tm,),
        in_specs=[pl.BlockSpec((tm, D_MODEL), row),
                  pl.BlockSpec((1, D_MODEL), fixed),
                  pl.BlockSpec((D_MODEL, D_MODEL), fixed),
                  pl.BlockSpec((D_MODEL, D_MODEL), fixed),
                  pl.BlockSpec((D_MODEL, D_MODEL), fixed)],
        out_specs=[pl.BlockSpec((tm, D_MODEL), row)] + [pl.BlockSpec((None, D_MODEL, tm), col)] * 4,
        out_shape=[jax.ShapeDtypeStruct((n, D_MODEL), BF16),
                   feature_major(F32), feature_major(F32), feature_major(BF16), feature_major(BF16)],
        compiler_params=_params(1),
        name="proj_bt",
    )(x, g, wq_bf, wkt_bf, wvt_bf)


def _strict_lower_ones(n):
    j = lax.broadcasted_iota(jnp.int32, (2 * n, n), 0) % n
    s = lax.broadcasted_iota(jnp.int32, (2 * n, n), 1)
    return jnp.where(j > s, 1.0, 0.0).astype(BF16)


_CONTRACT_LAST = (((1,), (1,)), ((), ()))


def _sb_scores(qh, kh, transposed):
    if transposed:
        return jnp.dot(qh, kh, preferred_element_type=F32)
    return lax.dot_general(qh, kh, _CONTRACT_LAST, preferred_element_type=F32)


def _sb_suffix(z, tri, before):
    softplus = jnp.maximum(z, 0.0) + jnp.log(1.0 + jnp.exp(-jnp.abs(z)))
    log_beta = z - softplus
    if before is not None:
        softplus = jnp.where(before, softplus, 0.0)
    hi = softplus.astype(BF16)
    lo = (softplus - hi.astype(F32)).astype(BF16)
    later = jnp.dot(jnp.concatenate([hi, lo], axis=1), tri, preferred_element_type=F32)
    return log_beta, later, later[:, 0:1] + softplus[:, 0:1]


def _sb_values(log_beta, later, vh, before, transposed):
    a = jnp.exp(log_beta - later)
    if before is not None:
        a = jnp.where(before, a, 0.0)
    if transposed:
        return lax.dot_general(a.astype(BF16), vh, _CONTRACT_LAST, preferred_element_type=F32)
    return jnp.dot(a.astype(BF16), vh, preferred_element_type=F32)


def _head_cols(ref):
    return lambda h: ref[:, h * HEAD_DIM:(h + 1) * HEAD_DIM].astype(BF16)


def _head_rows(ref):
    return lambda h: ref[h * HEAD_DIM:(h + 1) * HEAD_DIM, :].astype(BF16)


def _sb_block(q_ref, k_head, v_head, tri, carry_ref, acc_ref, before, transposed):
    first_head = lax.broadcasted_iota(jnp.int32, carry_ref.shape[1:], 1) < HEAD_DIM
    q_head = _head_cols(q_ref)
    carries = [carry_ref[pair] for pair in range(N_HEADS // 2)]
    z, mid, outs, masses = {}, {}, {}, {}
    for t in range(N_HEADS + 2 * SB_STAGE_SKEW):
        if t < N_HEADS:
            z[t] = _sb_scores(q_head(t), k_head(t), transposed)
        h = t - SB_STAGE_SKEW
        if 0 <= h < N_HEADS:
            log_beta, later, masses[h] = _sb_suffix(z.pop(h), tri, before)
            mid[h] = (log_beta, later)
        h = t - 2 * SB_STAGE_SKEW
        if 0 <= h < N_HEADS:
            outs[h] = _sb_values(*mid.pop(h), v_head(h), before, transposed)
    adds = []
    for pair in range(N_HEADS // 2):
        h = 2 * pair
        adds.append(jnp.exp(-carries[pair]) * jnp.concatenate([outs[h], outs[h + 1]], axis=1))
        carries[pair] = carries[pair] + jnp.where(first_head, masses[h], masses[h + 1])
    acc_ref[...] += jnp.concatenate(adds, axis=1)
    least = carries[0]
    for pair in range(N_HEADS // 2):
        carry_ref[pair] = carries[pair]
        least = jnp.minimum(least, carries[pair])
    return (jnp.min(least) <= SB_DECAY_LIMIT).astype(jnp.int32)


def _sb_old_blocks(n_old, fetch, block, live):
    @pl.when(n_old > 0)
    def _():
        for copy in fetch(1, 0):
            copy.start()

    def cond(state):
        j, live = state
        return jnp.logical_and(j <= n_old, live == 1)

    def body(state):
        j, _ = state
        slot = (j - 1) % 2
        for copy in fetch(j, slot):
            copy.wait()

        @pl.when(j < n_old)
        def _():
            for copy in fetch(j + 1, 1 - slot):
                copy.start()

        return j + 1, block(slot)

    j_end, _ = lax.while_loop(cond, body, (jnp.int32(1), live))

    @pl.when(j_end <= n_old)
    def _():
        for copy in fetch(j_end, (j_end - 1) % 2):
            copy.wait()


def _sb_prompt_kernel(q_ref, kd_ref, vd_ref, k_hbm, v_hbm, o_ref, kbuf, vbuf, sem, acc_ref, carry_ref,
                      *, blocks_per_seq):
    i = pl.program_id(0)
    n = SB_BLOCK
    tri = _strict_lower_ones(n)
    b = i // blocks_per_seq
    qi = i % blocks_per_seq

    def fetch(j, slot):
        cols = pl.ds((qi - j) * n, n)
        return (pltpu.make_async_copy(k_hbm.at[b, :, cols], kbuf.at[slot], sem.at[0, slot]),
                pltpu.make_async_copy(v_hbm.at[b, :, cols], vbuf.at[slot], sem.at[1, slot]))

    def old_block(slot):
        return _sb_block(q_ref, _head_rows(kbuf.at[slot]), _head_rows(vbuf.at[slot]), tri, carry_ref, acc_ref,
                         None, True)

    acc_ref[...] = jnp.zeros_like(acc_ref)
    carry_ref[...] = jnp.zeros_like(carry_ref)
    row = lax.broadcasted_iota(jnp.int32, (n, n), 0)
    col = lax.broadcasted_iota(jnp.int32, (n, n), 1)
    live = _sb_block(q_ref, _head_rows(kd_ref), _head_rows(vd_ref), tri, carry_ref, acc_ref, col < row, True)
    _sb_old_blocks(qi, fetch, old_block, live)
    o_ref[...] = acc_ref[...].astype(BF16)


def _attn_b_prompt(q_bf, kt_bf, vt_bf):
    n = q_bf.shape[0]
    seq = kt_bf.shape[2]
    blk = SB_BLOCK
    bps = seq // blk
    row = lambda i: (i, 0)
    diag = lambda i: (i // bps, 0, i % bps)
    return pl.pallas_call(
        functools.partial(_sb_prompt_kernel, blocks_per_seq=bps),
        grid=(n // blk,),
        in_specs=[pl.BlockSpec((blk, D_MODEL), row),
                  pl.BlockSpec((None, D_MODEL, blk), diag),
                  pl.BlockSpec((None, D_MODEL, blk), diag),
                  pl.BlockSpec(memory_space=pl.ANY),
                  pl.BlockSpec(memory_space=pl.ANY)],
        out_specs=pl.BlockSpec((blk, D_MODEL), row),
        out_shape=jax.ShapeDtypeStruct((n, D_MODEL), BF16),
        scratch_shapes=[pltpu.VMEM((2, D_MODEL, blk), BF16),
                        pltpu.VMEM((2, D_MODEL, blk), BF16),
                        pltpu.SemaphoreType.DMA((2, 2)),
                        pltpu.VMEM((blk, D_MODEL), F32),
                        pltpu.VMEM((N_HEADS // 2, blk, LANES), F32)],
        compiler_params=_params(1),
        name="attn_b_prompt",
    )(q_bf, kt_bf, vt_bf, kt_bf, vt_bf)


def _sb_sample_kernel(q_ref, kn_ref, vn_ref, ck_hbm, cv_hbm, o_ref, kbuf, vbuf, sem, acc_ref, carry_ref,
                      *, n_cache):
    b = pl.program_id(0)
    t = q_ref.shape[0]
    n = SB_BLOCK
    tri = _strict_lower_ones(n)

    def fetch(j, slot):
        cols = pl.ds((n_cache - j) * n, n)
        return (pltpu.make_async_copy(ck_hbm.at[b, :, cols], kbuf.at[slot], sem.at[0, slot]),
                pltpu.make_async_copy(cv_hbm.at[b, :, cols], vbuf.at[slot], sem.at[1, slot]))

    def old_block(slot):
        return _sb_block(q_ref, _head_rows(kbuf.at[slot]), _head_rows(vbuf.at[slot]), tri, carry_ref, acc_ref,
                         None, True)

    acc_ref[...] = jnp.zeros_like(acc_ref)
    carry_ref[...] = jnp.zeros_like(carry_ref)
    row = lax.broadcasted_iota(jnp.int32, (t, t), 0)
    col = lax.broadcasted_iota(jnp.int32, (t, t), 1)
    live = _sb_block(q_ref, _head_cols(kn_ref), _head_cols(vn_ref), _strict_lower_ones(t), carry_ref, acc_ref,
                     col < row, False)
    _sb_old_blocks(jnp.int32(n_cache), fetch, old_block, live)
    o_ref[...] = acc_ref[...].astype(BF16)


def _attn_b_sample(q_bf, k_bf, v_bf, cache_k, cache_v):
    streams, _, past = cache_k.shape
    t = q_bf.shape[0] // streams
    blk = SB_BLOCK
    new = lambda b: (b, 0)
    return pl.pallas_call(
        functools.partial(_sb_sample_kernel, n_cache=past // blk),
        grid=(streams,),
        in_specs=[pl.BlockSpec((t, D_MODEL), new),
                  pl.BlockSpec((t, D_MODEL), new),
                  pl.BlockSpec((t, D_MODEL), new),
                  pl.BlockSpec(memory_space=pl.ANY),
                  pl.BlockSpec(memory_space=pl.ANY)],
        out_specs=pl.BlockSpec((t, D_MODEL), new),
        out_shape=jax.ShapeDtypeStruct(q_bf.shape, BF16),
        scratch_shapes=[pltpu.VMEM((2, D_MODEL, blk), F32),
                        pltpu.VMEM((2, D_MODEL, blk), F32),
                        pltpu.SemaphoreType.DMA((2, 2)),
                        pltpu.VMEM((t, D_MODEL), F32),
                        pltpu.VMEM((N_HEADS // 2, t, LANES), F32)],
        compiler_params=_params(1),
        name="attn_b_sample",
    )(q_bf, k_bf, v_bf, cache_k, cache_v)


ROUTE_ROWS = 48


def _oproj_route_kernel(o_ref, wo_ref, h_ref, g_ref, wr_hi_ref, wr_lo_ref, base_ref,
                        h_out_ref, xn_ref, route_ref, count_ref):
    @pl.when(pl.program_id(0) == 0)
    def _():
        count_ref[...] = base_ref[...]

    h = h_ref[...] + jnp.dot(o_ref[...], wo_ref[...], preferred_element_type=F32)
    h_out_ref[...] = h
    xn = _rmsnorm(h, g_ref[...])
    x_hi = xn.astype(BF16)
    bits = lax.bitcast_convert_type(x_hi.astype(F32), jnp.uint32)
    half = D_MODEL // 2
    xn_ref[...] = (bits[:, :half] >> 16) | bits[:, half:]
    x_lo = (xn - x_hi.astype(F32)).astype(BF16)
    logits = (lax.dot_general(wr_hi_ref[...], x_hi, _CONTRACT_LAST, preferred_element_type=F32)
              + lax.dot_general(wr_lo_ref[...], x_hi, _CONTRACT_LAST, preferred_element_type=F32)
              + lax.dot_general(wr_hi_ref[...], x_lo, _CONTRACT_LAST, preferred_element_type=F32))
    tm = logits.shape[1]
    sub = EXPERTS_PER_GROUP
    row = lax.broadcasted_iota(jnp.int32, (sub, tm), 0)

    def top1(vals):
        peak = jnp.max(vals, axis=0, keepdims=True)
        return peak, jnp.min(jnp.where(vals == peak, row, sub), axis=0, keepdims=True)

    g_mask = row < N_GROUPS
    g_logits = jnp.where(g_mask, logits[:sub], -jnp.inf)
    g_max, g_idx = top1(g_logits)
    g_sum = jnp.sum(jnp.where(g_mask, jnp.exp(g_logits - g_max), 0.0), axis=0, keepdims=True)
    g_p = 1.0 / g_sum

    e_logits = logits[sub:2 * sub]
    for g in range(1, N_GROUPS):
        e_logits = jnp.where(g_idx == g, logits[(g + 1) * sub:(g + 2) * sub], e_logits)
    m1, i1 = top1(e_logits)
    rest = jnp.where(row == i1, -jnp.inf, e_logits)
    m2, i2 = top1(rest)
    e_sum = jnp.sum(jnp.exp(e_logits - m1), axis=0, keepdims=True)
    p1 = 1.0 / e_sum
    p2 = jnp.exp(m2 - m1) / e_sum
    top_sum = p1 + p2
    w1 = g_p * p1 / top_sum
    w2 = g_p * p2 / top_sum
    e1 = g_idx * sub + i1
    e2 = g_idx * sub + i2
    expert = lax.broadcasted_iota(jnp.int32, (N_EXPERTS, tm), 0)
    pick1 = expert == e1
    pick2 = expert == e2
    picks = jnp.where(pick1 | pick2, 1.0, 0.0)
    earlier = jnp.where(lax.broadcasted_iota(jnp.int32, (tm, tm), 0) < lax.broadcasted_iota(jnp.int32, (tm, tm), 1),
                        1.0, 0.0).astype(BF16)
    counts = count_ref[...]
    before = jnp.dot(picks.astype(BF16), earlier, preferred_element_type=F32) + counts[:, 0:1]
    rank1 = jnp.sum(jnp.where(pick1, before, 0.0), axis=0, keepdims=True)
    rank2 = jnp.sum(jnp.where(pick2, before, 0.0), axis=0, keepdims=True)
    count_ref[...] = counts + jnp.sum(picks, axis=1, keepdims=True)
    vals = (e1.astype(F32), e2.astype(F32), w1, w2, rank1, rank2)
    route = jnp.zeros((sub, tm), F32)
    for c, val in enumerate(vals):
        route = jnp.where(row == c, val, route)
    route_ref[...] = route


def _oproj_route(o_bf, wo_bf, h, g, wr_hi, wr_lo, base_counts):
    n = h.shape[0]
    tm = min(ROW_TILE, n)
    row = lambda i: (i, 0)
    fixed = lambda i: (0, 0)
    return pl.pallas_call(
        _oproj_route_kernel,
        grid=(n // tm,),
        in_specs=[pl.BlockSpec((tm, D_MODEL), row),
                  pl.BlockSpec((D_MODEL, D_MODEL), fixed),
                  pl.BlockSpec((tm, D_MODEL), row),
                  pl.BlockSpec((1, D_MODEL), fixed),
                  pl.BlockSpec((ROUTE_ROWS, D_MODEL), fixed),
                  pl.BlockSpec((ROUTE_ROWS, D_MODEL), fixed),
                  pl.BlockSpec((N_EXPERTS, LANES), fixed)],
        out_specs=[pl.BlockSpec((tm, D_MODEL), row),
                   pl.BlockSpec((tm, D_MODEL // 2), row),
                   pl.BlockSpec((EXPERTS_PER_GROUP, tm), lambda i: (0, i)),
                   pl.BlockSpec((N_EXPERTS, LANES), fixed)],
        out_shape=[jax.ShapeDtypeStruct((n, D_MODEL), F32),
                   jax.ShapeDtypeStruct((n, D_MODEL // 2), jnp.uint32),
                   jax.ShapeDtypeStruct((EXPERTS_PER_GROUP, n), F32),
                   jax.ShapeDtypeStruct((N_EXPERTS, LANES), F32)],
        compiler_params=_params(1),
        name="oproj_route",
    )(o_bf, wo_bf, h, g, wr_hi, wr_lo, base_counts)


def _dispatch_kernel(dest_ref, xn_ref, buf_in_ref, buf_ref, sem):
    del buf_in_ref
    i = pl.program_id(0)
    tm = xn_ref.shape[0]
    base = i * (tm * TOP_K)

    def row_copy(t, slot):
        return pltpu.make_async_copy(xn_ref.at[pl.ds(t, 1)], buf_ref.at[pl.ds(slot, 1)], sem)

    def issue(t, carry):
        for k in range(TOP_K):
            row_copy(t, dest_ref[base + TOP_K * t + k]).start(priority=k % 2)
        return carry

    lax.fori_loop(0, tm, issue, 0, unroll=8)
    for k in range(TOP_K):
        pltpu.make_async_copy(xn_ref, buf_ref.at[pl.ds(0, tm)], sem).wait()


def _dispatch(dest, xn, buf):
    n = xn.shape[0]
    tm = min(ROW_TILE, n)
    grid_spec = pltpu.PrefetchScalarGridSpec(
        num_scalar_prefetch=1,
        grid=(n // tm,),
        in_specs=[pl.BlockSpec((tm, xn.shape[1]), lambda i, d: (i, 0)),
                  pl.BlockSpec(memory_space=pl.ANY)],
        out_specs=pl.BlockSpec(memory_space=pl.ANY),
        scratch_shapes=[pltpu.SemaphoreType.DMA(())])
    return pl.pallas_call(
        _dispatch_kernel,
        grid_spec=grid_spec,
        out_shape=jax.ShapeDtypeStruct(buf.shape, buf.dtype),
        input_output_aliases={2: 0},
        compiler_params=_params(1, disable_bounds_checks=True, has_side_effects=True),
        name="moe_dispatch",
    )(dest, xn, buf)


def _expert_kernel(be_ref, nused_ref, x_ref, wg_ref, wu_ref, wd_ref, o_ref, wg_bf, wu_bf, wd_bf):
    i = pl.program_id(0)
    used = i < nused_ref[0]

    @pl.when(used)
    def _():
        @pl.when(jnp.logical_or(i == 0, be_ref[i] != be_ref[jnp.maximum(i - 1, 0)]))
        def _():
            wg_bf[...] = wg_ref[0].astype(BF16)
            wu_bf[...] = wu_ref[0].astype(BF16)
            wd_bf[...] = wd_ref[0].astype(BF16)

        words = x_ref[...]
        low = lax.bitcast_convert_type(words << 16, F32)
        high = lax.bitcast_convert_type(words & jnp.uint32(0xFFFF0000), F32)
        x = jnp.concatenate([low, high], axis=1).astype(BF16)
        a = jnp.dot(x, wg_bf[...], preferred_element_type=F32)
        b = jnp.dot(x, wu_bf[...], preferred_element_type=F32)
        mid = (a * (1.0 / (1.0 + jnp.exp(-a))) * b).astype(BF16)
        o_ref[...] = jnp.dot(mid, wd_bf[...], preferred_element_type=F32)

    @pl.when(jnp.logical_not(used))
    def _():
        o_ref[...] = jnp.zeros_like(o_ref)


def _experts(block_e, n_used, buf, w_gate, w_up, w_down, layer):
    bm = EXPERT_BLOCK
    n_blocks = buf.shape[0] // bm
    blk = lambda i, be, nu: (jnp.minimum(i, nu[0] - 1), 0)
    wsel = lambda i, be, nu: (layer, be[jnp.minimum(i, nu[0] - 1)], 0, 0)
    grid_spec = pltpu.PrefetchScalarGridSpec(
        num_scalar_prefetch=2,
        grid=(n_blocks,),
        in_specs=[pl.BlockSpec((bm, buf.shape[1]), blk),
                  pl.BlockSpec((None, 1, D_MODEL, D_EXPERT), wsel),
                  pl.BlockSpec((None, 1, D_MODEL, D_EXPERT), wsel),
                  pl.BlockSpec((None, 1, D_EXPERT, D_MODEL), wsel)],
        out_specs=pl.BlockSpec((bm, D_MODEL), lambda i, be, nu: (i, 0)),
        scratch_shapes=[pltpu.VMEM((D_MODEL, D_EXPERT), BF16),
                        pltpu.VMEM((D_MODEL, D_EXPERT), BF16),
                        pltpu.VMEM((D_EXPERT, D_MODEL), BF16)])
    return pl.pallas_call(
        _expert_kernel,
        grid_spec=grid_spec,
        out_shape=jax.ShapeDtypeStruct((buf.shape[0], D_MODEL), F32),
        compiler_params=_params(1),
        name="moe_experts",
    )(block_e, n_used, buf, w_gate, w_up, w_down)


def _combine_kernel(dest_ref, h_ref, gate_ref, g_ref, yb_ref, o_ref, rows_ref, sem, *, final_norm):
    i = pl.program_id(0)
    n_steps = pl.num_programs(0)
    tm = h_ref.shape[0]

    def row_copy(step, t, k):
        slot = step % 2
        src = dest_ref[step * (tm * TOP_K) + TOP_K * t + k]
        return pltpu.make_async_copy(yb_ref.at[pl.ds(src, 1)], rows_ref.at[slot, k, pl.ds(t, 1)], sem.at[slot])

    def issue(step):
        def body(t, carry):
            for k in range(TOP_K):
                row_copy(step, t, k).start(priority=k % 2)
            return carry
        lax.fori_loop(0, tm, body, 0, unroll=8)

    @pl.when(i == 0)
    def _():
        issue(i)

    @pl.when(i + 1 < n_steps)
    def _():
        issue(i + 1)

    slot = i % 2
    for k in range(TOP_K):
        pltpu.make_async_copy(yb_ref.at[pl.ds(0, tm)], rows_ref.at[slot, k], sem.at[slot]).wait()

    gates = gate_ref[...]
    out = h_ref[...] + (rows_ref[slot, 0] * gates[:, 0:1] + rows_ref[slot, 1] * gates[:, 1:2])
    if final_norm:
        out = _rmsnorm(out, g_ref[...])
    o_ref[...] = out


def _combine(dest, h, gates, g, yb, final_norm):
    n = h.shape[0]
    tm = min(COMBINE_TILE, n)
    grid_spec = pltpu.PrefetchScalarGridSpec(
        num_scalar_prefetch=1,
        grid=(n // tm,),
        in_specs=[pl.BlockSpec((tm, D_MODEL), lambda i, d: (i, 0)),
                  pl.BlockSpec((tm, TOP_K), lambda i, d: (i, 0)),
                  pl.BlockSpec((1, D_MODEL), lambda i, d: (0, 0)),
                  pl.BlockSpec(memory_space=pl.ANY)],
        out_specs=pl.BlockSpec((tm, D_MODEL), lambda i, d: (i, 0)),
        scratch_shapes=[pltpu.VMEM((2, TOP_K, tm, D_MODEL), F32),
                        pltpu.SemaphoreType.DMA((2,))])
    return pl.pallas_call(
        functools.partial(_combine_kernel, final_norm=final_norm),
        grid_spec=grid_spec,
        out_shape=jax.ShapeDtypeStruct(h.shape, F32),
        compiler_params=_params(1, disable_bounds_checks=True),
        name="moe_combine",
    )(dest, h, gates, g, yb)


def _route_plan(route_p, route_s, counts):
    route = jnp.concatenate([route_p, route_s], axis=1)
    flat_e = route[:TOP_K].T.astype(jnp.int32).reshape(-1)
    rank = route[2 * TOP_K:3 * TOP_K].T.astype(jnp.int32).reshape(-1)
    n_assign = flat_e.shape[0]
    counts = counts[:, 0].astype(jnp.int32)
    bm = EXPERT_BLOCK
    padded = (counts + bm - 1) // bm * bm
    pends = jnp.cumsum(padded)
    onehot = flat_e[:, None] == jnp.arange(N_EXPERTS, dtype=jnp.int32)[None, :]
    dest = jnp.sum(jnp.where(onehot, (pends - padded)[None, :], 0), axis=1) + rank
    n_blocks = -(-n_assign // bm) + N_EXPERTS
    first_rows = jnp.arange(n_blocks, dtype=jnp.int32) * bm
    block_e = jnp.minimum(jnp.sum((pends[None, :] <= first_rows[:, None]).astype(jnp.int32), axis=1), N_EXPERTS - 1)
    n_used = (pends[-1] // bm).astype(jnp.int32).reshape(1)
    return dest.astype(jnp.int32), block_e, n_used, n_blocks


def _moe(hp, hs, route_p, route_s, counts, xn_p, xn_s, g_final, w_gate, w_up, w_down, layer, final_norm):
    dest, block_e, n_used, n_blocks = _route_plan(route_p, route_s, counts)
    n_p = hp.shape[0] * TOP_K
    dest_p, dest_s = dest[:n_p], dest[n_p:]
    buf = jnp.zeros((n_blocks * EXPERT_BLOCK, xn_p.shape[1]), xn_p.dtype)
    buf = _dispatch(dest_p, xn_p, buf)
    buf = _dispatch(dest_s, xn_s, buf)
    yb = _experts(block_e, n_used, buf, w_gate, w_up, w_down, layer)
    out_p = _combine(dest_p, hp, route_p[TOP_K:2 * TOP_K].T, g_final, yb, final_norm)
    out_s = _combine(dest_s, hs, route_s[TOP_K:2 * TOP_K].T, g_final, yb, final_norm)
    return out_p, out_s


def _router_weights(w_group, w_router):
    w_exp = jnp.transpose(w_router, (0, 2, 1)).reshape(N_EXPERTS, D_MODEL)
    zeros = lambda rows: jnp.zeros((rows, D_MODEL), F32)
    w = jnp.concatenate([w_group.T, zeros(EXPERTS_PER_GROUP - N_GROUPS), w_exp,
                         zeros(ROUTE_ROWS - EXPERTS_PER_GROUP - N_EXPERTS)], axis=0)
    hi = w.astype(BF16)
    lo = (w - hi.astype(F32)).astype(BF16)
    return hi, lo


def kernel(x_prompt, x_sample, cache_a_k, cache_a_v, cache_b_k, cache_b_v, norm_mix, norm_ffn, norm_final,
           a_w_qkv, a_w_o, a_sinks, b_w_qkv, b_w_o, moe_w_group, moe_w_router, moe_w_gate, moe_w_up, moe_w_down):
    batch, seq, _ = x_prompt.shape
    streams, t_new, _ = x_sample.shape
    past = cache_b_k.shape[2]
    hp = x_prompt.reshape(batch * seq, D_MODEL)
    hs = x_sample.reshape(streams * t_new, D_MODEL)
    g_final = norm_final.reshape(1, D_MODEL)

    cs_p = _rope_table(jnp.arange(seq, dtype=jnp.int32))
    cs_s = _rope_table(jnp.tile(past + jnp.arange(t_new, dtype=jnp.int32), streams))

    def moe_layer(i, hp, hs, op, os_, w_o, final_norm):
        wo_bf = w_o.astype(BF16)
        g = norm_ffn[i].reshape(1, D_MODEL)
        wr_hi, wr_lo = _router_weights(moe_w_group[i], moe_w_router[i])
        hp, xn_p, route_p, counts = _oproj_route(op, wo_bf, hp, g, wr_hi, wr_lo, jnp.zeros((N_EXPERTS, LANES), F32))
        hs, xn_s, route_s, counts = _oproj_route(os_, wo_bf, hs, g, wr_hi, wr_lo, counts)
        return _moe(hp, hs, route_p, route_s, counts, xn_p, xn_s, g_final,
                    moe_w_gate, moe_w_up, moe_w_down, i, final_norm)

    g0 = norm_mix[0].reshape(1, D_MODEL)
    wa_bf = a_w_qkv[0].astype(BF16)
    qkv_p = _proj_a(hp, g0, wa_bf, cs_p)
    qkv_s = _proj_a(hs, g0, wa_bf, cs_s)
    ck = cache_a_k[0].reshape(streams, WINDOW, A_KV_DIM)
    cv = cache_a_v[0].reshape(streams, WINDOW, A_KV_DIM)
    op = _attn_a_prompt(qkv_p, a_sinks[0], seq)
    os_ = _attn_a_sample(qkv_s, ck, cv, a_sinks[0])
    tail_p = qkv_p.reshape(batch, seq, -1)[:, -WINDOW:, D_MODEL:]
    new_a_k_prompt = tail_p[:, :, :A_KV_DIM].reshape(1, batch, WINDOW, A_KV_HEADS, HEAD_DIM)
    new_a_v_prompt = tail_p[:, :, A_KV_DIM:].reshape(1, batch, WINDOW, A_KV_HEADS, HEAD_DIM)
    k_s = qkv_s[:, D_MODEL:D_MODEL + A_KV_DIM].reshape(streams, t_new, A_KV_HEADS, HEAD_DIM)
    v_s = qkv_s[:, D_MODEL + A_KV_DIM:].reshape(streams, t_new, A_KV_HEADS, HEAD_DIM)
    new_a_k_sample = jnp.concatenate([cache_a_k[0], k_s], axis=1)[None, :, -WINDOW:]
    new_a_v_sample = jnp.concatenate([cache_a_v[0], v_s], axis=1)[None, :, -WINDOW:]
    hp, hs = moe_layer(0, hp, hs, op, os_, a_w_o[0], False)

    g1 = norm_mix[1].reshape(1, D_MODEL)
    wb_bf = b_w_qkv[0].astype(BF16)
    wq_bf = wb_bf[:, :D_MODEL]
    wkt_bf = wb_bf[:, D_MODEL:2 * D_MODEL].T
    wvt_bf = wb_bf[:, 2 * D_MODEL:].T
    q_p, kt_p, vt_p, ktb_p, vtb_p = _proj_bt(hp, g1, wq_bf, wkt_bf, wvt_bf, seq)
    q_s, kf_s, vf_s, kb_s, vb_s = _proj_b(hs, g1, wb_bf)
    op = _attn_b_prompt(q_p, ktb_p, vtb_p)

    def feature_major(cache):
        return jnp.transpose(cache, (0, 1, 3, 4, 2)).reshape(streams, D_MODEL, past)

    def time_major(xt):
        return jnp.transpose(xt.reshape(1, batch, N_HEADS, HEAD_DIM, seq), (0, 1, 4, 2, 3))

    os_ = _attn_b_sample(q_s, kb_s, vb_s, feature_major(cache_b_k), feature_major(cache_b_v))
    new_b_k_prompt = time_major(kt_p)
    new_b_v_prompt = time_major(vt_p)
    new_b_k_sample = kf_s.reshape(1, streams, t_new, N_HEADS, HEAD_DIM)
    new_b_v_sample = vf_s.reshape(1, streams, t_new, N_HEADS, HEAD_DIM)
    hp, hs = moe_layer(1, hp, hs, op, os_, b_w_o[0], True)

    y_prompt = hp.reshape(batch, seq, D_MODEL)
    y_sample = hs.reshape(streams, t_new, D_MODEL)
    return (y_prompt, y_sample, new_a_k_prompt, new_a_v_prompt, new_a_k_sample, new_a_v_sample,
            new_b_k_prompt, new_b_v_prompt, new_b_k_sample, new_b_v_sample)
```

```python
import functools

import jax
import jax.numpy as jnp
import numpy as np
from jax import lax
from jax.experimental import pallas as pl
from jax.experimental.pallas import tpu as pltpu

F32 = jnp.float32
BF16 = jnp.bfloat16

D_MODEL = 1024
HEAD_DIM = 64
N_HEADS = D_MODEL // HEAD_DIM
A_KV_HEADS = 4
A_GROUP = N_HEADS // A_KV_HEADS
A_KV_DIM = A_KV_HEADS * HEAD_DIM
CHUNK = 64
WINDOW = 128
ROT_DIM = HEAD_DIM // 4
ROPE_THETA = 500000.0
N_GROUPS = 4
EXPERTS_PER_GROUP = 8
N_EXPERTS = N_GROUPS * EXPERTS_PER_GROUP
TOP_K = 2
D_EXPERT = D_MODEL // 2
RMS_EPS = 1e-6
NEG_INF = -1e30
SCALE = HEAD_DIM ** -0.5

LANES = 128
MXU_COLS = 256
ROW_TILE = 512
ATTN_A_BLOCK = WINDOW
SB_BLOCK = 256
EXPERT_BLOCK = 512
COMBINE_TILE = 512
DISPATCH_TILE = 1024
VMEM_LIMIT = 48 * 1024 * 1024
SB_STAGE_SKEW = 1
SB_DECAY_LIMIT = 105.0


def _params(n_axes, **kw):
    return pltpu.CompilerParams(dimension_semantics=("arbitrary",) * n_axes,
                                vmem_limit_bytes=VMEM_LIMIT, **kw)


def _rmsnorm(x, g):
    return x * lax.rsqrt(jnp.mean(x * x, axis=-1, keepdims=True) + RMS_EPS) * g


def _proj_a_kernel(x_ref, g_ref, w_ref, cs_ref, o_ref):
    xn = _rmsnorm(x_ref[...], g_ref[...]).astype(BF16)
    cos = cs_ref[:, :LANES]
    sin = cs_ref[:, LANES:]
    lane = lax.broadcasted_iota(jnp.int32, cos.shape, 1) % HEAD_DIM
    first_half = lane < ROT_DIM // 2
    n_rot = (D_MODEL + A_KV_DIM) // LANES
    wide = MXU_COLS // LANES
    for c in range(o_ref.shape[1] // MXU_COLS):
        both = jnp.dot(xn, w_ref[:, c * MXU_COLS:(c + 1) * MXU_COLS], preferred_element_type=F32)
        for j in range(c * wide, (c + 1) * wide):
            blk = both[:, (j - c * wide) * LANES:(j - c * wide + 1) * LANES]
            if j < n_rot:
                partner = jnp.where(first_half, pltpu.roll(blk, LANES - ROT_DIM // 2, 1),
                                    pltpu.roll(blk, ROT_DIM // 2, 1))
                blk = blk * cos + partner * sin
            o_ref[:, j * LANES:(j + 1) * LANES] = blk


def _rope_table(pos):
    half = ROT_DIM // 2
    inv = ROPE_THETA ** (-jnp.arange(0, ROT_DIM, 2, dtype=F32) / ROT_DIM)
    lane = jnp.arange(LANES, dtype=jnp.int32) % HEAD_DIM
    ang = pos.astype(F32)[:, None] * inv[lane % half][None, :]
    cos = jnp.where(lane < ROT_DIM, jnp.cos(ang), 1.0)
    sin = jnp.sin(ang)
    sin = jnp.where(lane < half, -sin, jnp.where(lane < ROT_DIM, sin, 0.0))
    return jnp.concatenate([cos, sin], axis=1)


def _proj_a(x, g, w_bf, cs):
    n = x.shape[0]
    tm = min(ROW_TILE, n)
    n_out = w_bf.shape[1]
    cs_blocks = cs.shape[0] // tm
    return pl.pallas_call(
        _proj_a_kernel,
        grid=(n // tm,),
        in_specs=[pl.BlockSpec((tm, D_MODEL), lambda i: (i, 0)),
                  pl.BlockSpec((1, D_MODEL), lambda i: (0, 0)),
                  pl.BlockSpec((D_MODEL, n_out), lambda i: (0, 0)),
                  pl.BlockSpec((tm, 2 * LANES), lambda i: (i % cs_blocks, 0))],
        out_specs=pl.BlockSpec((tm, n_out), lambda i: (i, 0)),
        out_shape=jax.ShapeDtypeStruct((n, n_out), F32),
        compiler_params=_params(1),
        name="proj_a",
    )(x, g, w_bf, cs)


def _sink_attention(q, k_bf, v_bf, sink_ref, valid):
    def scores(h):
        g = h // A_GROUP
        qh = (q[:, h * HEAD_DIM:(h + 1) * HEAD_DIM] * SCALE).astype(BF16)
        kh = k_bf[:, g * HEAD_DIM:(g + 1) * HEAD_DIM]
        return lax.dot_general(qh, kh, (((1,), (1,)), ((), ())), preferred_element_type=F32)

    def attend(h, s):
        g = h // A_GROUP
        vh = v_bf[:, g * HEAD_DIM:(g + 1) * HEAD_DIM]
        if valid is not None:
            s = jnp.where(valid, s, NEG_INF)
        sink = sink_ref[h]
        m = jnp.maximum(jnp.max(s, axis=-1, keepdims=True), sink)
        e = jnp.exp(s - m)
        den = jnp.sum(e, axis=-1, keepdims=True) + jnp.exp(sink - m)
        return jnp.dot(e.astype(BF16), vh, preferred_element_type=F32) / den

    s, outs = {}, []
    for t in range(N_HEADS + 1):
        if t < N_HEADS:
            s[t] = scores(t)
        if t >= 1:
            outs.append(attend(t - 1, s.pop(t - 1)))
    return jnp.concatenate(outs, axis=1).astype(BF16)


def _attn_a_prompt_kernel(sink_ref, q_ref, kp_ref, kc_ref, vp_ref, vc_ref, o_ref, *, blocks_per_seq):
    i = pl.program_id(0)
    has_prev = (i % blocks_per_seq) != 0
    k = jnp.concatenate([kp_ref[...], kc_ref[...]], axis=0).astype(BF16)
    v = jnp.concatenate([vp_ref[...], vc_ref[...]], axis=0).astype(BF16)
    rows, keys = ATTN_A_BLOCK, 2 * ATTN_A_BLOCK
    q_chunk = lax.broadcasted_iota(jnp.int32, (rows, keys), 0) // CHUNK
    col = lax.broadcasted_iota(jnp.int32, (rows, keys), 1)
    k_chunk = col // CHUNK
    valid = (k_chunk >= q_chunk) & (k_chunk <= q_chunk + WINDOW // CHUNK)
    valid = valid & ((col >= ATTN_A_BLOCK) | has_prev)
    o_ref[...] = _sink_attention(q_ref[...], k, v, sink_ref, valid)


def _attn_a_prompt(qkv, sinks, seq):
    n = qkv.shape[0]
    blk = ATTN_A_BLOCK
    blocks_per_seq = seq // blk
    kcol = D_MODEL // A_KV_DIM
    prev = lambda i: jnp.maximum(i - 1, 0)
    return pl.pallas_call(
        functools.partial(_attn_a_prompt_kernel, blocks_per_seq=blocks_per_seq),
        grid=(n // blk,),
        in_specs=[pl.BlockSpec(memory_space=pltpu.SMEM),
                  pl.BlockSpec((blk, D_MODEL), lambda i: (i, 0)),
                  pl.BlockSpec((blk, A_KV_DIM), lambda i: (prev(i), kcol)),
                  pl.BlockSpec((blk, A_KV_DIM), lambda i: (i, kcol)),
                  pl.BlockSpec((blk, A_KV_DIM), lambda i: (prev(i), kcol + 1)),
                  pl.BlockSpec((blk, A_KV_DIM), lambda i: (i, kcol + 1))],
        out_specs=pl.BlockSpec((blk, D_MODEL), lambda i: (i, 0)),
        out_shape=jax.ShapeDtypeStruct((n, D_MODEL), BF16),
        compiler_params=_params(1),
        name="attn_a_prompt",
    )(sinks, qkv, qkv, qkv, qkv, qkv)


def _attn_a_sample_kernel(sink_ref, q_ref, kn_ref, vn_ref, ck_ref, cv_ref, o_ref):
    k = jnp.concatenate([ck_ref[0], kn_ref[...]], axis=0).astype(BF16)
    v = jnp.concatenate([cv_ref[0], vn_ref[...]], axis=0).astype(BF16)
    o_ref[...] = _sink_attention(q_ref[...], k, v, sink_ref, None)


def _attn_a_sample(qkv, cache_k, cache_v, sinks):
    streams = cache_k.shape[0]
    t = qkv.shape[0] // streams
    kcol = D_MODEL // A_KV_DIM
    return pl.pallas_call(
        _attn_a_sample_kernel,
        grid=(streams,),
        in_specs=[pl.BlockSpec(memory_space=pltpu.SMEM),
                  pl.BlockSpec((t, D_MODEL), lambda b: (b, 0)),
                  pl.BlockSpec((t, A_KV_DIM), lambda b: (b, kcol)),
                  pl.BlockSpec((t, A_KV_DIM), lambda b: (b, kcol + 1)),
                  pl.BlockSpec((1, WINDOW, A_KV_DIM), lambda b: (b, 0, 0)),
                  pl.BlockSpec((1, WINDOW, A_KV_DIM), lambda b: (b, 0, 0))],
        out_specs=pl.BlockSpec((t, D_MODEL), lambda b: (b, 0)),
        out_shape=jax.ShapeDtypeStruct((qkv.shape[0], D_MODEL), BF16),
        compiler_params=_params(1),
        name="attn_a_sample",
    )(sinks, qkv, qkv, qkv, cache_k, cache_v)


def _proj_b_kernel(x_ref, g_ref, w_ref, q_ref, k_ref, v_ref, kb_ref, vb_ref):
    xn = _rmsnorm(x_ref[...], g_ref[...]).astype(BF16)
    for j in range(D_MODEL // LANES):
        cols = slice(j * LANES, (j + 1) * LANES)
        q = jnp.dot(xn, w_ref[:, cols], preferred_element_type=F32)
        q_ref[:, cols] = (q * SCALE).astype(BF16)
        k = jnp.dot(xn, w_ref[:, D_MODEL + j * LANES:D_MODEL + (j + 1) * LANES], preferred_element_type=F32)
        kb_ref[:, cols] = k.astype(BF16)
        v = jnp.dot(xn, w_ref[:, 2 * D_MODEL + j * LANES:2 * D_MODEL + (j + 1) * LANES],
                    preferred_element_type=F32)
        vb_ref[:, cols] = v.astype(BF16)
        for half in range(LANES // HEAD_DIM):
            h = j * (LANES // HEAD_DIM) + half
            k_ref[:, h, :] = k[:, half * HEAD_DIM:(half + 1) * HEAD_DIM]
            v_ref[:, h, :] = v[:, half * HEAD_DIM:(half + 1) * HEAD_DIM]


def _proj_b(x, g, w_bf):
    n = x.shape[0]
    tm = min(ROW_TILE, n)
    row = lambda i: (i, 0)
    return pl.pallas_call(
        _proj_b_kernel,
        grid=(n // tm,),
        in_specs=[pl.BlockSpec((tm, D_MODEL), row),
                  pl.BlockSpec((1, D_MODEL), lambda i: (0, 0)),
                  pl.BlockSpec((D_MODEL, 3 * D_MODEL), lambda i: (0, 0))],
        out_specs=[pl.BlockSpec((tm, D_MODEL), row),
                   pl.BlockSpec((tm, N_HEADS, HEAD_DIM), lambda i: (i, 0, 0)),
                   pl.BlockSpec((tm, N_HEADS, HEAD_DIM), lambda i: (i, 0, 0)),
                   pl.BlockSpec((tm, D_MODEL), row),
                   pl.BlockSpec((tm, D_MODEL), row)],
        out_shape=[jax.ShapeDtypeStruct((n, D_MODEL), BF16),
                   jax.ShapeDtypeStruct((n, N_HEADS, HEAD_DIM), F32),
                   jax.ShapeDtypeStruct((n, N_HEADS, HEAD_DIM), F32),
                   jax.ShapeDtypeStruct((n, D_MODEL), BF16),
                   jax.ShapeDtypeStruct((n, D_MODEL), BF16)],
        compiler_params=_params(1),
        name="proj_b",
    )(x, g, w_bf)


def _proj_bt_kernel(x_ref, g_ref, wq_ref, wkt_ref, wvt_ref, q_ref, kt_ref, vt_ref, ktb_ref, vtb_ref):
    xn = _rmsnorm(x_ref[...], g_ref[...]).astype(BF16)
    q = jnp.dot(xn, wq_ref[...], preferred_element_type=F32)
    q_ref[...] = (q * SCALE).astype(BF16)
    kt = lax.dot_general(wkt_ref[...], xn, _CONTRACT_LAST, preferred_element_type=F32)
    kt_ref[...] = kt
    ktb_ref[...] = kt.astype(BF16)
    vt = lax.dot_general(wvt_ref[...], xn, _CONTRACT_LAST, preferred_element_type=F32)
    vt_ref[...] = vt
    vtb_ref[...] = vt.astype(BF16)


def _proj_bt(x, g, wq_bf, wkt_bf, wvt_bf, seq):
    n = x.shape[0]
    tm = min(ROW_TILE, seq)
    tiles = seq // tm
    row = lambda i: (i, 0)
    fixed = lambda i: (0, 0)
    col = lambda i: (i // tiles, 0, i % tiles)
    feature_major = lambda dtype: jax.ShapeDtypeStruct((n // seq, D_MODEL, seq), dtype)
    return pl.pallas_call(
        _proj_bt_kernel,
        grid=(n // tm,),
        in_specs=[pl.BlockSpec((tm, D_MODEL), row),
                  pl.BlockSpec((1, D_MODEL), fixed),
                  pl.BlockSpec((D_MODEL, D_MODEL), fixed),
                  pl.BlockSpec((D_MODEL, D_MODEL), fixed),
                  pl.BlockSpec((D_MODEL, D_MODEL), fixed)],
        out_specs=[pl.BlockSpec((tm, D_MODEL), row)] + [pl.BlockSpec((None, D_MODEL, tm), col)] * 4,
        out_shape=[jax.ShapeDtypeStruct((n, D_MODEL), BF16),
                   feature_major(F32), feature_major(F32), feature_major(BF16), feature_major(BF16)],
        compiler_params=_params(1),
        name="proj_bt",
    )(x, g, wq_bf, wkt_bf, wvt_bf)


def _strict_lower_ones(n):
    j = lax.broadcasted_iota(jnp.int32, (2 * n, n), 0) % n
    s = lax.broadcasted_iota(jnp.int32, (2 * n, n), 1)
    return jnp.where(j > s, 1.0, 0.0).astype(BF16)


_CONTRACT_LAST = (((1,), (1,)), ((), ()))


def _sb_scores(qh, kh, transposed):
    if transposed:
        return jnp.dot(qh, kh, preferred_element_type=F32)
    return lax.dot_general(qh, kh, _CONTRACT_LAST, preferred_element_type=F32)


def _sb_suffix(z, tri, before):
    softplus = jnp.maximum(z, 0.0) + jnp.log(1.0 + jnp.exp(-jnp.abs(z)))
    log_beta = z - softplus
    if before is not None:
        softplus = jnp.where(before, softplus, 0.0)
    hi = softplus.astype(BF16)
    lo = (softplus - hi.astype(F32)).astype(BF16)
    later = jnp.dot(jnp.concatenate([hi, lo], axis=1), tri, preferred_element_type=F32)
    return log_beta, later, later[:, 0:1] + softplus[:, 0:1]


def _sb_values(log_beta, later, vh, before, transposed):
    a = jnp.exp(log_beta - later)
    if before is not None:
        a = jnp.where(before, a, 0.0)
    if transposed:
        return lax.dot_general(a.astype(BF16), vh, _CONTRACT_LAST, preferred_element_type=F32)
    return jnp.dot(a.astype(BF16), vh, preferred_element_type=F32)


def _head_cols(ref):
    return lambda h: ref[:, h * HEAD_DIM:(h + 1) * HEAD_DIM].astype(BF16)


def _head_rows(ref):
    return lambda h: ref[h * HEAD_DIM:(h + 1) * HEAD_DIM, :].astype(BF16)


def _sb_block(q_ref, k_head, v_head, tri, carry_ref, acc_ref, before, transposed):
    first_head = lax.broadcasted_iota(jnp.int32, carry_ref.shape[1:], 1) < HEAD_DIM
    q_head = _head_cols(q_ref)
    carries = [carry_ref[pair] for pair in range(N_HEADS // 2)]
    z, mid, outs, masses = {}, {}, {}, {}
    for t in range(N_HEADS + 2 * SB_STAGE_SKEW):
        if t < N_HEADS:
            z[t] = _sb_scores(q_head(t), k_head(t), transposed)
        h = t - SB_STAGE_SKEW
        if 0 <= h < N_HEADS:
            log_beta, later, masses[h] = _sb_suffix(z.pop(h), tri, before)
            mid[h] = (log_beta, later)
        h = t - 2 * SB_STAGE_SKEW
        if 0 <= h < N_HEADS:
            outs[h] = _sb_values(*mid.pop(h), v_head(h), before, transposed)
    adds = []
    for pair in range(N_HEADS // 2):
        h = 2 * pair
        adds.append(jnp.exp(-carries[pair]) * jnp.concatenate([outs[h], outs[h + 1]], axis=1))
        carries[pair] = carries[pair] + jnp.where(first_head, masses[h], masses[h + 1])
    acc_ref[...] += jnp.concatenate(adds, axis=1)
    least = carries[0]
    for pair in range(N_HEADS // 2):
        carry_ref[pair] = carries[pair]
        least = jnp.minimum(least, carries[pair])
    return (jnp.min(least) <= SB_DECAY_LIMIT).astype(jnp.int32)


def _sb_old_blocks(n_old, fetch, block, live):
    @pl.when(n_old > 0)
    def _():
        for copy in fetch(1, 0):
            copy.start()

    def cond(state):
        j, live = state
        return jnp.logical_and(j <= n_old, live == 1)

    def body(state):
        j, _ = state
        slot = (j - 1) % 2
        for copy in fetch(j, slot):
            copy.wait()

        @pl.when(j < n_old)
        def _():
            for copy in fetch(j + 1, 1 - slot):
                copy.start()

        return j + 1, block(slot)

    j_end, _ = lax.while_loop(cond, body, (jnp.int32(1), live))

    @pl.when(j_end <= n_old)
    def _():
        for copy in fetch(j_end, (j_end - 1) % 2):
            copy.wait()


def _sb_prompt_kernel(q_ref, kd_ref, vd_ref, k_hbm, v_hbm, o_ref, kbuf, vbuf, sem, acc_ref, carry_ref,
                      *, blocks_per_seq):
    i = pl.program_id(0)
    n = SB_BLOCK
    tri = _strict_lower_ones(n)
    b = i // blocks_per_seq
    qi = i % blocks_per_seq

    def fetch(j, slot):
        cols = pl.ds((qi - j) * n, n)
        return (pltpu.make_async_copy(k_hbm.at[b, :, cols], kbuf.at[slot], sem.at[0, slot]),
                pltpu.make_async_copy(v_hbm.at[b, :, cols], vbuf.at[slot], sem.at[1, slot]))

    def old_block(slot):
        return _sb_block(q_ref, _head_rows(kbuf.at[slot]), _head_rows(vbuf.at[slot]), tri, carry_ref, acc_ref,
                         None, True)

    acc_ref[...] = jnp.zeros_like(acc_ref)
    carry_ref[...] = jnp.zeros_like(carry_ref)
    row = lax.broadcasted_iota(jnp.int32, (n, n), 0)
    col = lax.broadcasted_iota(jnp.int32, (n, n), 1)
    live = _sb_block(q_ref, _head_rows(kd_ref), _head_rows(vd_ref), tri, carry_ref, acc_ref, col < row, True)
    _sb_old_blocks(qi, fetch, old_block, live)
    o_ref[...] = acc_ref[...].astype(BF16)


def _attn_b_prompt(q_bf, kt_bf, vt_bf):
    n = q_bf.shape[0]
    seq = kt_bf.shape[2]
    blk = SB_BLOCK
    bps = seq // blk
    row = lambda i: (i, 0)
    diag = lambda i: (i // bps, 0, i % bps)
    return pl.pallas_call(
        functools.partial(_sb_prompt_kernel, blocks_per_seq=bps),
        grid=(n // blk,),
        in_specs=[pl.BlockSpec((blk, D_MODEL), row),
                  pl.BlockSpec((None, D_MODEL, blk), diag),
                  pl.BlockSpec((None, D_MODEL, blk), diag),
                  pl.BlockSpec(memory_space=pl.ANY),
                  pl.BlockSpec(memory_space=pl.ANY)],
        out_specs=pl.BlockSpec((blk, D_MODEL), row),
        out_shape=jax.ShapeDtypeStruct((n, D_MODEL), BF16),
        scratch_shapes=[pltpu.VMEM((2, D_MODEL, blk), BF16),
                        pltpu.VMEM((2, D_MODEL, blk), BF16),
                        pltpu.SemaphoreType.DMA((2, 2)),
                        pltpu.VMEM((blk, D_MODEL), F32),
                        pltpu.VMEM((N_HEADS // 2, blk, LANES), F32)],
        compiler_params=_params(1),
        name="attn_b_prompt",
    )(q_bf, kt_bf, vt_bf, kt_bf, vt_bf)


def _sb_sample_kernel(q_ref, kn_ref, vn_ref, ck_hbm, cv_hbm, o_ref, kbuf, vbuf, sem, acc_ref, carry_ref,
                      *, n_cache):
    b = pl.program_id(0)
    t = q_ref.shape[0]
    n = SB_BLOCK
    tri = _strict_lower_ones(n)

    def fetch(j, slot):
        cols = pl.ds((n_cache - j) * n, n)
        return (pltpu.make_async_copy(ck_hbm.at[b, :, cols], kbuf.at[slot], sem.at[0, slot]),
                pltpu.make_async_copy(cv_hbm.at[b, :, cols], vbuf.at[slot], sem.at[1, slot]))

    def old_block(slot):
        return _sb_block(q_ref, _head_rows(kbuf.at[slot]), _head_rows(vbuf.at[slot]), tri, carry_ref, acc_ref,
                         None, True)

    acc_ref[...] = jnp.zeros_like(acc_ref)
    carry_ref[...] = jnp.zeros_like(carry_ref)
    row = lax.broadcasted_iota(jnp.int32, (t, t), 0)
    col = lax.broadcasted_iota(jnp.int32, (t, t), 1)
    live = _sb_block(q_ref, _head_cols(kn_ref), _head_cols(vn_ref), _strict_lower_ones(t), carry_ref, acc_ref,
                     col < row, False)
    _sb_old_blocks(jnp.int32(n_cache), fetch, old_block, live)
    o_ref[...] = acc_ref[...].astype(BF16)


def _attn_b_sample(q_bf, k_bf, v_bf, cache_k, cache_v):
    streams, _, past = cache_k.shape
    t = q_bf.shape[0] // streams
    blk = SB_BLOCK
    new = lambda b: (b, 0)
    return pl.pallas_call(
        functools.partial(_sb_sample_kernel, n_cache=past // blk),
        grid=(streams,),
        in_specs=[pl.BlockSpec((t, D_MODEL), new),
                  pl.BlockSpec((t, D_MODEL), new),
                  pl.BlockSpec((t, D_MODEL), new),
                  pl.BlockSpec(memory_space=pl.ANY),
                  pl.BlockSpec(memory_space=pl.ANY)],
        out_specs=pl.BlockSpec((t, D_MODEL), new),
        out_shape=jax.ShapeDtypeStruct(q_bf.shape, BF16),
        scratch_shapes=[pltpu.VMEM((2, D_MODEL, blk), F32),
                        pltpu.VMEM((2, D_MODEL, blk), F32),
                        pltpu.SemaphoreType.DMA((2, 2)),
                        pltpu.VMEM((t, D_MODEL), F32),
                        pltpu.VMEM((N_HEADS // 2, t, LANES), F32)],
        compiler_params=_params(1),
        name="attn_b_sample",
    )(q_bf, k_bf, v_bf, cache_k, cache_v)


ROUTE_ROWS = 48


def _oproj_route_kernel(o_ref, wo_ref, h_ref, g_ref, wr_hi_ref, wr_lo_ref, base_ref,
                        h_out_ref, xn_ref, route_ref, count_ref):
    @pl.when(pl.program_id(0) == 0)
    def _():
        count_ref[...] = base_ref[...]

    h = h_ref[...] + jnp.dot(o_ref[...], wo_ref[...], preferred_element_type=F32)
    h_out_ref[...] = h
    xn = _rmsnorm(h, g_ref[...])
    x_hi = xn.astype(BF16)
    bits = lax.bitcast_convert_type(x_hi.astype(F32), jnp.uint32)
    half = D_MODEL // 2
    xn_ref[...] = (bits[:, :half] >> 16) | bits[:, half:]
    x_lo = (xn - x_hi.astype(F32)).astype(BF16)
    logits = (lax.dot_general(wr_hi_ref[...], x_hi, _CONTRACT_LAST, preferred_element_type=F32)
              + lax.dot_general(wr_lo_ref[...], x_hi, _CONTRACT_LAST, preferred_element_type=F32)
              + lax.dot_general(wr_hi_ref[...], x_lo, _CONTRACT_LAST, preferred_element_type=F32))
    tm = logits.shape[1]
    sub = EXPERTS_PER_GROUP
    row = lax.broadcasted_iota(jnp.int32, (sub, tm), 0)

    def top1(vals):
        peak = jnp.max(vals, axis=0, keepdims=True)
        return peak, jnp.min(jnp.where(vals == peak, row, sub), axis=0, keepdims=True)

    g_mask = row < N_GROUPS
    g_logits = jnp.where(g_mask, logits[:sub], -jnp.inf)
    g_max, g_idx = top1(g_logits)
    g_sum = jnp.sum(jnp.where(g_mask, jnp.exp(g_logits - g_max), 0.0), axis=0, keepdims=True)
    g_p = 1.0 / g_sum

    e_logits = logits[sub:2 * sub]
    for g in range(1, N_GROUPS):
        e_logits = jnp.where(g_idx == g, logits[(g + 1) * sub:(g + 2) * sub], e_logits)
    m1, i1 = top1(e_logits)
    rest = jnp.where(row == i1, -jnp.inf, e_logits)
    m2, i2 = top1(rest)
    e_sum = jnp.sum(jnp.exp(e_logits - m1), axis=0, keepdims=True)
    p1 = 1.0 / e_sum
    p2 = jnp.exp(m2 - m1) / e_sum
    top_sum = p1 + p2
    w1 = g_p * p1 / top_sum
    w2 = g_p * p2 / top_sum
    e1 = g_idx * sub + i1
    e2 = g_idx * sub + i2
    expert = lax.broadcasted_iota(jnp.int32, (N_EXPERTS, tm), 0)
    pick1 = expert == e1
    pick2 = expert == e2
    picks = jnp.where(pick1 | pick2, 1.0, 0.0)
    earlier = jnp.where(lax.broadcasted_iota(jnp.int32, (tm, tm), 0) < lax.broadcasted_iota(jnp.int32, (tm, tm), 1),
                        1.0, 0.0).astype(BF16)
    counts = count_ref[...]
    before = jnp.dot(picks.astype(BF16), earlier, preferred_element_type=F32) + counts[:, 0:1]
    rank1 = jnp.sum(jnp.where(pick1, before, 0.0), axis=0, keepdims=True)
    rank2 = jnp.sum(jnp.where(pick2, before, 0.0), axis=0, keepdims=True)
    count_ref[...] = counts + jnp.sum(picks, axis=1, keepdims=True)
    vals = (e1.astype(F32), e2.astype(F32), w1, w2, rank1, rank2)
    route = jnp.zeros((sub, tm), F32)
    for c, val in enumerate(vals):
        route = jnp.where(row == c, val, route)
    route_ref[...] = route


def _oproj_route(o_bf, wo_bf, h, g, wr_hi, wr_lo, base_counts):
    n = h.shape[0]
    tm = min(ROW_TILE, n)
    row = lambda i: (i, 0)
    fixed = lambda i: (0, 0)
    return pl.pallas_call(
        _oproj_route_kernel,
        grid=(n // tm,),
        in_specs=[pl.BlockSpec((tm, D_MODEL), row),
                  pl.BlockSpec((D_MODEL, D_MODEL), fixed),
                  pl.BlockSpec((tm, D_MODEL), row),
                  pl.BlockSpec((1, D_MODEL), fixed),
                  pl.BlockSpec((ROUTE_ROWS, D_MODEL), fixed),
                  pl.BlockSpec((ROUTE_ROWS, D_MODEL), fixed),
                  pl.BlockSpec((N_EXPERTS, LANES), fixed)],
        out_specs=[pl.BlockSpec((tm, D_MODEL), row),
                   pl.BlockSpec((tm, D_MODEL // 2), row),
                   pl.BlockSpec((EXPERTS_PER_GROUP, tm), lambda i: (0, i)),
                   pl.BlockSpec((N_EXPERTS, LANES), fixed)],
        out_shape=[jax.ShapeDtypeStruct((n, D_MODEL), F32),
                   jax.ShapeDtypeStruct((n, D_MODEL // 2), jnp.uint32),
                   jax.ShapeDtypeStruct((EXPERTS_PER_GROUP, n), F32),
                   jax.ShapeDtypeStruct((N_EXPERTS, LANES), F32)],
        compiler_params=_params(1),
        name="oproj_route",
    )(o_bf, wo_bf, h, g, wr_hi, wr_lo, base_counts)


def _dispatch_kernel(dest_ref, xn_ref, buf_in_ref, buf_ref, sem):
    del buf_in_ref
    i = pl.program_id(0)
    tm = xn_ref.shape[0]
    base = i * (tm * TOP_K)

    def row_copy(t, slot):
        return pltpu.make_async_copy(xn_ref.at[pl.ds(t, 1)], buf_ref.at[pl.ds(slot, 1)], sem)

    def issue(t, carry):
        for k in range(TOP_K):
            row_copy(t, dest_ref[base + TOP_K * t + k]).start(priority=k % 2)
        return carry

    lax.fori_loop(0, tm, issue, 0, unroll=8)
    for k in range(TOP_K):
        pltpu.make_async_copy(xn_ref, buf_ref.at[pl.ds(0, tm)], sem).wait()


def _dispatch(dest, xn, buf):
    n = xn.shape[0]
    tm = min(DISPATCH_TILE, n)
    grid_spec = pltpu.PrefetchScalarGridSpec(
        num_scalar_prefetch=1,
        grid=(n // tm,),
        in_specs=[pl.BlockSpec((tm, xn.shape[1]), lambda i, d: (i, 0)),
                  pl.BlockSpec(memory_space=pl.ANY)],
        out_specs=pl.BlockSpec(memory_space=pl.ANY),
        scratch_shapes=[pltpu.SemaphoreType.DMA(())])
    return pl.pallas_call(
        _dispatch_kernel,
        grid_spec=grid_spec,
        out_shape=jax.ShapeDtypeStruct(buf.shape, buf.dtype),
        input_output_aliases={2: 0},
        compiler_params=_params(1, disable_bounds_checks=True, has_side_effects=True),
        name="moe_dispatch",
    )(dest, xn, buf)


def _expert_kernel(be_ref, nused_ref, x_ref, wg_ref, wu_ref, wd_ref, o_ref, wg_bf, wu_bf, wd_bf):
    i = pl.program_id(0)
    used = i < nused_ref[0]

    @pl.when(used)
    def _():
        @pl.when(jnp.logical_or(i == 0, be_ref[i] != be_ref[jnp.maximum(i - 1, 0)]))
        def _():
            wg_bf[...] = wg_ref[0].astype(BF16)
            wu_bf[...] = wu_ref[0].astype(BF16)
            wd_bf[...] = wd_ref[0].astype(BF16)

        words = x_ref[...]
        low = lax.bitcast_convert_type(words << 16, F32)
        high = lax.bitcast_convert_type(words & jnp.uint32(0xFFFF0000), F32)
        x = jnp.concatenate([low, high], axis=1).astype(BF16)
        a = jnp.dot(x, wg_bf[...], preferred_element_type=F32)
        b = jnp.dot(x, wu_bf[...], preferred_element_type=F32)
        mid = (a * (1.0 / (1.0 + jnp.exp(-a))) * b).astype(BF16)
        o_ref[...] = jnp.dot(mid, wd_bf[...], preferred_element_type=F32)

    @pl.when(jnp.logical_not(used))
    def _():
        o_ref[...] = jnp.zeros_like(o_ref)


def _experts(block_e, n_used, buf, w_gate, w_up, w_down, layer):
    bm = EXPERT_BLOCK
    n_blocks = buf.shape[0] // bm
    blk = lambda i, be, nu: (jnp.minimum(i, nu[0] - 1), 0)
    wsel = lambda i, be, nu: (layer, be[jnp.minimum(i, nu[0] - 1)], 0, 0)
    grid_spec = pltpu.PrefetchScalarGridSpec(
        num_scalar_prefetch=2,
        grid=(n_blocks,),
        in_specs=[pl.BlockSpec((bm, buf.shape[1]), blk),
                  pl.BlockSpec((None, 1, D_MODEL, D_EXPERT), wsel),
                  pl.BlockSpec((None, 1, D_MODEL, D_EXPERT), wsel),
                  pl.BlockSpec((None, 1, D_EXPERT, D_MODEL), wsel)],
        out_specs=pl.BlockSpec((bm, D_MODEL), lambda i, be, nu: (i, 0)),
        scratch_shapes=[pltpu.VMEM((D_MODEL, D_EXPERT), BF16),
                        pltpu.VMEM((D_MODEL, D_EXPERT), BF16),
                        pltpu.VMEM((D_EXPERT, D_MODEL), BF16)])
    return pl.pallas_call(
        _expert_kernel,
        grid_spec=grid_spec,
        out_shape=jax.ShapeDtypeStruct((buf.shape[0], D_MODEL), F32),
        compiler_params=_params(1),
        name="moe_experts",
    )(block_e, n_used, buf, w_gate, w_up, w_down)


def _combine_kernel(dest_ref, h_ref, gate_ref, g_ref, yb_ref, o_ref, rows_ref, sem, *, final_norm):
    i = pl.program_id(0)
    n_steps = pl.num_programs(0)
    tm = h_ref.shape[0]

    def row_copy(step, t, k):
        slot = step % 2
        src = dest_ref[step * (tm * TOP_K) + TOP_K * t + k]
        return pltpu.make_async_copy(yb_ref.at[pl.ds(src, 1)], rows_ref.at[slot, k, pl.ds(t, 1)], sem.at[slot])

    def issue(step):
        def body(t, carry):
            for k in range(TOP_K):
                row_copy(step, t, k).start(priority=k % 2)
            return carry
        lax.fori_loop(0, tm, body, 0, unroll=8)

    @pl.when(i == 0)
    def _():
        issue(i)

    @pl.when(i + 1 < n_steps)
    def _():
        issue(i + 1)

    slot = i % 2
    for k in range(TOP_K):
        pltpu.make_async_copy(yb_ref.at[pl.ds(0, tm)], rows_ref.at[slot, k], sem.at[slot]).wait()

    gates = gate_ref[...]
    out = h_ref[...] + (rows_ref[slot, 0] * gates[:, 0:1] + rows_ref[slot, 1] * gates[:, 1:2])
    if final_norm:
        out = _rmsnorm(out, g_ref[...])
    o_ref[...] = out


def _combine(dest, h, gates, g, yb, final_norm):
    n = h.shape[0]
    tm = min(COMBINE_TILE, n)
    grid_spec = pltpu.PrefetchScalarGridSpec(
        num_scalar_prefetch=1,
        grid=(n // tm,),
        in_specs=[pl.BlockSpec((tm, D_MODEL), lambda i, d: (i, 0)),
                  pl.BlockSpec((tm, TOP_K), lambda i, d: (i, 0)),
                  pl.BlockSpec((1, D_MODEL), lambda i, d: (0, 0)),
                  pl.BlockSpec(memory_space=pl.ANY)],
        out_specs=pl.BlockSpec((tm, D_MODEL), lambda i, d: (i, 0)),
        scratch_shapes=[pltpu.VMEM((2, TOP_K, tm, D_MODEL), F32),
                        pltpu.SemaphoreType.DMA((2,))])
    return pl.pallas_call(
        functools.partial(_combine_kernel, final_norm=final_norm),
        grid_spec=grid_spec,
        out_shape=jax.ShapeDtypeStruct(h.shape, F32),
        compiler_params=_params(1, disable_bounds_checks=True),
        name="moe_combine",
    )(dest, h, gates, g, yb)


def _route_plan(route_p, route_s, counts):
    route = jnp.concatenate([route_p, route_s], axis=1)
    flat_e = route[:TOP_K].T.astype(jnp.int32).reshape(-1)
    rank = route[2 * TOP_K:3 * TOP_K].T.astype(jnp.int32).reshape(-1)
    n_assign = flat_e.shape[0]
    counts = counts[:, 0].astype(jnp.int32)
    bm = EXPERT_BLOCK
    padded = (counts + bm - 1) // bm * bm
    pends = jnp.cumsum(padded)
    onehot = flat_e[:, None] == jnp.arange(N_EXPERTS, dtype=jnp.int32)[None, :]
    dest = jnp.sum(jnp.where(onehot, (pends - padded)[None, :], 0), axis=1) + rank
    n_blocks = -(-n_assign // bm) + N_EXPERTS
    first_rows = jnp.arange(n_blocks, dtype=jnp.int32) * bm
    block_e = jnp.minimum(jnp.sum((pends[None, :] <= first_rows[:, None]).astype(jnp.int32), axis=1), N_EXPERTS - 1)
    n_used = (pends[-1] // bm).astype(jnp.int32).reshape(1)
    return dest.astype(jnp.int32), block_e, n_used, n_blocks


def _moe(hp, hs, route_p, route_s, counts, xn_p, xn_s, g_final, w_gate, w_up, w_down, layer, final_norm):
    dest, block_e, n_used, n_blocks = _route_plan(route_p, route_s, counts)
    n_p = hp.shape[0] * TOP_K
    dest_p, dest_s = dest[:n_p], dest[n_p:]
    buf = jnp.zeros((n_blocks * EXPERT_BLOCK, xn_p.shape[1]), xn_p.dtype)
    buf = _dispatch(dest_p, xn_p, buf)
    buf = _dispatch(dest_s, xn_s, buf)
    yb = _experts(block_e, n_used, buf, w_gate, w_up, w_down, layer)
    out_p = _combine(dest_p, hp, route_p[TOP_K:2 * TOP_K].T, g_final, yb, final_norm)
    out_s = _combine(dest_s, hs, route_s[TOP_K:2 * TOP_K].T, g_final, yb, final_norm)
    return out_p, out_s


def _router_weights(w_group, w_router):
    w_exp = jnp.transpose(w_router, (0, 2, 1)).reshape(N_EXPERTS, D_MODEL)
    zeros = lambda rows: jnp.zeros((rows, D_MODEL), F32)
    w = jnp.concatenate([w_group.T, zeros(EXPERTS_PER_GROUP - N_GROUPS), w_exp,
                         zeros(ROUTE_ROWS - EXPERTS_PER_GROUP - N_EXPERTS)], axis=0)
    hi = w.astype(BF16)
    lo = (w - hi.astype(F32)).astype(BF16)
    return hi, lo


def kernel(x_prompt, x_sample, cache_a_k, cache_a_v, cache_b_k, cache_b_v, norm_mix, norm_ffn, norm_final,
           a_w_qkv, a_w_o, a_sinks, b_w_qkv, b_w_o, moe_w_group, moe_w_router, moe_w_gate, moe_w_up, moe_w_down):
    batch, seq, _ = x_prompt.shape
    streams, t_new, _ = x_sample.shape
    past = cache_b_k.shape[2]
    hp = x_prompt.reshape(batch * seq, D_MODEL)
    hs = x_sample.reshape(streams * t_new, D_MODEL)
    g_final = norm_final.reshape(1, D_MODEL)

    cs_p = _rope_table(jnp.arange(seq, dtype=jnp.int32))
    cs_s = _rope_table(jnp.tile(past + jnp.arange(t_new, dtype=jnp.int32), streams))

    def moe_layer(i, hp, hs, op, os_, w_o, final_norm):
        wo_bf = w_o.astype(BF16)
        g = norm_ffn[i].reshape(1, D_MODEL)
        wr_hi, wr_lo = _router_weights(moe_w_group[i], moe_w_router[i])
        hp, xn_p, route_p, counts = _oproj_route(op, wo_bf, hp, g, wr_hi, wr_lo, jnp.zeros((N_EXPERTS, LANES), F32))
        hs, xn_s, route_s, counts = _oproj_route(os_, wo_bf, hs, g, wr_hi, wr_lo, counts)
        return _moe(hp, hs, route_p, route_s, counts, xn_p, xn_s, g_final,
                    moe_w_gate, moe_w_up, moe_w_down, i, final_norm)

    g0 = norm_mix[0].reshape(1, D_MODEL)
    wa_bf = a_w_qkv[0].astype(BF16)
    qkv_p = _proj_a(hp, g0, wa_bf, cs_p)
    qkv_s = _proj_a(hs, g0, wa_bf, cs_s)
    ck = cache_a_k[0].reshape(streams, WINDOW, A_KV_DIM)
    cv = cache_a_v[0].reshape(streams, WINDOW, A_KV_DIM)
    op = _attn_a_prompt(qkv_p, a_sinks[0], seq)
    os_ = _attn_a_sample(qkv_s, ck, cv, a_sinks[0])
    tail_p = qkv_p.reshape(batch, seq, -1)[:, -WINDOW:, D_MODEL:]
    new_a_k_prompt = tail_p[:, :, :A_KV_DIM].reshape(1, batch, WINDOW, A_KV_HEADS, HEAD_DIM)
    new_a_v_prompt = tail_p[:, :, A_KV_DIM:].reshape(1, batch, WINDOW, A_KV_HEADS, HEAD_DIM)
    k_s = qkv_s[:, D_MODEL:D_MODEL + A_KV_DIM].reshape(streams, t_new, A_KV_HEADS, HEAD_DIM)
    v_s = qkv_s[:, D_MODEL + A_KV_DIM:].reshape(streams, t_new, A_KV_HEADS, HEAD_DIM)
    new_a_k_sample = jnp.concatenate([cache_a_k[0], k_s], axis=1)[None, :, -WINDOW:]
    new_a_v_sample = jnp.concatenate([cache_a_v[0], v_s], axis=1)[None, :, -WINDOW:]
    hp, hs = moe_layer(0, hp, hs, op, os_, a_w_o[0], False)

    g1 = norm_mix[1].reshape(1, D_MODEL)
    wb_bf = b_w_qkv[0].astype(BF16)
    wq_bf = wb_bf[:, :D_MODEL]
    wkt_bf = wb_bf[:, D_MODEL:2 * D_MODEL].T
    wvt_bf = wb_bf[:, 2 * D_MODEL:].T
    q_p, kt_p, vt_p, ktb_p, vtb_p = _proj_bt(hp, g1, wq_bf, wkt_bf, wvt_bf, seq)
    q_s, kf_s, vf_s, kb_s, vb_s = _proj_b(hs, g1, wb_bf)
    op = _attn_b_prompt(q_p, ktb_p, vtb_p)

    def feature_major(cache):
        return jnp.transpose(cache, (0, 1, 3, 4, 2)).reshape(streams, D_MODEL, past)

    def time_major(xt):
        return jnp.transpose(xt.reshape(1, batch, N_HEADS, HEAD_DIM, seq), (0, 1, 4, 2, 3))

    os_ = _attn_b_sample(q_s, kb_s, vb_s, feature_major(cache_b_k), feature_major(cache_b_v))
    new_b_k_prompt = time_major(kt_p)
    new_b_v_prompt = time_major(vt_p)
    new_b_k_sample = kf_s.reshape(1, streams, t_new, N_HEADS, HEAD_DIM)
    new_b_v_sample = vf_s.reshape(1, streams, t_new, N_HEADS, HEAD_DIM)
    hp, hs = moe_layer(1, hp, hs, op, os_, b_w_o[0], True)

    y_prompt = hp.reshape(batch, seq, D_MODEL)
    y_sample = hs.reshape(streams, t_new, D_MODEL)
    return (y_prompt, y_sample, new_a_k_prompt, new_a_v_prompt, new_a_k_sample, new_a_v_sample,
            new_b_k_prompt, new_b_v_prompt, new_b_k_sample, new_b_v_sample)
```

```python
import functools

import jax
import jax.numpy as jnp
import numpy as np
from jax import lax
from jax.experimental import pallas as pl
from jax.experimental.pallas import tpu as pltpu

F32 = jnp.float32
BF16 = jnp.bfloat16

D_MODEL = 1024
HEAD_DIM = 64
N_HEADS = D_MODEL // HEAD_DIM
A_KV_HEADS = 4
A_GROUP = N_HEADS // A_KV_HEADS
A_KV_DIM = A_KV_HEADS * HEAD_DIM
CHUNK = 64
WINDOW = 128
ROT_DIM = HEAD_DIM // 4
ROPE_THETA = 500000.0
N_GROUPS = 4
EXPERTS_PER_GROUP = 8
N_EXPERTS = N_GROUPS * EXPERTS_PER_GROUP
TOP_K = 2
D_EXPERT = D_MODEL // 2
RMS_EPS = 1e-6
NEG_INF = -1e30
SCALE = HEAD_DIM ** -0.5

LANES = 128
MXU_COLS = 256
ROW_TILE = 512
ATTN_A_BLOCK = WINDOW
SB_BLOCK = 256
EXPERT_BLOCK = 512
COMBINE_TILE = 1024
DISPATCH_TILE = 1024
VMEM_LIMIT = 48 * 1024 * 1024
SB_STAGE_SKEW = 1
SB_DECAY_LIMIT = 105.0


def _params(n_axes, **kw):
    return pltpu.CompilerParams(dimension_semantics=("arbitrary",) * n_axes,
                                vmem_limit_bytes=VMEM_LIMIT, **kw)


def _rmsnorm(x, g):
    return x * lax.rsqrt(jnp.mean(x * x, axis=-1, keepdims=True) + RMS_EPS) * g


def _proj_a_kernel(x_ref, g_ref, w_ref, cs_ref, o_ref):
    xn = _rmsnorm(x_ref[...], g_ref[...]).astype(BF16)
    cos = cs_ref[:, :LANES]
    sin = cs_ref[:, LANES:]
    lane = lax.broadcasted_iota(jnp.int32, cos.shape, 1) % HEAD_DIM
    first_half = lane < ROT_DIM // 2
    n_rot = (D_MODEL + A_KV_DIM) // LANES
    wide = MXU_COLS // LANES
    for c in range(o_ref.shape[1] // MXU_COLS):
        both = jnp.dot(xn, w_ref[:, c * MXU_COLS:(c + 1) * MXU_COLS], preferred_element_type=F32)
        for j in range(c * wide, (c + 1) * wide):
            blk = both[:, (j - c * wide) * LANES:(j - c * wide + 1) * LANES]
            if j < n_rot:
                partner = jnp.where(first_half, pltpu.roll(blk, LANES - ROT_DIM // 2, 1),
                                    pltpu.roll(blk, ROT_DIM // 2, 1))
                blk = blk * cos + partner * sin
            o_ref[:, j * LANES:(j + 1) * LANES] = blk


def _rope_table(pos):
    half = ROT_DIM // 2
    inv = ROPE_THETA ** (-jnp.arange(0, ROT_DIM, 2, dtype=F32) / ROT_DIM)
    lane = jnp.arange(LANES, dtype=jnp.int32) % HEAD_DIM
    ang = pos.astype(F32)[:, None] * inv[lane % half][None, :]
    cos = jnp.where(lane < ROT_DIM, jnp.cos(ang), 1.0)
    sin = jnp.sin(ang)
    sin = jnp.where(lane < half, -sin, jnp.where(lane < ROT_DIM, sin, 0.0))
    return jnp.concatenate([cos, sin], axis=1)


def _proj_a(x, g, w_bf, cs):
    n = x.shape[0]
    tm = min(ROW_TILE, n)
    n_out = w_bf.shape[1]
    cs_blocks = cs.shape[0] // tm
    return pl.pallas_call(
        _proj_a_kernel,
        grid=(n // tm,),
        in_specs=[pl.BlockSpec((tm, D_MODEL), lambda i: (i, 0)),
                  pl.BlockSpec((1, D_MODEL), lambda i: (0, 0)),
                  pl.BlockSpec((D_MODEL, n_out), lambda i: (0, 0)),
                  pl.BlockSpec((tm, 2 * LANES), lambda i: (i % cs_blocks, 0))],
        out_specs=pl.BlockSpec((tm, n_out), lambda i: (i, 0)),
        out_shape=jax.ShapeDtypeStruct((n, n_out), F32),
        compiler_params=_params(1),
        name="proj_a",
    )(x, g, w_bf, cs)


def _sink_attention(q, k_bf, v_bf, sink_ref, valid):
    def scores(h):
        g = h // A_GROUP
        qh = (q[:, h * HEAD_DIM:(h + 1) * HEAD_DIM] * SCALE).astype(BF16)
        kh = k_bf[:, g * HEAD_DIM:(g + 1) * HEAD_DIM]
        return lax.dot_general(qh, kh, (((1,), (1,)), ((), ())), preferred_element_type=F32)

    def attend(h, s):
        g = h // A_GROUP
        vh = v_bf[:, g * HEAD_DIM:(g + 1) * HEAD_DIM]
        if valid is not None:
            s = jnp.where(valid, s, NEG_INF)
        sink = sink_ref[h]
        m = jnp.maximum(jnp.max(s, axis=-1, keepdims=True), sink)
        e = jnp.exp(s - m)
        den = jnp.sum(e, axis=-1, keepdims=True) + jnp.exp(sink - m)
        return jnp.dot(e.astype(BF16), vh, preferred_element_type=F32) / den

    s, outs = {}, []
    for t in range(N_HEADS + 1):
        if t < N_HEADS:
            s[t] = scores(t)
        if t >= 1:
            outs.append(attend(t - 1, s.pop(t - 1)))
    return jnp.concatenate(outs, axis=1).astype(BF16)


def _attn_a_prompt_kernel(sink_ref, q_ref, kp_ref, kc_ref, vp_ref, vc_ref, o_ref, *, blocks_per_seq):
    i = pl.program_id(0)
    has_prev = (i % blocks_per_seq) != 0
    k = jnp.concatenate([kp_ref[...], kc_ref[...]], axis=0).astype(BF16)
    v = jnp.concatenate([vp_ref[...], vc_ref[...]], axis=0).astype(BF16)
    rows, keys = ATTN_A_BLOCK, 2 * ATTN_A_BLOCK
    q_chunk = lax.broadcasted_iota(jnp.int32, (rows, keys), 0) // CHUNK
    col = lax.broadcasted_iota(jnp.int32, (rows, keys), 1)
    k_chunk = col // CHUNK
    valid = (k_chunk >= q_chunk) & (k_chunk <= q_chunk + WINDOW // CHUNK)
    valid = valid & ((col >= ATTN_A_BLOCK) | has_prev)
    o_ref[...] = _sink_attention(q_ref[...], k, v, sink_ref, valid)


def _attn_a_prompt(qkv, sinks, seq):
    n = qkv.shape[0]
    blk = ATTN_A_BLOCK
    blocks_per_seq = seq // blk
    kcol = D_MODEL // A_KV_DIM
    prev = lambda i: jnp.maximum(i - 1, 0)
    return pl.pallas_call(
        functools.partial(_attn_a_prompt_kernel, blocks_per_seq=blocks_per_seq),
        grid=(n // blk,),
        in_specs=[pl.BlockSpec(memory_space=pltpu.SMEM),
                  pl.BlockSpec((blk, D_MODEL), lambda i: (i, 0)),
                  pl.BlockSpec((blk, A_KV_DIM), lambda i: (prev(i), kcol)),
                  pl.BlockSpec((blk, A_KV_DIM), lambda i: (i, kcol)),
                  pl.BlockSpec((blk, A_KV_DIM), lambda i: (prev(i), kcol + 1)),
                  pl.BlockSpec((blk, A_KV_DIM), lambda i: (i, kcol + 1))],
        out_specs=pl.BlockSpec((blk, D_MODEL), lambda i: (i, 0)),
        out_shape=jax.ShapeDtypeStruct((n, D_MODEL), BF16),
        compiler_params=_params(1),
        name="attn_a_prompt",
    )(sinks, qkv, qkv, qkv, qkv, qkv)


def _attn_a_sample_kernel(sink_ref, q_ref, kn_ref, vn_ref, ck_ref, cv_ref, o_ref):
    k = jnp.concatenate([ck_ref[0], kn_ref[...]], axis=0).astype(BF16)
    v = jnp.concatenate([cv_ref[0], vn_ref[...]], axis=0).astype(BF16)
    o_ref[...] = _sink_attention(q_ref[...], k, v, sink_ref, None)


def _attn_a_sample(qkv, cache_k, cache_v, sinks):
    streams = cache_k.shape[0]
    t = qkv.shape[0] // streams
    kcol = D_MODEL // A_KV_DIM
    return pl.pallas_call(
        _attn_a_sample_kernel,
        grid=(streams,),
        in_specs=[pl.BlockSpec(memory_space=pltpu.SMEM),
                  pl.BlockSpec((t, D_MODEL), lambda b: (b, 0)),
                  pl.BlockSpec((t, A_KV_DIM), lambda b: (b, kcol)),
                  pl.BlockSpec((t, A_KV_DIM), lambda b: (b, kcol + 1)),
                  pl.BlockSpec((1, WINDOW, A_KV_DIM), lambda b: (b, 0, 0)),
                  pl.BlockSpec((1, WINDOW, A_KV_DIM), lambda b: (b, 0, 0))],
        out_specs=pl.BlockSpec((t, D_MODEL), lambda b: (b, 0)),
        out_shape=jax.ShapeDtypeStruct((qkv.shape[0], D_MODEL), BF16),
        compiler_params=_params(1),
        name="attn_a_sample",
    )(sinks, qkv, qkv, qkv, cache_k, cache_v)


def _proj_b_kernel(x_ref, g_ref, w_ref, q_ref, k_ref, v_ref, kb_ref, vb_ref):
    xn = _rmsnorm(x_ref[...], g_ref[...]).astype(BF16)
    for j in range(D_MODEL // LANES):
        cols = slice(j * LANES, (j + 1) * LANES)
        q = jnp.dot(xn, w_ref[:, cols], preferred_element_type=F32)
        q_ref[:, cols] = (q * SCALE).astype(BF16)
        k = jnp.dot(xn, w_ref[:, D_MODEL + j * LANES:D_MODEL + (j + 1) * LANES], preferred_element_type=F32)
        kb_ref[:, cols] = k.astype(BF16)
        v = jnp.dot(xn, w_ref[:, 2 * D_MODEL + j * LANES:2 * D_MODEL + (j + 1) * LANES],
                    preferred_element_type=F32)
        vb_ref[:, cols] = v.astype(BF16)
        for half in range(LANES // HEAD_DIM):
            h = j * (LANES // HEAD_DIM) + half
            k_ref[:, h, :] = k[:, half * HEAD_DIM:(half + 1) * HEAD_DIM]
            v_ref[:, h, :] = v[:, half * HEAD_DIM:(half + 1) * HEAD_DIM]


def _proj_b(x, g, w_bf):
    n = x.shape[0]
    tm = min(ROW_TILE, n)
    row = lambda i: (i, 0)
    return pl.pallas_call(
        _proj_b_kernel,
        grid=(n // tm,),
        in_specs=[pl.BlockSpec((tm, D_MODEL), row),
                  pl.BlockSpec((1, D_MODEL), lambda i: (0, 0)),
                  pl.BlockSpec((D_MODEL, 3 * D_MODEL), lambda i: (0, 0))],
        out_specs=[pl.BlockSpec((tm, D_MODEL), row),
                   pl.BlockSpec((tm, N_HEADS, HEAD_DIM), lambda i: (i, 0, 0)),
                   pl.BlockSpec((tm, N_HEADS, HEAD_DIM), lambda i: (i, 0, 0)),
                   pl.BlockSpec((tm, D_MODEL), row),
                   pl.BlockSpec((tm, D_MODEL), row)],
        out_shape=[jax.ShapeDtypeStruct((n, D_MODEL), BF16),
                   jax.ShapeDtypeStruct((n, N_HEADS, HEAD_DIM), F32),
                   jax.ShapeDtypeStruct((n, N_HEADS, HEAD_DIM), F32),
                   jax.ShapeDtypeStruct((n, D_MODEL), BF16),
                   jax.ShapeDtypeStruct((n, D_MODEL), BF16)],
        compiler_params=_params(1),
        name="proj_b",
    )(x, g, w_bf)


def _proj_bt_kernel(x_ref, g_ref, wq_ref, wkt_ref, wvt_ref, q_ref, kt_ref, vt_ref, ktb_ref, vtb_ref):
    xn = _rmsnorm(x_ref[...], g_ref[...]).astype(BF16)
    q = jnp.dot(xn, wq_ref[...], preferred_element_type=F32)
    q_ref[...] = (q * SCALE).astype(BF16)
    kt = lax.dot_general(wkt_ref[...], xn, _CONTRACT_LAST, preferred_element_type=F32)
    kt_ref[...] = kt
    ktb_ref[...] = kt.astype(BF16)
    vt = lax.dot_general(wvt_ref[...], xn, _CONTRACT_LAST, preferred_element_type=F32)
    vt_ref[...] = vt
    vtb_ref[...] = vt.astype(BF16)


def _proj_bt(x, g, wq_bf, wkt_bf, wvt_bf, seq):
    n = x.shape[0]
    tm = min(ROW_TILE, seq)
    tiles = seq // tm
    row = lambda i: (i, 0)
    fixed = lambda i: (0, 0)
    col = lambda i: (i // tiles, 0, i % tiles)
    feature_major = lambda dtype: jax.ShapeDtypeStruct((n // seq, D_MODEL, seq), dtype)
    return pl.pallas_call(
        _proj_bt_kernel,
        grid=(n // tm,),
        in_specs=[pl.BlockSpec((tm, D_MODEL), row),
                  pl.BlockSpec((1, D_MODEL), fixed),
                  pl.BlockSpec((D_MODEL, D_MODEL), fixed),
                  pl.BlockSpec((D_MODEL, D_MODEL), fixed),
                  pl.BlockSpec((D_MODEL, D_MODEL), fixed)],
        out_specs=[pl.BlockSpec((tm, D_MODEL), row)] + [pl.BlockSpec((None, D_MODEL, tm), col)] * 4,
        out_shape=[jax.ShapeDtypeStruct((n, D_MODEL), BF16),
                   feature_major(F32), feature_major(F32), feature_major(BF16), feature_major(BF16)],
        compiler_params=_params(1),
        name="proj_bt",
    )(x, g, wq_bf, wkt_bf, wvt_bf)


def _strict_lower_ones(n):
    j = lax.broadcasted_iota(jnp.int32, (2 * n, n), 0) % n
    s = lax.broadcasted_iota(jnp.int32, (2 * n, n), 1)
    return jnp.where(j > s, 1.0, 0.0).astype(BF16)


_CONTRACT_LAST = (((1,), (1,)), ((), ()))


def _sb_scores(qh, kh, transposed):
    if transposed:
        return jnp.dot(qh, kh, preferred_element_type=F32)
    return lax.dot_general(qh, kh, _CONTRACT_LAST, preferred_element_type=F32)


def _sb_suffix(z, tri, before):
    softplus = jnp.maximum(z, 0.0) + jnp.log(1.0 + jnp.exp(-jnp.abs(z)))
    log_beta = z - softplus
    if before is not None:
        softplus = jnp.where(before, softplus, 0.0)
    hi = softplus.astype(BF16)
    lo = (softplus - hi.astype(F32)).astype(BF16)
    later = jnp.dot(jnp.concatenate([hi, lo], axis=1), tri, preferred_element_type=F32)
    return log_beta, later, later[:, 0:1] + softplus[:, 0:1]


def _sb_values(log_beta, later, vh, before, transposed):
    a = jnp.exp(log_beta - later)
    if before is not None:
        a = jnp.where(before, a, 0.0)
    if transposed:
        return lax.dot_general(a.astype(BF16), vh, _CONTRACT_LAST, preferred_element_type=F32)
    return jnp.dot(a.astype(BF16), vh, preferred_element_type=F32)


def _head_cols(ref):
    return lambda h: ref[:, h * HEAD_DIM:(h + 1) * HEAD_DIM].astype(BF16)


def _head_rows(ref):
    return lambda h: ref[h * HEAD_DIM:(h + 1) * HEAD_DIM, :].astype(BF16)


def _sb_block(q_ref, k_head, v_head, tri, carry_ref, acc_ref, before, transposed):
    first_head = lax.broadcasted_iota(jnp.int32, carry_ref.shape[1:], 1) < HEAD_DIM
    q_head = _head_cols(q_ref)
    carries = [carry_ref[pair] for pair in range(N_HEADS // 2)]
    z, mid, outs, masses = {}, {}, {}, {}
    for t in range(N_HEADS + 2 * SB_STAGE_SKEW):
        if t < N_HEADS:
            z[t] = _sb_scores(q_head(t), k_head(t), transposed)
        h = t - SB_STAGE_SKEW
        if 0 <= h < N_HEADS:
            log_beta, later, masses[h] = _sb_suffix(z.pop(h), tri, before)
            mid[h] = (log_beta, later)
        h = t - 2 * SB_STAGE_SKEW
        if 0 <= h < N_HEADS:
            outs[h] = _sb_values(*mid.pop(h), v_head(h), before, transposed)
    adds = []
    for pair in range(N_HEADS // 2):
        h = 2 * pair
        adds.append(jnp.exp(-carries[pair]) * jnp.concatenate([outs[h], outs[h + 1]], axis=1))
        carries[pair] = carries[pair] + jnp.where(first_head, masses[h], masses[h + 1])
    acc_ref[...] += jnp.concatenate(adds, axis=1)
    least = carries[0]
    for pair in range(N_HEADS // 2):
        carry_ref[pair] = carries[pair]
        least = jnp.minimum(least, carries[pair])
    return (jnp.min(least) <= SB_DECAY_LIMIT).astype(jnp.int32)


def _sb_old_blocks(n_old, fetch, block, live):
    @pl.when(n_old > 0)
    def _():
        for copy in fetch(1, 0):
            copy.start()

    def cond(state):
        j, live = state
        return jnp.logical_and(j <= n_old, live == 1)

    def body(state):
        j, _ = state
        slot = (j - 1) % 2
        for copy in fetch(j, slot):
            copy.wait()

        @pl.when(j < n_old)
        def _():
            for copy in fetch(j + 1, 1 - slot):
                copy.start()

        return j + 1, block(slot)

    j_end, _ = lax.while_loop(cond, body, (jnp.int32(1), live))

    @pl.when(j_end <= n_old)
    def _():
        for copy in fetch(j_end, (j_end - 1) % 2):
            copy.wait()


def _sb_prompt_kernel(q_ref, kd_ref, vd_ref, k_hbm, v_hbm, o_ref, kbuf, vbuf, sem, acc_ref, carry_ref,
                      *, blocks_per_seq):
    i = pl.program_id(0)
    n = SB_BLOCK
    tri = _strict_lower_ones(n)
    b = i // blocks_per_seq
    qi = i % blocks_per_seq

    def fetch(j, slot):
        cols = pl.ds((qi - j) * n, n)
        return (pltpu.make_async_copy(k_hbm.at[b, :, cols], kbuf.at[slot], sem.at[0, slot]),
                pltpu.make_async_copy(v_hbm.at[b, :, cols], vbuf.at[slot], sem.at[1, slot]))

    def old_block(slot):
        return _sb_block(q_ref, _head_rows(kbuf.at[slot]), _head_rows(vbuf.at[slot]), tri, carry_ref, acc_ref,
                         None, True)

    acc_ref[...] = jnp.zeros_like(acc_ref)
    carry_ref[...] = jnp.zeros_like(carry_ref)
    row = lax.broadcasted_iota(jnp.int32, (n, n), 0)
    col = lax.broadcasted_iota(jnp.int32, (n, n), 1)
    live = _sb_block(q_ref, _head_rows(kd_ref), _head_rows(vd_ref), tri, carry_ref, acc_ref, col < row, True)
    _sb_old_blocks(qi, fetch, old_block, live)
    o_ref[...] = acc_ref[...].astype(BF16)


def _attn_b_prompt(q_bf, kt_bf, vt_bf):
    n = q_bf.shape[0]
    seq = kt_bf.shape[2]
    blk = SB_BLOCK
    bps = seq // blk
    row = lambda i: (i, 0)
    diag = lambda i: (i // bps, 0, i % bps)
    return pl.pallas_call(
        functools.partial(_sb_prompt_kernel, blocks_per_seq=bps),
        grid=(n // blk,),
        in_specs=[pl.BlockSpec((blk, D_MODEL), row),
                  pl.BlockSpec((None, D_MODEL, blk), diag),
                  pl.BlockSpec((None, D_MODEL, blk), diag),
                  pl.BlockSpec(memory_space=pl.ANY),
                  pl.BlockSpec(memory_space=pl.ANY)],
        out_specs=pl.BlockSpec((blk, D_MODEL), row),
        out_shape=jax.ShapeDtypeStruct((n, D_MODEL), BF16),
        scratch_shapes=[pltpu.VMEM((2, D_MODEL, blk), BF16),
                        pltpu.VMEM((2, D_MODEL, blk), BF16),
                        pltpu.SemaphoreType.DMA((2, 2)),
                        pltpu.VMEM((blk, D_MODEL), F32),
                        pltpu.VMEM((N_HEADS // 2, blk, LANES), F32)],
        compiler_params=_params(1),
        name="attn_b_prompt",
    )(q_bf, kt_bf, vt_bf, kt_bf, vt_bf)


def _sb_sample_kernel(q_ref, kn_ref, vn_ref, ck_hbm, cv_hbm, o_ref, kbuf, vbuf, sem, acc_ref, carry_ref,
                      *, n_cache):
    b = pl.program_id(0)
    t = q_ref.shape[0]
    n = SB_BLOCK
    tri = _strict_lower_ones(n)

    def fetch(j, slot):
        cols = pl.ds((n_cache - j) * n, n)
        return (pltpu.make_async_copy(ck_hbm.at[b, :, cols], kbuf.at[slot], sem.at[0, slot]),
                pltpu.make_async_copy(cv_hbm.at[b, :, cols], vbuf.at[slot], sem.at[1, slot]))

    def old_block(slot):
        return _sb_block(q_ref, _head_rows(kbuf.at[slot]), _head_rows(vbuf.at[slot]), tri, carry_ref, acc_ref,
                         None, True)

    acc_ref[...] = jnp.zeros_like(acc_ref)
    carry_ref[...] = jnp.zeros_like(carry_ref)
    row = lax.broadcasted_iota(jnp.int32, (t, t), 0)
    col = lax.broadcasted_iota(jnp.int32, (t, t), 1)
    live = _sb_block(q_ref, _head_cols(kn_ref), _head_cols(vn_ref), _strict_lower_ones(t), carry_ref, acc_ref,
                     col < row, False)
    _sb_old_blocks(jnp.int32(n_cache), fetch, old_block, live)
    o_ref[...] = acc_ref[...].astype(BF16)


def _attn_b_sample(q_bf, k_bf, v_bf, cache_k, cache_v):
    streams, _, past = cache_k.shape
    t = q_bf.shape[0] // streams
    blk = SB_BLOCK
    new = lambda b: (b, 0)
    return pl.pallas_call(
        functools.partial(_sb_sample_kernel, n_cache=past // blk),
        grid=(streams,),
        in_specs=[pl.BlockSpec((t, D_MODEL), new),
                  pl.BlockSpec((t, D_MODEL), new),
                  pl.BlockSpec((t, D_MODEL), new),
                  pl.BlockSpec(memory_space=pl.ANY),
                  pl.BlockSpec(memory_space=pl.ANY)],
        out_specs=pl.BlockSpec((t, D_MODEL), new),
        out_shape=jax.ShapeDtypeStruct(q_bf.shape, BF16),
        scratch_shapes=[pltpu.VMEM((2, D_MODEL, blk), F32),
                        pltpu.VMEM((2, D_MODEL, blk), F32),
                        pltpu.SemaphoreType.DMA((2, 2)),
                        pltpu.VMEM((t, D_MODEL), F32),
                        pltpu.VMEM((N_HEADS // 2, t, LANES), F32)],
        compiler_params=_params(1),
        name="attn_b_sample",
    )(q_bf, k_bf, v_bf, cache_k, cache_v)


ROUTE_ROWS = 48


def _oproj_route_kernel(o_ref, wo_ref, h_ref, g_ref, wr_hi_ref, wr_lo_ref, base_ref,
                        h_out_ref, xn_ref, route_ref, count_ref):
    @pl.when(pl.program_id(0) == 0)
    def _():
        count_ref[...] = base_ref[...]

    h = h_ref[...] + jnp.dot(o_ref[...], wo_ref[...], preferred_element_type=F32)
    h_out_ref[...] = h
    xn = _rmsnorm(h, g_ref[...])
    x_hi = xn.astype(BF16)
    bits = lax.bitcast_convert_type(x_hi.astype(F32), jnp.uint32)
    half = D_MODEL // 2
    xn_ref[...] = (bits[:, :half] >> 16) | bits[:, half:]
    x_lo = (xn - x_hi.astype(F32)).astype(BF16)
    logits = (lax.dot_general(wr_hi_ref[...], x_hi, _CONTRACT_LAST, preferred_element_type=F32)
              + lax.dot_general(wr_lo_ref[...], x_hi, _CONTRACT_LAST, preferred_element_type=F32)
              + lax.dot_general(wr_hi_ref[...], x_lo, _CONTRACT_LAST, preferred_element_type=F32))
    tm = logits.shape[1]
    sub = EXPERTS_PER_GROUP
    row = lax.broadcasted_iota(jnp.int32, (sub, tm), 0)

    def top1(vals):
        peak = jnp.max(vals, axis=0, keepdims=True)
        return peak, jnp.min(jnp.where(vals == peak, row, sub), axis=0, keepdims=True)

    g_mask = row < N_GROUPS
    g_logits = jnp.where(g_mask, logits[:sub], -jnp.inf)
    g_max, g_idx = top1(g_logits)
    g_sum = jnp.sum(jnp.where(g_mask, jnp.exp(g_logits - g_max), 0.0), axis=0, keepdims=True)
    g_p = 1.0 / g_sum

    e_logits = logits[sub:2 * sub]
    for g in range(1, N_GROUPS):
        e_logits = jnp.where(g_idx == g, logits[(g + 1) * sub:(g + 2) * sub], e_logits)
    m1, i1 = top1(e_logits)
    rest = jnp.where(row == i1, -jnp.inf, e_logits)
    m2, i2 = top1(rest)
    e_sum = jnp.sum(jnp.exp(e_logits - m1), axis=0, keepdims=True)
    p1 = 1.0 / e_sum
    p2 = jnp.exp(m2 - m1) / e_sum
    top_sum = p1 + p2
    w1 = g_p * p1 / top_sum
    w2 = g_p * p2 / top_sum
    e1 = g_idx * sub + i1
    e2 = g_idx * sub + i2
    expert = lax.broadcasted_iota(jnp.int32, (N_EXPERTS, tm), 0)
    pick1 = expert == e1
    pick2 = expert == e2
    picks = jnp.where(pick1 | pick2, 1.0, 0.0)
    earlier = jnp.where(lax.broadcasted_iota(jnp.int32, (tm, tm), 0) < lax.broadcasted_iota(jnp.int32, (tm, tm), 1),
                        1.0, 0.0).astype(BF16)
    counts = count_ref[...]
    before = jnp.dot(picks.astype(BF16), earlier, preferred_element_type=F32) + counts[:, 0:1]
    rank1 = jnp.sum(jnp.where(pick1, before, 0.0), axis=0, keepdims=True)
    rank2 = jnp.sum(jnp.where(pick2, before, 0.0), axis=0, keepdims=True)
    count_ref[...] = counts + jnp.sum(picks, axis=1, keepdims=True)
    vals = (e1.astype(F32), e2.astype(F32), w1, w2, rank1, rank2)
    route = jnp.zeros((sub, tm), F32)
    for c, val in enumerate(vals):
        route = jnp.where(row == c, val, route)
    route_ref[...] = route


def _oproj_route(o_bf, wo_bf, h, g, wr_hi, wr_lo, base_counts):
    n = h.shape[0]
    tm = min(ROW_TILE, n)
    row = lambda i: (i, 0)
    fixed = lambda i: (0, 0)
    return pl.pallas_call(
        _oproj_route_kernel,
        grid=(n // tm,),
        in_specs=[pl.BlockSpec((tm, D_MODEL), row),
                  pl.BlockSpec((D_MODEL, D_MODEL), fixed),
                  pl.BlockSpec((tm, D_MODEL), row),
                  pl.BlockSpec((1, D_MODEL), fixed),
                  pl.BlockSpec((ROUTE_ROWS, D_MODEL), fixed),
                  pl.BlockSpec((ROUTE_ROWS, D_MODEL), fixed),
                  pl.BlockSpec((N_EXPERTS, LANES), fixed)],
        out_specs=[pl.BlockSpec((tm, D_MODEL), row),
                   pl.BlockSpec((tm, D_MODEL // 2), row),
                   pl.BlockSpec((EXPERTS_PER_GROUP, tm), lambda i: (0, i)),
                   pl.BlockSpec((N_EXPERTS, LANES), fixed)],
        out_shape=[jax.ShapeDtypeStruct((n, D_MODEL), F32),
                   jax.ShapeDtypeStruct((n, D_MODEL // 2), jnp.uint32),
                   jax.ShapeDtypeStruct((EXPERTS_PER_GROUP, n), F32),
                   jax.ShapeDtypeStruct((N_EXPERTS, LANES), F32)],
        compiler_params=_params(1),
        name="oproj_route",
    )(o_bf, wo_bf, h, g, wr_hi, wr_lo, base_counts)


def _dispatch_kernel(dest_ref, xn_ref, buf_in_ref, buf_ref, sem):
    del buf_in_ref
    i = pl.program_id(0)
    tm = xn_ref.shape[0]
    base = i * (tm * TOP_K)

    def row_copy(t, slot):
        return pltpu.make_async_copy(xn_ref.at[pl.ds(t, 1)], buf_ref.at[pl.ds(slot, 1)], sem)

    def issue(t, carry):
        for k in range(TOP_K):
            row_copy(t, dest_ref[base + TOP_K * t + k]).start(priority=k % 2)
        return carry

    lax.fori_loop(0, tm, issue, 0, unroll=8)
    for k in range(TOP_K):
        pltpu.make_async_copy(xn_ref, buf_ref.at[pl.ds(0, tm)], sem).wait()


def _dispatch(dest, xn, buf):
    n = xn.shape[0]
    tm = min(DISPATCH_TILE, n)
    grid_spec = pltpu.PrefetchScalarGridSpec(
        num_scalar_prefetch=1,
        grid=(n // tm,),
        in_specs=[pl.BlockSpec((tm, xn.shape[1]), lambda i, d: (i, 0)),
                  pl.BlockSpec(memory_space=pl.ANY)],
        out_specs=pl.BlockSpec(memory_space=pl.ANY),
        scratch_shapes=[pltpu.SemaphoreType.DMA(())])
    return pl.pallas_call(
        _dispatch_kernel,
        grid_spec=grid_spec,
        out_shape=jax.ShapeDtypeStruct(buf.shape, buf.dtype),
        input_output_aliases={2: 0},
        compiler_params=_params(1, disable_bounds_checks=True, has_side_effects=True),
        name="moe_dispatch",
    )(dest, xn, buf)


def _expert_kernel(be_ref, nused_ref, x_ref, wg_ref, wu_ref, wd_ref, o_ref, wg_bf, wu_bf, wd_bf):
    i = pl.program_id(0)
    used = i < nused_ref[0]

    @pl.when(used)
    def _():
        @pl.when(jnp.logical_or(i == 0, be_ref[i] != be_ref[jnp.maximum(i - 1, 0)]))
        def _():
            wg_bf[...] = wg_ref[0].astype(BF16)
            wu_bf[...] = wu_ref[0].astype(BF16)
            wd_bf[...] = wd_ref[0].astype(BF16)

        words = x_ref[...]
        low = lax.bitcast_convert_type(words << 16, F32)
        high = lax.bitcast_convert_type(words & jnp.uint32(0xFFFF0000), F32)
        x = jnp.concatenate([low, high], axis=1).astype(BF16)
        a = jnp.dot(x, wg_bf[...], preferred_element_type=F32)
        b = jnp.dot(x, wu_bf[...], preferred_element_type=F32)
        mid = (a * (1.0 / (1.0 + jnp.exp(-a))) * b).astype(BF16)
        o_ref[...] = jnp.dot(mid, wd_bf[...], preferred_element_type=F32)

    @pl.when(jnp.logical_not(used))
    def _():
        o_ref[...] = jnp.zeros_like(o_ref)


def _experts(block_e, n_used, buf, w_gate, w_up, w_down, layer):
    bm = EXPERT_BLOCK
    n_blocks = buf.shape[0] // bm
    blk = lambda i, be, nu: (jnp.minimum(i, nu[0] - 1), 0)
    wsel = lambda i, be, nu: (layer, be[jnp.minimum(i, nu[0] - 1)], 0, 0)
    grid_spec = pltpu.PrefetchScalarGridSpec(
        num_scalar_prefetch=2,
        grid=(n_blocks,),
        in_specs=[pl.BlockSpec((bm, buf.shape[1]), blk),
                  pl.BlockSpec((None, 1, D_MODEL, D_EXPERT), wsel),
                  pl.BlockSpec((None, 1, D_MODEL, D_EXPERT), wsel),
                  pl.BlockSpec((None, 1, D_EXPERT, D_MODEL), wsel)],
        out_specs=pl.BlockSpec((bm, D_MODEL), lambda i, be, nu: (i, 0)),
        scratch_shapes=[pltpu.VMEM((D_MODEL, D_EXPERT), BF16),
                        pltpu.VMEM((D_MODEL, D_EXPERT), BF16),
                        pltpu.VMEM((D_EXPERT, D_MODEL), BF16)])
    return pl.pallas_call(
        _expert_kernel,
        grid_spec=grid_spec,
        out_shape=jax.ShapeDtypeStruct((buf.shape[0], D_MODEL), F32),
        compiler_params=_params(1),
        name="moe_experts",
    )(block_e, n_used, buf, w_gate, w_up, w_down)


def _combine_kernel(dest_ref, h_ref, gate_ref, g_ref, yb_ref, o_ref, rows_ref, sem, *, final_norm):
    i = pl.program_id(0)
    n_steps = pl.num_programs(0)
    tm = h_ref.shape[0]

    def row_copy(step, t, k):
        slot = step % 2
        src = dest_ref[step * (tm * TOP_K) + TOP_K * t + k]
        return pltpu.make_async_copy(yb_ref.at[pl.ds(src, 1)], rows_ref.at[slot, k, pl.ds(t, 1)], sem.at[slot])

    def issue(step):
        def body(t, carry):
            for k in range(TOP_K):
                row_copy(step, t, k).start(priority=k % 2)
            return carry
        lax.fori_loop(0, tm, body, 0, unroll=8)

    @pl.when(i == 0)
    def _():
        issue(i)

    @pl.when(i + 1 < n_steps)
    def _():
        issue(i + 1)

    slot = i % 2
    for k in range(TOP_K):
        pltpu.make_async_copy(yb_ref.at[pl.ds(0, tm)], rows_ref.at[slot, k], sem.at[slot]).wait()

    gates = gate_ref[...]
    out = h_ref[...] + (rows_ref[slot, 0] * gates[:, 0:1] + rows_ref[slot, 1] * gates[:, 1:2])
    if final_norm:
        out = _rmsnorm(out, g_ref[...])
    o_ref[...] = out


def _combine(dest, h, gates, g, yb, final_norm):
    n = h.shape[0]
    tm = min(COMBINE_TILE, n)
    grid_spec = pltpu.PrefetchScalarGridSpec(
        num_scalar_prefetch=1,
        grid=(n // tm,),
        in_specs=[pl.BlockSpec((tm, D_MODEL), lambda i, d: (i, 0)),
                  pl.BlockSpec((tm, TOP_K), lambda i, d: (i, 0)),
                  pl.BlockSpec((1, D_MODEL), lambda i, d: (0, 0)),
                  pl.BlockSpec(memory_space=pl.ANY)],
        out_specs=pl.BlockSpec((tm, D_MODEL), lambda i, d: (i, 0)),
        scratch_shapes=[pltpu.VMEM((2, TOP_K, tm, D_MODEL), F32),
                        pltpu.SemaphoreType.DMA((2,))])
    return pl.pallas_call(
        functools.partial(_combine_kernel, final_norm=final_norm),
        grid_spec=grid_spec,
        out_shape=jax.ShapeDtypeStruct(h.shape, F32),
        compiler_params=_params(1, disable_bounds_checks=True),
        name="moe_combine",
    )(dest, h, gates, g, yb)


def _route_plan(route_p, route_s, counts):
    route = jnp.concatenate([route_p, route_s], axis=1)
    flat_e = route[:TOP_K].T.astype(jnp.int32).reshape(-1)
    rank = route[2 * TOP_K:3 * TOP_K].T.astype(jnp.int32).reshape(-1)
    n_assign = flat_e.shape[0]
    counts = counts[:, 0].astype(jnp.int32)
    bm = EXPERT_BLOCK
    padded = (counts + bm - 1) // bm * bm
    pends = jnp.cumsum(padded)
    onehot = flat_e[:, None] == jnp.arange(N_EXPERTS, dtype=jnp.int32)[None, :]
    dest = jnp.sum(jnp.where(onehot, (pends - padded)[None, :], 0), axis=1) + rank
    n_blocks = -(-n_assign // bm) + N_EXPERTS
    first_rows = jnp.arange(n_blocks, dtype=jnp.int32) * bm
    block_e = jnp.minimum(jnp.sum((pends[None, :] <= first_rows[:, None]).astype(jnp.int32), axis=1), N_EXPERTS - 1)
    n_used = (pends[-1] // bm).astype(jnp.int32).reshape(1)
    return dest.astype(jnp.int32), block_e, n_used, n_blocks


def _moe(hp, hs, route_p, route_s, counts, xn_p, xn_s, g_final, w_gate, w_up, w_down, layer, final_norm):
    dest, block_e, n_used, n_blocks = _route_plan(route_p, route_s, counts)
    n_p = hp.shape[0] * TOP_K
    dest_p, dest_s = dest[:n_p], dest[n_p:]
    buf = jnp.zeros((n_blocks * EXPERT_BLOCK, xn_p.shape[1]), xn_p.dtype)
    buf = _dispatch(dest_p, xn_p, buf)
    buf = _dispatch(dest_s, xn_s, buf)
    yb = _experts(block_e, n_used, buf, w_gate, w_up, w_down, layer)
    out_p = _combine(dest_p, hp, route_p[TOP_K:2 * TOP_K].T, g_final, yb, final_norm)
    out_s = _combine(dest_s, hs, route_s[TOP_K:2 * TOP_K].T, g_final, yb, final_norm)
    return out_p, out_s


def _router_weights(w_group, w_router):
    w_exp = jnp.transpose(w_router, (0, 2, 1)).reshape(N_EXPERTS, D_MODEL)
    zeros = lambda rows: jnp.zeros((rows, D_MODEL), F32)
    w = jnp.concatenate([w_group.T, zeros(EXPERTS_PER_GROUP - N_GROUPS), w_exp,
                         zeros(ROUTE_ROWS - EXPERTS_PER_GROUP - N_EXPERTS)], axis=0)
    hi = w.astype(BF16)
    lo = (w - hi.astype(F32)).astype(BF16)
    return hi, lo


def kernel(x_prompt, x_sample, cache_a_k, cache_a_v, cache_b_k, cache_b_v, norm_mix, norm_ffn, norm_final,
           a_w_qkv, a_w_o, a_sinks, b_w_qkv, b_w_o, moe_w_group, moe_w_router, moe_w_gate, moe_w_up, moe_w_down):
    batch, seq, _ = x_prompt.shape
    streams, t_new, _ = x_sample.shape
    past = cache_b_k.shape[2]
    hp = x_prompt.reshape(batch * seq, D_MODEL)
    hs = x_sample.reshape(streams * t_new, D_MODEL)
    g_final = norm_final.reshape(1, D_MODEL)

    cs_p = _rope_table(jnp.arange(seq, dtype=jnp.int32))
    cs_s = _rope_table(jnp.tile(past + jnp.arange(t_new, dtype=jnp.int32), streams))

    def moe_layer(i, hp, hs, op, os_, w_o, final_norm):
        wo_bf = w_o.astype(BF16)
        g = norm_ffn[i].reshape(1, D_MODEL)
        wr_hi, wr_lo = _router_weights(moe_w_group[i], moe_w_router[i])
        hp, xn_p, route_p, counts = _oproj_route(op, wo_bf, hp, g, wr_hi, wr_lo, jnp.zeros((N_EXPERTS, LANES), F32))
        hs, xn_s, route_s, counts = _oproj_route(os_, wo_bf, hs, g, wr_hi, wr_lo, counts)
        return _moe(hp, hs, route_p, route_s, counts, xn_p, xn_s, g_final,
                    moe_w_gate, moe_w_up, moe_w_down, i, final_norm)

    g0 = norm_mix[0].reshape(1, D_MODEL)
    wa_bf = a_w_qkv[0].astype(BF16)
    qkv_p = _proj_a(hp, g0, wa_bf, cs_p)
    qkv_s = _proj_a(hs, g0, wa_bf, cs_s)
    ck = cache_a_k[0].reshape(streams, WINDOW, A_KV_DIM)
    cv = cache_a_v[0].reshape(streams, WINDOW, A_KV_DIM)
    op = _attn_a_prompt(qkv_p, a_sinks[0], seq)
    os_ = _attn_a_sample(qkv_s, ck, cv, a_sinks[0])
    tail_p = qkv_p.reshape(batch, seq, -1)[:, -WINDOW:, D_MODEL:]
    new_a_k_prompt = tail_p[:, :, :A_KV_DIM].reshape(1, batch, WINDOW, A_KV_HEADS, HEAD_DIM)
    new_a_v_prompt = tail_p[:, :, A_KV_DIM:].reshape(1, batch, WINDOW, A_KV_HEADS, HEAD_DIM)
    k_s = qkv_s[:, D_MODEL:D_MODEL + A_KV_DIM].reshape(streams, t_new, A_KV_HEADS, HEAD_DIM)
    v_s = qkv_s[:, D_MODEL + A_KV_DIM:].reshape(streams, t_new, A_KV_HEADS, HEAD_DIM)
    new_a_k_sample = jnp.concatenate([cache_a_k[0], k_s], axis=1)[None, :, -WINDOW:]
    new_a_v_sample = jnp.concatenate([cache_a_v[0], v_s], axis=1)[None, :, -WINDOW:]
    hp, hs = moe_layer(0, hp, hs, op, os_, a_w_o[0], False)

    g1 = norm_mix[1].reshape(1, D_MODEL)
    wb_bf = b_w_qkv[0].astype(BF16)
    wq_bf = wb_bf[:, :D_MODEL]
    wkt_bf = wb_bf[:, D_MODEL:2 * D_MODEL].T
    wvt_bf = wb_bf[:, 2 * D_MODEL:].T
    q_p, kt_p, vt_p, ktb_p, vtb_p = _proj_bt(hp, g1, wq_bf, wkt_bf, wvt_bf, seq)
    q_s, kf_s, vf_s, kb_s, vb_s = _proj_b(hs, g1, wb_bf)
    op = _attn_b_prompt(q_p, ktb_p, vtb_p)

    def feature_major(cache):
        return jnp.transpose(cache, (0, 1, 3, 4, 2)).reshape(streams, D_MODEL, past)

    def time_major(xt):
        return jnp.transpose(xt.reshape(1, batch, N_HEADS, HEAD_DIM, seq), (0, 1, 4, 2, 3))

    os_ = _attn_b_sample(q_s, kb_s, vb_s, feature_major(cache_b_k), feature_major(cache_b_v))
    new_b_k_prompt = time_major(kt_p)
    new_b_v_prompt = time_major(vt_p)
    new_b_k_sample = kf_s.reshape(1, streams, t_new, N_HEADS, HEAD_DIM)
    new_b_v_sample = vf_s.reshape(1, streams, t_new, N_HEADS, HEAD_DIM)
    hp, hs = moe_layer(1, hp, hs, op, os_, b_w_o[0], True)

    y_prompt = hp.reshape(batch, seq, D_MODEL)
    y_sample = hs.reshape(streams, t_new, D_MODEL)
    return (y_prompt, y_sample, new_a_k_prompt, new_a_v_prompt, new_a_k_sample, new_a_v_sample,
            new_b_k_prompt, new_b_v_prompt, new_b_k_sample, new_b_v_sample)
```
